```python
import math
import jax, jax.numpy as jnp
from jax import lax
import numpy as np


D_MODEL = 2048
BATCH = 2
SEQ = 4096
DEPTH = 2
DEC_BATCH = 32
DEC_SEQ = 32
PAST_LEN = 1024

CHUNK = 64
N_EVEN = (DEPTH + 1) // 2
N_ODD = DEPTH // 2
N_HEADS = 16
HEAD_DIM = 64
V_DIM = 2 * HEAD_DIM
QK_WIDTH = N_HEADS * 2 * HEAD_DIM
QKV_WIDTH = 2 * QK_WIDTH + N_HEADS * V_DIM
ROT_DIM = HEAD_DIM // 4
ROPE_THETA = 500000.0
Q_BLOCK = 128
POOL_WINDOWS = (2, 4, 8, 16)
N_GROUPS = len(POOL_WINDOWS)
GROUP_DIM = D_MODEL // N_GROUPS
POOL_CTX = max(POOL_WINDOWS) - 1
D_FF = 5632
N_EXPERTS = 8
TOP_K = 2
D_FF_EXPERT = 7168
MOE_BLOCK = 256
ALPHA = (2 * DEPTH) ** 0.25
BETA = (8 * DEPTH) ** -0.25
LN_EPS = 1e-5

kernel_name = 'hybrid_diffattn_pool_moe_stream_step'


def layer_norm(x, g, b):
    xf = x.astype(jnp.float32)
    mu = jnp.mean(xf, -1, keepdims=True)
    var = jnp.mean(jnp.square(xf - mu), -1, keepdims=True)
    return ((xf - mu) * lax.rsqrt(var + LN_EPS) * g.astype(jnp.float32) + b.astype(jnp.float32)).astype(x.dtype)


def rms_norm(x, g):
    xf = x.astype(jnp.float32)
    return (xf * lax.rsqrt(jnp.mean(jnp.square(xf), -1, keepdims=True) + LN_EPS) * g.astype(jnp.float32)).astype(x.dtype)


def partial_rope(x, pos):
    half = ROT_DIM // 2
    inv_freq = ROPE_THETA ** (-jnp.arange(0, ROT_DIM, 2, dtype=jnp.float32) / ROT_DIM)
    ang = pos.astype(jnp.float32)[:, None] * inv_freq[None, :]
    cos = jnp.cos(ang)[:, None, None, :]
    sin = jnp.sin(ang)[:, None, None, :]
    xr = x[..., :ROT_DIM].astype(jnp.float32)
    x1, x2 = xr[..., :half], xr[..., half:]
    rot = jnp.concatenate([x1 * cos - x2 * sin, x2 * cos + x1 * sin], -1)
    return jnp.concatenate([rot.astype(x.dtype), x[..., ROT_DIM:]], -1)


def diff_attend(q, k, v, q_pos, k_pos, lam):
    s = jnp.einsum('bqhmd,bkhmd->bhmqk', q, k, preferred_element_type=jnp.float32) * (HEAD_DIM ** -0.5)
    visible = (k_pos[None, :] // CHUNK) <= (q_pos[:, None] // CHUNK)
    p = jax.nn.softmax(jnp.where(visible, s, -jnp.inf), axis=-1)
    a = p[:, :, 0] - lam * p[:, :, 1]
    return jnp.einsum('bhqk,bkhe->bqhe', a.astype(v.dtype), v)


def diff_attn_mixer(x, pos, past_k, past_v, w_in, w_o, lambdas, subln_g, lam_init):
    b, t, _ = x.shape
    qkv = x @ w_in
    q, k, v = jnp.split(qkv, [QK_WIDTH, 2 * QK_WIDTH], axis=-1)
    q = partial_rope(q.reshape(b, t, N_HEADS, 2, HEAD_DIM), pos)
    k = partial_rope(k.reshape(b, t, N_HEADS, 2, HEAD_DIM), pos)
    v = v.reshape(b, t, N_HEADS, V_DIM)
    lf = lambdas.astype(jnp.float32)
    lam = jnp.exp(jnp.sum(lf[0] * lf[1])) - jnp.exp(jnp.sum(lf[2] * lf[3])) + lam_init
    if past_k is None:
        nblk = t // Q_BLOCK
        qb = jnp.moveaxis(q.reshape(b, nblk, Q_BLOCK, N_HEADS, 2, HEAD_DIM), 1, 0)
        pb = pos.reshape(nblk, Q_BLOCK)
        o = lax.map(lambda qp: diff_attend(qp[0], k, v, qp[1], pos, lam), (qb, pb))
        o = jnp.moveaxis(o, 0, 1).reshape(b, t, N_HEADS, V_DIM)
    else:
        p_len = past_k.shape[1]
        keys = jnp.concatenate([past_k.reshape(b, p_len, N_HEADS, 2, HEAD_DIM).astype(k.dtype), k], 1)
        vals = jnp.concatenate([past_v.astype(v.dtype), v], 1)
        k_pos = jnp.arange(p_len + t)
        o = diff_attend(q, keys, vals, pos, k_pos, lam)
    o = rms_norm(o, subln_g) * (1.0 - lam_init)
    y = o.reshape(b, t, N_HEADS * V_DIM) @ w_o
    return y, k.reshape(b, t, N_HEADS, 2 * HEAD_DIM), v


def pool_mixer(x, pos, past_u, w_in, w_grp, scale, w_o):
    b, t, _ = x.shape
    u = x @ w_in
    if past_u is None:
        ctx = jnp.zeros((b, POOL_CTX, D_MODEL), u.dtype)
    else:
        ctx = past_u.astype(u.dtype)
    u_ext = jnp.concatenate([ctx, u], 1)
    c = jnp.cumsum(u_ext.astype(jnp.float32), axis=1)
    c = jnp.concatenate([jnp.zeros((b, 1, D_MODEL), jnp.float32), c], 1)
    c = c.reshape(b, POOL_CTX + 1 + t, N_GROUPS, GROUP_DIM)
    cur = c[:, POOL_CTX + 1:]
    lag = jnp.stack([c[:, POOL_CTX + 1 - w: POOL_CTX + 1 - w + t, g] for g, w in enumerate(POOL_WINDOWS)], axis=2)
    count = jnp.minimum(jnp.asarray(POOL_WINDOWS, jnp.float32)[None, :], (pos[:, None] + 1).astype(jnp.float32))
    d = (cur - lag) / count[None, :, :, None] - u.reshape(b, t, N_GROUPS, GROUP_DIM).astype(jnp.float32)
    z = jnp.einsum('btgc,gce->btge', d.astype(x.dtype), w_grp) * scale.reshape(N_GROUPS, GROUP_DIM)
    return z.reshape(b, t, D_MODEL) @ w_o, u_ext[:, -POOL_CTX:]


def swiglu_dense(x, w_gu, w_down):
    g, u = jnp.split(x @ w_gu, 2, axis=-1)
    return (jax.nn.silu(g) * u) @ w_down


def moe_swiglu(x, w_router, w_gu, w_down):
    b, t, d = x.shape
    n = b * t
    xt = x.reshape(n, d)
    logits = (xt @ w_router).astype(jnp.float32)
    top_val, top_idx = lax.top_k(logits, TOP_K)
    gates = jax.nn.softmax(top_val, axis=-1)
    a = n * TOP_K
    e_flat = top_idx.reshape(a)
    tok_flat = jnp.repeat(jnp.arange(n), TOP_K)
    g_flat = gates.reshape(a)
    order = jnp.argsort(e_flat, stable=True)
    e_sorted = e_flat[order]
    tok_sorted = tok_flat[order]
    counts = jnp.bincount(e_flat, length=N_EXPERTS)
    start = jnp.cumsum(counts) - counts
    padded = ((counts + MOE_BLOCK - 1) // MOE_BLOCK) * MOE_BLOCK
    pend = jnp.cumsum(padded)
    pstart = pend - padded
    dest = pstart[e_sorted] + (jnp.arange(a) - start[e_sorted])
    n_blocks = (a + MOE_BLOCK - 1) // MOE_BLOCK + N_EXPERTS
    buf = jnp.zeros((n_blocks * MOE_BLOCK, d), x.dtype).at[dest].set(xt[tok_sorted])
    block_expert = jnp.clip(jnp.searchsorted(pend, jnp.arange(n_blocks) * MOE_BLOCK, side='right'), 0, N_EXPERTS - 1)

    def expert_block(args):
        xb, e = args
        gb, ub = jnp.split(xb @ w_gu[e], 2, axis=-1)
        return (jax.nn.silu(gb) * ub) @ w_down[e]

    out = lax.map(expert_block, (buf.reshape(n_blocks, MOE_BLOCK, d), block_expert)).reshape(n_blocks * MOE_BLOCK, d)
    y_assign = out[dest] * g_flat[order][:, None].astype(x.dtype)
    y = jax.ops.segment_sum(y_assign, tok_sorted, num_segments=n)
    return y.reshape(b, t, d)


def setup_inputs(seed: int = 0) -> dict:
    key = jax.random.key(seed)
    ks = jax.random.split(key, 24)

    def nrm(k, shape, s):
        return jax.random.normal(k, shape, jnp.float32) * s

    return {
        'x_prompt': nrm(ks[0], (BATCH, SEQ, D_MODEL), 1.0),
        'x_sample': nrm(ks[1], (DEC_BATCH, DEC_SEQ, D_MODEL), 1.0),
        'cache_k': nrm(ks[2], (N_EVEN, DEC_BATCH, PAST_LEN, N_HEADS, 2 * HEAD_DIM), 1.0),
        'cache_v': nrm(ks[3], (N_EVEN, DEC_BATCH, PAST_LEN, N_HEADS, V_DIM), 1.0),
        'state_pool': nrm(ks[4], (N_ODD, DEC_BATCH, POOL_CTX, D_MODEL), 1.0),
        'attn_w_in': nrm(ks[5], (N_EVEN, D_MODEL, QKV_WIDTH), D_MODEL ** -0.5),
        'attn_w_o': nrm(ks[6], (N_EVEN, N_HEADS * V_DIM, D_MODEL), (N_HEADS * V_DIM) ** -0.5 * BETA),
        'attn_lambda': nrm(ks[7], (N_EVEN, 4, HEAD_DIM), 0.1),
        'attn_subln_g': 1.0 + nrm(ks[8], (N_EVEN, V_DIM), 0.1),
        'pool_w_in': nrm(ks[9], (N_ODD, D_MODEL, D_MODEL), D_MODEL ** -0.5),
        'pool_w_grp': nrm(ks[10], (N_ODD, N_GROUPS, GROUP_DIM, GROUP_DIM), GROUP_DIM ** -0.5),
        'pool_scale': 1.0 + nrm(ks[11], (N_ODD, D_MODEL), 0.1),
        'pool_w_o': nrm(ks[12], (N_ODD, D_MODEL, D_MODEL), D_MODEL ** -0.5 * BETA),
        'ln_mix_g': 1.0 + nrm(ks[13], (DEPTH, D_MODEL), 0.1),
        'ln_mix_b': nrm(ks[14], (DEPTH, D_MODEL), 0.01),
        'ln_ffn_g': 1.0 + nrm(ks[15], (DEPTH, D_MODEL), 0.1),
        'ln_ffn_b': nrm(ks[16], (DEPTH, D_MODEL), 0.01),
        'ffn_w_gu': nrm(ks[17], (N_EVEN, D_MODEL, 2 * D_FF), D_MODEL ** -0.5),
        'ffn_w_down': nrm(ks[18], (N_EVEN, D_FF, D_MODEL), D_FF ** -0.5 * BETA),
        'moe_w_router': nrm(ks[19], (N_ODD, D_MODEL, N_EXPERTS), D_MODEL ** -0.5),
        'moe_w_gu': nrm(ks[20], (N_ODD, N_EXPERTS, D_MODEL, 2 * D_FF_EXPERT), D_MODEL ** -0.5),
        'moe_w_down': nrm(ks[21], (N_ODD, N_EXPERTS, D_FF_EXPERT, D_MODEL), D_FF_EXPERT ** -0.5 * BETA),
    }


def reference(x_prompt, x_sample, cache_k, cache_v, state_pool, attn_w_in, attn_w_o, attn_lambda, attn_subln_g,
              pool_w_in, pool_w_grp, pool_scale, pool_w_o, ln_mix_g, ln_mix_b, ln_ffn_g, ln_ffn_b,
              ffn_w_gu, ffn_w_down, moe_w_router, moe_w_gu, moe_w_down):
    pos_p = jnp.arange(x_prompt.shape[1])
    pos_s = cache_k.shape[2] + jnp.arange(x_sample.shape[1])
    xp, xs = x_prompt, x_sample
    kp_l, vp_l, ks_l, vs_l, pp_l, ps_l = [], [], [], [], [], []
    for i in range(DEPTH):
        j = i // 2
        if i % 2 == 0:
            lam_init = 0.8 - 0.6 * math.exp(-0.3 * i)
            mp, kp, vp = diff_attn_mixer(xp, pos_p, None, None, attn_w_in[j], attn_w_o[j], attn_lambda[j], attn_subln_g[j], lam_init)
            ms, kn, vn = diff_attn_mixer(xs, pos_s, cache_k[j], cache_v[j], attn_w_in[j], attn_w_o[j], attn_lambda[j], attn_subln_g[j], lam_init)
            kp_l.append(kp); vp_l.append(vp); ks_l.append(kn); vs_l.append(vn)
        else:
            mp, sp = pool_mixer(xp, pos_p, None, pool_w_in[j], pool_w_grp[j], pool_scale[j], pool_w_o[j])
            ms, sn = pool_mixer(xs, pos_s, state_pool[j], pool_w_in[j], pool_w_grp[j], pool_scale[j], pool_w_o[j])
            pp_l.append(sp); ps_l.append(sn)
        xp = layer_norm(ALPHA * xp + mp, ln_mix_g[i], ln_mix_b[i])
        xs = layer_norm(ALPHA * xs + ms, ln_mix_g[i], ln_mix_b[i])
        if i % 2 == 0:
            fp = swiglu_dense(xp, ffn_w_gu[j], ffn_w_down[j])
            fs = swiglu_dense(xs, ffn_w_gu[j], ffn_w_down[j])
        else:
            fp = moe_swiglu(xp, moe_w_router[j], moe_w_gu[j], moe_w_down[j])
            fs = moe_swiglu(xs, moe_w_router[j], moe_w_gu[j], moe_w_down[j])
        xp = layer_norm(ALPHA * xp + fp, ln_ffn_g[i], ln_ffn_b[i])
        xs = layer_norm(ALPHA * xs + fs, ln_ffn_g[i], ln_ffn_b[i])
    return (xp, xs, jnp.stack(kp_l), jnp.stack(vp_l), jnp.stack(pp_l), jnp.stack(ks_l), jnp.stack(vs_l), jnp.stack(ps_l))
```

```python
import functools
import math

import jax
import jax.numpy as jnp
from jax import lax
from jax.experimental import pallas as pl
from jax.experimental.pallas import tpu as pltpu

CHUNK = 64
HEAD_DIM = 64
V_DIM = 2 * HEAD_DIM
ROT_DIM = HEAD_DIM // 4
ROPE_THETA = 500000.0
POOL_WINDOWS = (2, 4, 8, 16)
POOL_CTX = max(POOL_WINDOWS) - 1
TOP_K = 2
LN_EPS = 1e-5

LANES = 128
V7X_VMEM_REQUEST_CAP = 56 * 1024 * 1024
COMPILER_SCRATCH_ALLOWANCE = 6 * 1024 * 1024

HALO = 16
MOE_BLOCK = 256

_F32 = jnp.float32
_BF16 = jnp.bfloat16


def _nbytes(shape, dtype):
    return math.prod(shape) * jnp.dtype(dtype).itemsize


def _params(semantics, pipelined_bytes, scratch_bytes=0):
    need = 2 * pipelined_bytes + scratch_bytes + COMPILER_SCRATCH_ALLOWANCE
    return pltpu.CompilerParams(dimension_semantics=semantics,
                                vmem_limit_bytes=min(need, V7X_VMEM_REQUEST_CAP))


def _tile(n, pref):
    t = min(n, pref)
    while n % t:
        t //= 2
    return t


def _rope(acc, cos, sin_lo, sin_hi):
    half = ROT_DIM // 2
    pieces = []
    for g in range(acc.shape[1] // LANES):
        xg = acc[:, g * LANES:(g + 1) * LANES]
        pieces.append(xg * cos
                      + pltpu.roll(xg, half, 1) * sin_hi
                      + pltpu.roll(xg, LANES - half, 1) * sin_lo)
    return pieces[0] if len(pieces) == 1 else jnp.concatenate(pieces, axis=1)


def _proj_kernel(*refs, rope, want32, want16):
    x_ref, w_ref = refs[0], refs[1]
    acc = jnp.dot(x_ref[...], w_ref[...], preferred_element_type=_F32)
    pos = 2
    if rope:
        acc = _rope(acc, refs[2][...], refs[3][...], refs[4][...])
        pos = 5
    if want32:
        refs[pos][...] = acc
        pos += 1
    if want16:
        refs[pos][...] = acc.astype(_BF16)


def _proj(xb, w, col0, ncols, tables, want32, want16, name):
    m, k = xb.shape
    tm = _tile(m, 1024)
    tn = _tile(ncols, 512)
    off = col0 // tn
    in_specs = [pl.BlockSpec((tm, k), lambda i, j: (i, 0)),
                pl.BlockSpec((k, tn), lambda i, j: (0, j + off))]
    args = [xb, w]
    blk = _nbytes((tm, k), _BF16) + _nbytes((k, tn), _BF16)
    if tables is not None:
        for t in tables:
            in_specs.append(pl.BlockSpec((tm, LANES), lambda i, j: (i, 0)))
            args.append(t)
            blk += _nbytes((tm, LANES), _F32)
    out_shape, out_specs = [], []
    for want, dt in ((want32, _F32), (want16, _BF16)):
        if want:
            out_shape.append(jax.ShapeDtypeStruct((m, ncols), dt))
            out_specs.append(pl.BlockSpec((tm, tn), lambda i, j: (i, j)))
            blk += _nbytes((tm, tn), dt)
    return pl.pallas_call(
        functools.partial(_proj_kernel, rope=tables is not None, want32=want32, want16=want16),
        grid=(m // tm, ncols // tn),
        in_specs=in_specs, out_specs=out_specs, out_shape=out_shape,
        compiler_params=_params(("parallel", "parallel"), blk, _nbytes((tm, tn), _F32)),
        name=name,
    )(*args)


def _rope_tables(pos):
    half = ROT_DIM // 2
    inv_freq = ROPE_THETA ** (-jnp.arange(0, ROT_DIM, 2, dtype=_F32) / ROT_DIM)
    ang = pos.astype(_F32)[:, None] * inv_freq[None, :]
    cos, sin = jnp.cos(ang), jnp.sin(ang)
    t = pos.shape[0]
    ones = jnp.ones((t, HEAD_DIM - ROT_DIM), _F32)
    zeros = jnp.zeros((t, HEAD_DIM - ROT_DIM), _F32)
    zh = jnp.zeros((t, half), _F32)
    cos_map = jnp.concatenate([cos, cos, ones], 1)
    lo_map = jnp.concatenate([-sin, zh, zeros], 1)
    hi_map = jnp.concatenate([zh, sin, zeros], 1)
    return tuple(jnp.concatenate([a, a], 1) for a in (cos_map, lo_map, hi_map))


def _stack_maps(q):
    lane = lax.broadcasted_iota(jnp.int32, q.shape, 1)
    zero = jnp.zeros_like(q)
    return jnp.concatenate([jnp.where(lane < HEAD_DIM, q, zero),
                            jnp.where(lane >= HEAD_DIM, q, zero)], axis=0)


def _diff_lambda(lam_ref, lam_init):
    lf = lam_ref[...]
    a = jnp.sum(lf[0:1] * lf[1:2], axis=1, keepdims=True)
    b = jnp.sum(lf[2:3] * lf[3:4], axis=1, keepdims=True)
    return jnp.exp(a) - jnp.exp(b) + lam_init


def _diff_finish(acc, l, tq, lam, g, lam_init):
    o = acc / l
    o = o[:tq] - lam * o[tq:]
    ms = jnp.mean(o * o, axis=1, keepdims=True)
    return (o * lax.rsqrt(ms + LN_EPS) * g) * (1.0 - lam_init)


def _attn_prompt_kernel(q_ref, k_ref, v_ref, lam_ref, g_ref, o_ref, *, tq, tk, lam_init):
    q0 = pl.program_id(2) * tq
    qs = _stack_maps(q_ref[...] * jnp.asarray(HEAD_DIM ** -0.5, _BF16))

    def block(kb, carry, masked):
        m, l, acc = carry
        k0 = pl.multiple_of(kb * tk, tk)
        kblk = k_ref[pl.ds(k0, tk), :]
        vblk = v_ref[pl.ds(k0, tk), :]
        s = lax.dot_general(qs, kblk, (((1,), (1,)), ((), ())), preferred_element_type=_F32)
        if masked:
            qc = (q0 + lax.broadcasted_iota(jnp.int32, (tq, tk), 0)) // CHUNK
            kc = (k0 + lax.broadcasted_iota(jnp.int32, (tq, tk), 1)) // CHUNK
            vis = kc <= qc
            s = jnp.where(jnp.concatenate([vis, vis], axis=0), s, -jnp.inf)
        m_new = jnp.maximum(m, jnp.max(s, axis=1, keepdims=True))
        p = jnp.exp(s - m_new)
        alpha = jnp.exp(m - m_new)
        l = alpha * l + jnp.sum(p, axis=1, keepdims=True)
        acc = alpha * acc + jnp.dot(p.astype(_BF16), vblk, preferred_element_type=_F32)
        return m_new, l, acc

    n_full = q0 // tk
    n_tot = (q0 + tq + tk - 1) // tk
    carry = (jnp.full((2 * tq, 1), -jnp.inf, _F32), jnp.zeros((2 * tq, 1), _F32),
             jnp.zeros((2 * tq, V_DIM), _F32))
    carry = lax.fori_loop(0, n_full, lambda kb, c: block(kb, c, False), carry)
    carry = lax.fori_loop(n_full, n_tot, lambda kb, c: block(kb, c, True), carry)
    _, l, acc = carry
    lam = _diff_lambda(lam_ref, lam_init)
    o_ref[...] = _diff_finish(acc, l, tq, lam, g_ref[...], lam_init).astype(o_ref.dtype)


def _attn_prompt(q16, k16, v16, lam, g, batch, seq, lam_init):
    n_heads = q16.shape[1] // LANES
    tq = _tile(seq, 256)
    tk = _tile(seq, 512)
    nq = seq // tq
    blk = (_nbytes((tq, LANES), _BF16) * 2 + 2 * _nbytes((seq, LANES), _BF16))
    return pl.pallas_call(
        functools.partial(_attn_prompt_kernel, tq=tq, tk=tk, lam_init=lam_init),
        grid=(batch, n_heads, nq),
        in_specs=[pl.BlockSpec((tq, LANES), lambda b, h, i: (b * nq + i, h)),
                  pl.BlockSpec((seq, LANES), lambda b, h, i: (b, h)),
                  pl.BlockSpec((seq, LANES), lambda b, h, i: (b, h)),
                  pl.BlockSpec(lam.shape, lambda b, h, i: (0, 0)),
                  pl.BlockSpec((1, V_DIM), lambda b, h, i: (0, 0))],
        out_specs=pl.BlockSpec((tq, LANES), lambda b, h, i: (b * nq + i, h)),
        out_shape=jax.ShapeDtypeStruct((batch * seq, n_heads * V_DIM), _BF16),
        compiler_params=_params(("parallel", "parallel", "parallel"), blk,
                                4 * _nbytes((2 * tq, tk), _F32)),
        name="attn_prompt",
    )(q16, k16, v16, lam, g)


def _attn_sample_kernel(q_ref, kn_ref, vn_ref, kp_ref, vp_ref, lam_ref, g_ref, o_ref, *,
                        t, past, heads, lam_init):
    lam = _diff_lambda(lam_ref, lam_init)
    g = g_ref[...]
    qc = (past + lax.broadcasted_iota(jnp.int32, (t, t), 0)) // CHUNK
    kc = (past + lax.broadcasted_iota(jnp.int32, (t, t), 1)) // CHUNK
    vis = jnp.concatenate([kc <= qc, kc <= qc], axis=0)
    outs = []
    for h in range(heads):
        sl = slice(h * LANES, (h + 1) * LANES)
        qs = _stack_maps(q_ref[:, sl] * jnp.asarray(HEAD_DIM ** -0.5, _BF16))
        kp = kp_ref[:, sl].astype(_BF16)
        vp = vp_ref[:, sl].astype(_BF16)
        dims = (((1,), (1,)), ((), ()))
        s_p = lax.dot_general(qs, kp, dims, preferred_element_type=_F32)
        s_n = lax.dot_general(qs, kn_ref[:, sl], dims, preferred_element_type=_F32)
        s_n = jnp.where(vis, s_n, -jnp.inf)
        m = jnp.maximum(jnp.max(s_p, axis=1, keepdims=True), jnp.max(s_n, axis=1, keepdims=True))
        p_p = jnp.exp(s_p - m)
        p_n = jnp.exp(s_n - m)
        l = jnp.sum(p_p, axis=1, keepdims=True) + jnp.sum(p_n, axis=1, keepdims=True)
        acc = (jnp.dot(p_p.astype(_BF16), vp, preferred_element_type=_F32)
               + jnp.dot(p_n.astype(_BF16), vn_ref[:, sl], preferred_element_type=_F32))
        outs.append(_diff_finish(acc, l, t, lam, g, lam_init))
    o_ref[...] = jnp.concatenate(outs, axis=1).astype(o_ref.dtype)


def _attn_sample(q16, k16, v16, past_k, past_v, lam, g, row0, n_seq, t, lam_init):
    width = q16.shape[1]
    past = past_k.shape[1]
    hw = _tile(width, 4 * LANES)
    r0 = row0 // t
    new_spec = pl.BlockSpec((t, hw), lambda s, j: (r0 + s, j))
    past_spec = pl.BlockSpec((None, past, hw), lambda s, j: (s, 0, j))
    blk = 4 * _nbytes((t, hw), _BF16) + 2 * _nbytes((past, hw), _F32)
    return pl.pallas_call(
        functools.partial(_attn_sample_kernel, t=t, past=past, heads=hw // LANES, lam_init=lam_init),
        grid=(n_seq, width // hw),
        in_specs=[new_spec, new_spec, new_spec, past_spec, past_spec,
                  pl.BlockSpec(lam.shape, lambda s, j: (0, 0)),
                  pl.BlockSpec((1, V_DIM), lambda s, j: (0, 0))],
        out_specs=pl.BlockSpec((t, hw), lambda s, j: (s, j)),
        out_shape=jax.ShapeDtypeStruct((n_seq * t, width), _BF16),
        compiler_params=_params(("parallel", "parallel"), blk, 6 * _nbytes((2 * t, past), _F32)),
        name="attn_sample",
    )(q16, k16, v16, past_k, past_v, lam, g)


def _res_ln(acc, res, g, b, alpha):
    y = alpha * res + acc
    mu = jnp.mean(y, axis=1, keepdims=True)
    yc = y - mu
    var = jnp.mean(yc * yc, axis=1, keepdims=True)
    return yc * lax.rsqrt(var + LN_EPS) * g + b


def _mm_res_ln_kernel(a_ref, w_ref, res_ref, g_ref, b_ref, o32_ref, o16_ref, *scratch, nk, alpha):
    def finish(acc):
        y = _res_ln(acc, res_ref[...], g_ref[...], b_ref[...], alpha)
        o32_ref[...] = y
        o16_ref[...] = y.astype(_BF16)

    part = jnp.dot(a_ref[...], w_ref[...], preferred_element_type=_F32)
    if nk == 1:
        finish(part)
        return
    acc_ref, = scratch
    kk = pl.program_id(1)

    @pl.when(kk == 0)
    def _():
        acc_ref[...] = part

    @pl.when(kk > 0)
    def _():
        acc_ref[...] += part

    @pl.when(kk == nk - 1)
    def _():
        finish(acc_ref[...])


def _mm_res_ln(a16, w16, res, g, b, alpha, name):
    m, k = a16.shape
    d = w16.shape[1]
    tm = _tile(m, 512)
    tk = k if k <= 2048 else _tile(k, 512)
    nk = k // tk
    blk = (_nbytes((tm, tk), _BF16) + _nbytes((tk, d), _BF16) + 2 * _nbytes((tm, d), _F32)
           + _nbytes((tm, d), _BF16))
    scratch = [pltpu.VMEM((tm, d), _F32)] if nk > 1 else []
    return pl.pallas_call(
        functools.partial(_mm_res_ln_kernel, nk=nk, alpha=alpha),
        grid=(m // tm, nk),
        in_specs=[pl.BlockSpec((tm, tk), lambda i, kk: (i, kk)),
                  pl.BlockSpec((tk, d), lambda i, kk: (kk, 0)),
                  pl.BlockSpec((tm, d), lambda i, kk: (i, 0)),
                  pl.BlockSpec((1, d), lambda i, kk: (0, 0)),
                  pl.BlockSpec((1, d), lambda i, kk: (0, 0))],
        out_specs=[pl.BlockSpec((tm, d), lambda i, kk: (i, 0)),
                   pl.BlockSpec((tm, d), lambda i, kk: (i, 0))],
        out_shape=[jax.ShapeDtypeStruct((m, d), _F32), jax.ShapeDtypeStruct((m, d), _BF16)],
        scratch_shapes=scratch,
        compiler_params=_params(("parallel", "arbitrary"), blk, 2 * _nbytes((tm, d), _F32)),
        name=name,
    )(a16, w16, res, g.reshape(1, d), b.reshape(1, d))


def _swiglu(g, u):
    return g * jax.nn.sigmoid(g) * u


def _gate_up_kernel(x_ref, wg_ref, wu_ref, h_ref):
    x = x_ref[...]
    g = jnp.dot(x, wg_ref[...], preferred_element_type=_F32)
    u = jnp.dot(x, wu_ref[...], preferred_element_type=_F32)
    h_ref[...] = _swiglu(g, u).astype(h_ref.dtype)


def _gate_up(x16, w_gu16):
    m, k = x16.shape
    f = w_gu16.shape[1] // 2
    tm = _tile(m, 1024)
    tf = _tile(f, 512)
    nf = f // tf
    blk = _nbytes((tm, k), _BF16) + 2 * _nbytes((k, tf), _BF16) + _nbytes((tm, tf), _BF16)
    return pl.pallas_call(
        _gate_up_kernel,
        grid=(m // tm, nf),
        in_specs=[pl.BlockSpec((tm, k), lambda i, j: (i, 0)),
                  pl.BlockSpec((k, tf), lambda i, j: (0, j)),
                  pl.BlockSpec((k, tf), lambda i, j: (0, nf + j))],
        out_specs=pl.BlockSpec((tm, tf), lambda i, j: (i, j)),
        out_shape=jax.ShapeDtypeStruct((m, f), _BF16),
        compiler_params=_params(("parallel", "parallel"), blk, 3 * _nbytes((tm, tf), _F32)),
        name="ffn_gate_up",
    )(x16, w_gu16, w_gu16)


def _pool_kernel(halo_ref, u_ref, w_ref, scale_ref, z_ref, ext_ref, *, tm, pos0, zero_first):
    i = pl.program_id(1)
    halo = halo_ref[...]
    if zero_first:
        halo = jnp.where(i == 0, jnp.zeros_like(halo), halo)
    ext_ref[0:HALO, :] = halo
    ext_ref[HALO:HALO + tm, :] = u_ref[...]
    pos = pos0 + i * tm + lax.broadcasted_iota(jnp.int32, (tm, 1), 0)
    gd = u_ref.shape[1] // len(POOL_WINDOWS)
    for g, w in enumerate(POOL_WINDOWS):
        cols = slice(g * gd, (g + 1) * gd)
        win = ext_ref[HALO:HALO + tm, cols]
        for back in range(1, w):
            win = win + ext_ref[HALO - back:HALO - back + tm, cols]
        count = jnp.minimum(w, pos + 1).astype(_F32)
        d = win / count - u_ref[:, cols]
        zg = jnp.dot(d.astype(_BF16), w_ref[g], preferred_element_type=_F32)
        z_ref[:, cols] = (zg * scale_ref[:, cols]).astype(z_ref.dtype)


def _pool_call(halo_arr, halo_spec, u, u_spec, w_grp16, scale, grid, tm, pos0, zero_first, out_rows, out_spec, name):
    d = u.shape[1]
    blk = (_nbytes((HALO, d), _F32) + _nbytes((tm, d), _F32) + _nbytes(w_grp16.shape, _BF16)
           + _nbytes((tm, d), _BF16))
    return pl.pallas_call(
        functools.partial(_pool_kernel, tm=tm, pos0=pos0, zero_first=zero_first),
        grid=grid,
        in_specs=[halo_spec, u_spec,
                  pl.BlockSpec(w_grp16.shape, lambda b, i: (0, 0, 0)),
                  pl.BlockSpec((1, d), lambda b, i: (0, 0))],
        out_specs=out_spec,
        out_shape=jax.ShapeDtypeStruct((out_rows, d), _BF16),
        scratch_shapes=[pltpu.VMEM((HALO + tm, d), _F32)],
        compiler_params=_params(("parallel", "arbitrary"), blk, 3 * _nbytes((HALO + tm, d), _F32)),
        name=name,
    )(halo_arr, u, w_grp16, scale.reshape(1, d))


def _pool_prompt(u, w_grp16, scale, batch, seq):
    d = u.shape[1]
    tm = _tile(seq, 256)
    nt = seq // tm
    per = tm // HALO
    halo_spec = pl.BlockSpec((HALO, d), lambda b, i: (jnp.maximum((b * nt + i) * per - 1, 0), 0))
    u_spec = pl.BlockSpec((tm, d), lambda b, i: (b * nt + i, 0))
    out_spec = pl.BlockSpec((tm, d), lambda b, i: (b * nt + i, 0))
    return _pool_call(u, halo_spec, u, u_spec, w_grp16, scale, (batch, nt), tm, 0, True,
                      batch * seq, out_spec, "pool_prompt")


def _pool_sample(u, ctx, w_grp16, scale, row0, n_seq, t, pos0):
    d = u.shape[1]
    r0 = row0 // t
    halo_spec = pl.BlockSpec((None, HALO, d), lambda s, i: (s, 0, 0))
    u_spec = pl.BlockSpec((t, d), lambda s, i: (r0 + s, 0))
    out_spec = pl.BlockSpec((t, d), lambda s, i: (s, 0))
    return _pool_call(ctx, halo_spec, u, u_spec, w_grp16, scale, (n_seq, 1), t, pos0, False,
                      n_seq * t, out_spec, "pool_sample")


def _split_bf16(x):
    hi = x.astype(_BF16)
    lo = (x - hi.astype(_F32)).astype(_BF16)
    return hi, lo


def _router_kernel(x_ref, w_ref, meta_ref, count_ref, carry_ref, *, tm, n_experts):
    i = pl.program_id(0)

    @pl.when(i == 0)
    def _():
        carry_ref[...] = jnp.zeros_like(carry_ref)

    xh, xl = _split_bf16(x_ref[...])
    wh, wl = _split_bf16(w_ref[...])
    logits = (jnp.dot(xh, wh, preferred_element_type=_F32)
              + (jnp.dot(xh, wl, preferred_element_type=_F32)
                 + jnp.dot(xl, wh, preferred_element_type=_F32)))
    lane = lax.broadcasted_iota(jnp.int32, (tm, LANES), 1)
    lg = jnp.where(lane < n_experts, logits, -jnp.inf)
    m1 = jnp.max(lg, axis=1, keepdims=True)
    i1 = jnp.min(jnp.where(lg == m1, lane, LANES), axis=1, keepdims=True)
    lg2 = jnp.where(lane == i1, -jnp.inf, lg)
    m2 = jnp.max(lg2, axis=1, keepdims=True)
    i2 = jnp.min(jnp.where(lg2 == m2, lane, LANES), axis=1, keepdims=True)
    e = jnp.exp(m2 - m1)
    g1 = 1.0 / (1.0 + e)
    g2 = e / (1.0 + e)

    sel1 = lane == i1
    sel2 = lane == i2
    cnt = jnp.where(sel1, 1.0, 0.0) + jnp.where(sel2, 1.0, 0.0)
    row = lax.broadcasted_iota(jnp.int32, (tm, tm), 0)
    col = lax.broadcasted_iota(jnp.int32, (tm, tm), 1)
    lower = jnp.where(col < row, 1.0, 0.0).astype(_BF16)
    before = jnp.dot(lower, cnt.astype(_BF16), preferred_element_type=_F32) + carry_ref[0:1, :]
    r1 = jnp.sum(jnp.where(sel1, before, 0.0), axis=1, keepdims=True)
    r2 = jnp.sum(jnp.where(sel2, before, 0.0), axis=1, keepdims=True)
    carry_ref[0:1, :] = carry_ref[0:1, :] + jnp.sum(cnt, axis=0, keepdims=True)
    count_ref[...] = carry_ref[...]

    meta = jnp.zeros((tm, LANES), _F32)
    for k, v in enumerate((i1.astype(_F32), i2.astype(_F32), r1, r2, g1, g2)):
        meta = jnp.where(lane == k, v, meta)
    meta_ref[...] = meta


def _router(x32, w_router):
    n, d = x32.shape
    n_experts = w_router.shape[1]
    tm = _tile(n, 256)
    w_pad = jnp.pad(w_router, ((0, 0), (0, LANES - n_experts)))
    blk = _nbytes((tm, d), _F32) + _nbytes((d, LANES), _F32) + 2 * _nbytes((tm, LANES), _F32)
    return pl.pallas_call(
        functools.partial(_router_kernel, tm=tm, n_experts=n_experts),
        grid=(n // tm,),
        in_specs=[pl.BlockSpec((tm, d), lambda i: (i, 0)),
                  pl.BlockSpec((d, LANES), lambda i: (0, 0))],
        out_specs=[pl.BlockSpec((tm, LANES), lambda i: (i, 0)),
                   pl.BlockSpec((8, LANES), lambda i: (0, 0))],
        out_shape=[jax.ShapeDtypeStruct((n, LANES), _F32), jax.ShapeDtypeStruct((8, LANES), _F32)],
        scratch_shapes=[pltpu.VMEM((8, LANES), _F32)],
        compiler_params=_params(("arbitrary",), blk, 4 * _nbytes((tm, d), _F32)),
        name="moe_router",
    )(x32, w_pad)


def _row_copy(src_hbm, row, dst_vmem, r, sem):
    return pltpu.make_async_copy(src_hbm.at[pl.ds(row, 1), :], dst_vmem.at[pl.ds(r, 1), :], sem)


def _moe_gather_kernel(tok_ref, x_hbm, o_ref, buf_ref, sem, *, tg):
    base = pl.program_id(0) * tg

    def issue(r, c):
        _row_copy(x_hbm, tok_ref[base + r], buf_ref, r, sem).start()
        return c

    def drain(r, c):
        _row_copy(x_hbm, 0, buf_ref, r, sem).wait()
        return c

    lax.fori_loop(0, tg, issue, 0)
    lax.fori_loop(0, tg, drain, 0)
    o_ref[...] = buf_ref[...].astype(o_ref.dtype)


def _moe_gather(x32, row_tok):
    rows = row_tok.shape[0]
    d = x32.shape[1]
    tg = MOE_BLOCK
    return pl.pallas_call(
        functools.partial(_moe_gather_kernel, tg=tg),
        grid_spec=pltpu.PrefetchScalarGridSpec(
            num_scalar_prefetch=1,
            grid=(rows // tg,),
            in_specs=[pl.BlockSpec(memory_space=pl.ANY)],
            out_specs=pl.BlockSpec((tg, d), lambda i, tok: (i, 0)),
            scratch_shapes=[pltpu.VMEM((tg, d), _F32), pltpu.SemaphoreType.DMA(())]),
        out_shape=jax.ShapeDtypeStruct((rows, d), _BF16),
        compiler_params=_params(("arbitrary",), _nbytes((tg, d), _BF16), 2 * _nbytes((tg, d), _F32)),
        name="moe_gather",
    )(row_tok, x32)


def _weights_changed(be_ref, i):
    return jnp.logical_or(i == 0, be_ref[i] != be_ref[jnp.maximum(i - 1, 0)])


def _moe_gate_up_kernel(be_ref, nu_ref, x_ref, wg_ref, wu_ref, h_ref, wg16_ref, wu16_ref):
    i = pl.program_id(1)

    @pl.when(_weights_changed(be_ref, i))
    def _():
        wg16_ref[...] = wg_ref[...].astype(_BF16)
        wu16_ref[...] = wu_ref[...].astype(_BF16)

    @pl.when(i < nu_ref[0])
    def _():
        x = x_ref[...]
        g = jnp.dot(x, wg16_ref[...], preferred_element_type=_F32)
        u = jnp.dot(x, wu16_ref[...], preferred_element_type=_F32)
        h_ref[...] = _swiglu(g, u).astype(h_ref.dtype)

    @pl.when(i >= nu_ref[0])
    def _():
        h_ref[...] = jnp.zeros_like(h_ref)


def _moe_gate_up(xs16, w_gu, block_expert, n_used):
    rows, d = xs16.shape
    f = w_gu.shape[2] // 2
    tm = MOE_BLOCK
    tf = _tile(f, 512)
    nf = f // tf
    nb = rows // tm
    blk = _nbytes((tm, d), _BF16) + 2 * _nbytes((d, tf), _F32) + _nbytes((tm, tf), _BF16)
    return pl.pallas_call(
        _moe_gate_up_kernel,
        grid_spec=pltpu.PrefetchScalarGridSpec(
            num_scalar_prefetch=2,
            grid=(nf, nb),
            in_specs=[pl.BlockSpec((tm, d), lambda j, i, be, nu: (jnp.minimum(i, nu[0] - 1), 0)),
                      pl.BlockSpec((None, d, tf), lambda j, i, be, nu: (be[i], 0, j)),
                      pl.BlockSpec((None, d, tf), lambda j, i, be, nu: (be[i], 0, nf + j))],
            out_specs=pl.BlockSpec((tm, tf), lambda j, i, be, nu: (i, j)),
            scratch_shapes=[pltpu.VMEM((d, tf), _BF16), pltpu.VMEM((d, tf), _BF16)]),
        out_shape=jax.ShapeDtypeStruct((rows, f), _BF16),
        compiler_params=_params(("arbitrary", "arbitrary"), blk,
                                2 * _nbytes((d, tf), _BF16) + 3 * _nbytes((tm, tf), _F32)),
        name="moe_gate_up",
    )(block_expert, n_used, xs16, w_gu, w_gu)


def _moe_down_kernel(be_ref, nu_ref, h_ref, w_ref, o_ref, w16_ref):
    i = pl.program_id(1)

    @pl.when(_weights_changed(be_ref, i))
    def _():
        w16_ref[...] = w_ref[...].astype(_BF16)

    @pl.when(i < nu_ref[0])
    def _():
        o_ref[...] = jnp.dot(h_ref[...], w16_ref[...], preferred_element_type=_F32)

    @pl.when(i >= nu_ref[0])
    def _():
        o_ref[...] = jnp.zeros_like(o_ref)


def _moe_down(h16, w_down, block_expert, n_used):
    rows, f = h16.shape
    d = w_down.shape[2]
    tm = MOE_BLOCK
    tn = _tile(d, 256)
    nb = rows // tm
    blk = _nbytes((tm, f), _BF16) + _nbytes((f, tn), _F32) + _nbytes((tm, tn), _F32)
    return pl.pallas_call(
        _moe_down_kernel,
        grid_spec=pltpu.PrefetchScalarGridSpec(
            num_scalar_prefetch=2,
            grid=(d // tn, nb),
            in_specs=[pl.BlockSpec((tm, f), lambda j, i, be, nu: (jnp.minimum(i, nu[0] - 1), 0)),
                      pl.BlockSpec((None, f, tn), lambda j, i, be, nu: (be[i], 0, j))],
            out_specs=pl.BlockSpec((tm, tn), lambda j, i, be, nu: (i, j)),
            scratch_shapes=[pltpu.VMEM((f, tn), _BF16)]),
        out_shape=jax.ShapeDtypeStruct((rows, d), _F32),
        compiler_params=_params(("arbitrary", "arbitrary"), blk, _nbytes((f, tn), _BF16)),
        name="moe_down",
    )(block_expert, n_used, h16, w_down)


def _moe_combine_kernel(d1_ref, d2_ref, y_hbm, meta_ref, res_ref, g_ref, b_ref, o32_ref, o16_ref,
                        buf_ref, sem, *, tc, alpha):
    base = pl.program_id(0) * tc

    def issue(r, c):
        _row_copy(y_hbm, d1_ref[base + r], buf_ref.at[0], r, sem).start()
        _row_copy(y_hbm, d2_ref[base + r], buf_ref.at[1], r, sem).start()
        return c

    def drain(r, c):
        _row_copy(y_hbm, 0, buf_ref.at[0], r, sem).wait()
        _row_copy(y_hbm, 0, buf_ref.at[1], r, sem).wait()
        return c

    lax.fori_loop(0, tc, issue, 0)
    lax.fori_loop(0, tc, drain, 0)
    meta = meta_ref[...]
    f = buf_ref[0] * meta[:, 4:5] + buf_ref[1] * meta[:, 5:6]
    y = _res_ln(f, res_ref[...], g_ref[...], b_ref[...], alpha)
    o32_ref[...] = y
    o16_ref[...] = y.astype(_BF16)


def _moe_combine_ln(y_rows, dest1, dest2, meta, res, g, b, alpha):
    n, d = res.shape
    tc = _tile(n, 128)
    blk = _nbytes((tc, LANES), _F32) + 2 * _nbytes((tc, d), _F32) + _nbytes((tc, d), _BF16)
    return pl.pallas_call(
        functools.partial(_moe_combine_kernel, tc=tc, alpha=alpha),
        grid_spec=pltpu.PrefetchScalarGridSpec(
            num_scalar_prefetch=2,
            grid=(n // tc,),
            in_specs=[pl.BlockSpec(memory_space=pl.ANY),
                      pl.BlockSpec((tc, LANES), lambda i, a, c: (i, 0)),
                      pl.BlockSpec((tc, d), lambda i, a, c: (i, 0)),
                      pl.BlockSpec((1, d), lambda i, a, c: (0, 0)),
                      pl.BlockSpec((1, d), lambda i, a, c: (0, 0))],
            out_specs=[pl.BlockSpec((tc, d), lambda i, a, c: (i, 0)),
                       pl.BlockSpec((tc, d), lambda i, a, c: (i, 0))],
            scratch_shapes=[pltpu.VMEM((2, tc, d), _F32), pltpu.SemaphoreType.DMA(())]),
        out_shape=[jax.ShapeDtypeStruct((n, d), _F32), jax.ShapeDtypeStruct((n, d), _BF16)],
        compiler_params=_params(("arbitrary",), blk, 4 * _nbytes((tc, d), _F32)),
        name="moe_combine_ln",
    )(dest1, dest2, y_rows, meta, res, g.reshape(1, d), b.reshape(1, d))


def _moe_layer(x32, w_router, w_gu, w_down, g, b, alpha):
    n, d = x32.shape
    n_experts = w_router.shape[1]
    meta, counts = _router(x32, w_router)
    e1 = meta[:, 0].astype(jnp.int32)
    e2 = meta[:, 1].astype(jnp.int32)
    counts = counts[0, :n_experts].astype(jnp.int32)
    padded = ((counts + MOE_BLOCK - 1) // MOE_BLOCK) * MOE_BLOCK
    pend = jnp.cumsum(padded)
    pstart = pend - padded
    dest1 = pstart[e1] + meta[:, 2].astype(jnp.int32)
    dest2 = pstart[e2] + meta[:, 3].astype(jnp.int32)
    nb = (n * TOP_K) // MOE_BLOCK + n_experts
    rows = nb * MOE_BLOCK
    tok = jnp.arange(n, dtype=jnp.int32)
    row_tok = jnp.zeros((rows,), jnp.int32).at[dest1].set(tok).at[dest2].set(tok)
    n_used = (pend[-1] // MOE_BLOCK).astype(jnp.int32)
    blk_start = jnp.arange(nb, dtype=jnp.int32) * MOE_BLOCK
    block_expert = jnp.clip(jnp.searchsorted(pend, blk_start, side='right'), 0, n_experts - 1)
    block_expert = jnp.where(blk_start < pend[-1], block_expert,
                             block_expert[jnp.maximum(n_used - 1, 0)]).astype(jnp.int32)
    n_used = n_used.reshape(1)

    xs16 = _moe_gather(x32, row_tok)
    h16 = _moe_gate_up(xs16, w_gu, block_expert, n_used)
    y_rows = _moe_down(h16, w_down, block_expert, n_used)
    return _moe_combine_ln(y_rows, dest1, dest2, meta, x32, g, b, alpha)


def kernel(x_prompt, x_sample, cache_k, cache_v, state_pool, attn_w_in, attn_w_o, attn_lambda, attn_subln_g, pool_w_in, pool_w_grp, pool_scale, pool_w_o, ln_mix_g, ln_mix_b, ln_ffn_g, ln_ffn_b, ffn_w_gu, ffn_w_down, moe_w_router, moe_w_gu, moe_w_down):
    batch, seq, d = x_prompt.shape
    n_seq, t, _ = x_sample.shape
    past = cache_k.shape[2]
    depth = ln_mix_g.shape[0]
    alpha = (2 * depth) ** 0.25
    n_p = batch * seq
    n_heads = cache_k.shape[3]
    qk_width = n_heads * 2 * HEAD_DIM

    x32 = jnp.concatenate([x_prompt.reshape(n_p, d), x_sample.reshape(n_seq * t, d)], axis=0)
    x16 = x32.astype(_BF16)
    pos = jnp.concatenate([jnp.tile(jnp.arange(seq), batch), jnp.tile(past + jnp.arange(t), n_seq)])
    tables = _rope_tables(pos)

    kp_l, vp_l, ks_l, vs_l, pp_l, ps_l = [], [], [], [], [], []
    for i in range(depth):
        j = i // 2
        if i % 2 == 0:
            lam_init = 0.8 - 0.6 * math.exp(-0.3 * i)
            w_in16 = attn_w_in[j].astype(_BF16)
            q16, = _proj(x16, w_in16, 0, qk_width, tables, False, True, "proj_q")
            k32, k16 = _proj(x16, w_in16, qk_width, qk_width, tables, True, True, "proj_k")
            v32, v16 = _proj(x16, w_in16, 2 * qk_width, n_heads * V_DIM, None, True, True, "proj_v")
            g = attn_subln_g[j].reshape(1, V_DIM)
            o_p = _attn_prompt(q16, k16, v16, attn_lambda[j], g, batch, seq, lam_init)
            o_s = _attn_sample(q16, k16, v16,
                               cache_k[j].reshape(n_seq, past, qk_width),
                               cache_v[j].reshape(n_seq, past, n_heads * V_DIM),
                               attn_lambda[j], g, n_p, n_seq, t, lam_init)
            mix16 = jnp.concatenate([o_p, o_s], axis=0)
            w_o16 = attn_w_o[j].astype(_BF16)
            kp_l.append(k32[:n_p].reshape(batch, seq, n_heads, 2 * HEAD_DIM))
            vp_l.append(v32[:n_p].reshape(batch, seq, n_heads, V_DIM))
            ks_l.append(k32[n_p:].reshape(n_seq, t, n_heads, 2 * HEAD_DIM))
            vs_l.append(v32[n_p:].reshape(n_seq, t, n_heads, V_DIM))
        else:
            u32, = _proj(x16, pool_w_in[j].astype(_BF16), 0, d, None, True, False, "pool_in")
            w_grp16 = pool_w_grp[j].astype(_BF16)
            ctx = jnp.pad(state_pool[j], ((0, 0), (HALO - POOL_CTX, 0), (0, 0)))
            z_p = _pool_prompt(u32, w_grp16, pool_scale[j], batch, seq)
            z_s = _pool_sample(u32, ctx, w_grp16, pool_scale[j], n_p, n_seq, t, past)
            mix16 = jnp.concatenate([z_p, z_s], axis=0)
            w_o16 = pool_w_o[j].astype(_BF16)
            u_p = u32[:n_p].reshape(batch, seq, d)
            u_s = u32[n_p:].reshape(n_seq, t, d)
            pp_l.append(jnp.concatenate([jnp.zeros((batch, POOL_CTX, d), _F32), u_p], 1)[:, -POOL_CTX:])
            ps_l.append(jnp.concatenate([state_pool[j], u_s], 1)[:, -POOL_CTX:])
        x32, x16 = _mm_res_ln(mix16, w_o16, x32, ln_mix_g[i], ln_mix_b[i], alpha, "mix_out_ln")
        if i % 2 == 0:
            h16 = _gate_up(x16, ffn_w_gu[j].astype(_BF16))
            x32, x16 = _mm_res_ln(h16, ffn_w_down[j].astype(_BF16), x32, ln_ffn_g[i], ln_ffn_b[i],
                                  alpha, "ffn_down_ln")
        else:
            x32, x16 = _moe_layer(x32, moe_w_router[j], moe_w_gu[j], moe_w_down[j],
                                  ln_ffn_g[i], ln_ffn_b[i], alpha)
    y_p = x32[:n_p].reshape(batch, seq, d)
    y_s = x32[n_p:].reshape(n_seq, t, d)
    return (y_p, y_s, jnp.stack(kp_l), jnp.stack(vp_l), jnp.stack(pp_l),
            jnp.stack(ks_l), jnp.stack(vs_l), jnp.stack(ps_l))
```

```python
import functools
import math

import numpy as np
import jax
import jax.numpy as jnp
from jax import lax
from jax.experimental import pallas as pl
from jax.experimental.pallas import tpu as pltpu

CHUNK = 64
HEAD_DIM = 64
V_DIM = 2 * HEAD_DIM
ROT_DIM = HEAD_DIM // 4
ROPE_THETA = 500000.0
POOL_WINDOWS = (2, 4, 8, 16)
POOL_CTX = max(POOL_WINDOWS) - 1
TOP_K = 2
LN_EPS = 1e-5

LANES = 128
V7X_VMEM_REQUEST_CAP = 56 * 1024 * 1024
COMPILER_SCRATCH_ALLOWANCE = 6 * 1024 * 1024

HALO = 16
MOE_BLOCK = 256
Q_SCALE = HEAD_DIM ** -0.5 * math.log2(math.e)

_F32 = jnp.float32
_BF16 = jnp.bfloat16
_NT = (((1,), (1,)), ((), ()))


def _nbytes(shape, dtype):
    return math.prod(shape) * jnp.dtype(dtype).itemsize


def _params(semantics, pipelined_bytes, scratch_bytes=0):
    need = 2 * pipelined_bytes + scratch_bytes + COMPILER_SCRATCH_ALLOWANCE
    return pltpu.CompilerParams(dimension_semantics=semantics,
                                vmem_limit_bytes=min(need, V7X_VMEM_REQUEST_CAP))


def _tile(n, pref):
    t = min(n, pref)
    while n % t:
        t //= 2
    return t


def _rope(acc, cos, sin_lo, sin_hi):
    half = ROT_DIM // 2
    pieces = []
    for g in range(acc.shape[1] // LANES):
        xg = acc[:, g * LANES:(g + 1) * LANES]
        pieces.append(xg * cos
                      + pltpu.roll(xg, half, 1) * sin_hi
                      + pltpu.roll(xg, LANES - half, 1) * sin_lo)
    return pieces[0] if len(pieces) == 1 else jnp.concatenate(pieces, axis=1)


def _proj_kernel(*refs, rope, want32, want16, scale):
    x_ref, w_ref = refs[0], refs[1]
    acc = jnp.dot(x_ref[...], w_ref[...], preferred_element_type=_F32)
    pos = 2
    if rope:
        acc = _rope(acc, refs[2][...], refs[3][...], refs[4][...])
        pos = 5
    if scale is not None:
        acc = acc * scale
    if want32:
        refs[pos][...] = acc
        pos += 1
    if want16:
        refs[pos][...] = acc.astype(_BF16)


def _proj(xb, w, col0, ncols, row0, nrows, tables, want32, want16, name, scale=None):
    k = xb.shape[1]
    period = nrows if tables is None else tables[0].shape[0]
    tm = _tile(math.gcd(math.gcd(row0, nrows), period) if row0 else math.gcd(nrows, period), 1024)
    tn = _tile(ncols, 512)
    off = col0 // tn
    rb0 = row0 // tm
    tper = period // tm
    in_specs = [pl.BlockSpec((tm, k), lambda i, j: (rb0 + i, 0)),
                pl.BlockSpec((k, tn), lambda i, j: (0, j + off))]
    args = [xb, w]
    blk = _nbytes((tm, k), _BF16) + _nbytes((k, tn), _BF16)
    if tables is not None:
        for t in tables:
            in_specs.append(pl.BlockSpec((tm, LANES), lambda i, j: (i % tper, 0)))
            args.append(t)
            blk += _nbytes((tm, LANES), _F32)
    out_shape, out_specs = [], []
    for want, dt in ((want32, _F32), (want16, _BF16)):
        if want:
            out_shape.append(jax.ShapeDtypeStruct((nrows, ncols), dt))
            out_specs.append(pl.BlockSpec((tm, tn), lambda i, j: (i, j)))
            blk += _nbytes((tm, tn), dt)
    return pl.pallas_call(
        functools.partial(_proj_kernel, rope=tables is not None, want32=want32, want16=want16,
                          scale=scale),
        grid=(nrows // tm, ncols // tn),
        in_specs=in_specs, out_specs=out_specs, out_shape=out_shape,
        compiler_params=_params(("parallel", "parallel"), blk, _nbytes((tm, tn), _F32)),
        name=name,
    )(*args)


def _rope_tables(pos):
    half = ROT_DIM // 2
    inv_freq = ROPE_THETA ** (-np.arange(0, ROT_DIM, 2, dtype=np.float64) / ROT_DIM)
    ang = np.asarray(pos, np.float64)[:, None] * inv_freq[None, :]
    cos, sin = np.cos(ang), np.sin(ang)
    t = ang.shape[0]
    ones = np.ones((t, HEAD_DIM - ROT_DIM))
    zeros = np.zeros((t, HEAD_DIM - ROT_DIM))
    zh = np.zeros((t, half))
    cos_map = np.concatenate([cos, cos, ones], 1)
    lo_map = np.concatenate([-sin, zh, zeros], 1)
    hi_map = np.concatenate([zh, sin, zeros], 1)
    return tuple(jnp.asarray(np.concatenate([a, a], 1), _F32) for a in (cos_map, lo_map, hi_map))


def _stack_maps(q):
    lane = lax.broadcasted_iota(jnp.int32, q.shape, 1)
    zero = jnp.zeros_like(q)
    return jnp.concatenate([jnp.where(lane < HEAD_DIM, q, zero),
                            jnp.where(lane >= HEAD_DIM, q, zero)], axis=0)


def _diff_lambda(lam_ref, lam_init):
    lf = lam_ref[...]
    a = jnp.sum(lf[0:1] * lf[1:2], axis=1, keepdims=True)
    b = jnp.sum(lf[2:3] * lf[3:4], axis=1, keepdims=True)
    return jnp.exp(a) - jnp.exp(b) + lam_init


def _diff_finish(acc, l, tq, lam, g, lam_init):
    o = acc / l
    o = o[:tq] - lam * o[tq:]
    ms = jnp.mean(o * o, axis=1, keepdims=True)
    return (o * lax.rsqrt(ms + LN_EPS) * g) * (1.0 - lam_init)


def _attn_prompt_kernel(q_ref, k_ref, v_ref, lam_ref, g_ref, o_ref, *, tq, tk, lam_init):
    q0 = pl.program_id(2) * tq
    qs = _stack_maps(q_ref[...])
    n_chains = 1
    rows = 2 * tq // n_chains
    q_chunks = [qs[c * rows:(c + 1) * rows] for c in range(n_chains)]

    def block(kb, carry, masked):
        k0 = pl.multiple_of(kb * tk, tk)
        kblk = k_ref[pl.ds(k0, tk), :]
        vblk = v_ref[pl.ds(k0, tk), :]
        if masked:
            qc = (q0 + lax.broadcasted_iota(jnp.int32, (tq, tk), 0)) // CHUNK
            kc = (k0 + lax.broadcasted_iota(jnp.int32, (tq, tk), 1)) // CHUNK
            vis = kc <= qc
            vis = jnp.concatenate([vis, vis], axis=0)
        out = []
        for c in range(n_chains):
            m, l, acc = carry[c]
            s = lax.dot_general(q_chunks[c], kblk, _NT, preferred_element_type=_F32)
            if masked:
                s = jnp.where(vis[c * rows:(c + 1) * rows], s, -jnp.inf)
            m_new = jnp.maximum(m, jnp.max(s, axis=1, keepdims=True))
            p = jnp.exp2(s - m_new)
            alpha = jnp.exp2(m - m_new)
            l = alpha * l + jnp.sum(p, axis=1, keepdims=True)
            acc = alpha * acc + jnp.dot(p.astype(_BF16), vblk, preferred_element_type=_F32)
            out.append((m_new, l, acc))
        return tuple(out)

    n_full = q0 // tk
    n_tot = (q0 + tq + tk - 1) // tk
    carry = tuple((jnp.full((rows, 1), -jnp.inf, _F32), jnp.zeros((rows, 1), _F32),
                   jnp.zeros((rows, V_DIM), _F32)) for _ in range(n_chains))
    carry = lax.fori_loop(0, n_full, lambda kb, c: block(kb, c, False), carry)
    carry = lax.fori_loop(n_full, n_tot, lambda kb, c: block(kb, c, True), carry)
    l = jnp.concatenate([c[1] for c in carry], axis=0)
    acc = jnp.concatenate([c[2] for c in carry], axis=0)
    lam = _diff_lambda(lam_ref, lam_init)
    o_ref[...] = _diff_finish(acc, l, tq, lam, g_ref[...], lam_init).astype(o_ref.dtype)


def _attn_prompt(q16, k16, v16, lam, g, batch, seq, lam_init):
    n_heads = q16.shape[1] // LANES
    tq = _tile(seq, 256)
    tk = _tile(seq, 512)
    assert tk % tq == 0
    nq = seq // tq
    blk = (_nbytes((tq, LANES), _BF16) * 2 + 2 * _nbytes((seq, LANES), _BF16))
    return pl.pallas_call(
        functools.partial(_attn_prompt_kernel, tq=tq, tk=tk, lam_init=lam_init),
        grid=(batch, n_heads, nq),
        in_specs=[pl.BlockSpec((tq, LANES), lambda b, h, i: (b * nq + i, h)),
                  pl.BlockSpec((seq, LANES), lambda b, h, i: (b, h)),
                  pl.BlockSpec((seq, LANES), lambda b, h, i: (b, h)),
                  pl.BlockSpec(lam.shape, lambda b, h, i: (0, 0)),
                  pl.BlockSpec((1, V_DIM), lambda b, h, i: (0, 0))],
        out_specs=pl.BlockSpec((tq, LANES), lambda b, h, i: (b * nq + i, h)),
        out_shape=jax.ShapeDtypeStruct((batch * seq, n_heads * V_DIM), _BF16),
        compiler_params=_params(("parallel", "parallel", "parallel"), blk,
                                6 * _nbytes((2 * tq, tk), _F32)),
        name="attn_prompt",
    )(q16, k16, v16, lam, g)


def _attn_sample_kernel(q_ref, kn_ref, vn_ref, kp_ref, vp_ref, lam_ref, g_ref, o_ref, *,
                        t, past, heads, lam_init):
    lam = _diff_lambda(lam_ref, lam_init)
    g = g_ref[...]
    qc = (past + lax.broadcasted_iota(jnp.int32, (t, t), 0)) // CHUNK
    kc = (past + lax.broadcasted_iota(jnp.int32, (t, t), 1)) // CHUNK
    vis = jnp.concatenate([kc <= qc, kc <= qc], axis=0)
    outs = []
    for h in range(heads):
        sl = slice(h * LANES, (h + 1) * LANES)
        qs = _stack_maps(q_ref[:, sl])
        kp = kp_ref[:, sl].astype(_BF16)
        vp = vp_ref[:, sl].astype(_BF16)
        s_p = lax.dot_general(qs, kp, _NT, preferred_element_type=_F32)
        s_n = lax.dot_general(qs, kn_ref[:, sl], _NT, preferred_element_type=_F32)
        s_n = jnp.where(vis, s_n, -jnp.inf)
        m = jnp.maximum(jnp.max(s_p, axis=1, keepdims=True), jnp.max(s_n, axis=1, keepdims=True))
        p_p = jnp.exp2(s_p - m)
        p_n = jnp.exp2(s_n - m)
        l = jnp.sum(p_p, axis=1, keepdims=True) + jnp.sum(p_n, axis=1, keepdims=True)
        acc = (jnp.dot(p_p.astype(_BF16), vp, preferred_element_type=_F32)
               + jnp.dot(p_n.astype(_BF16), vn_ref[:, sl], preferred_element_type=_F32))
        outs.append(_diff_finish(acc, l, t, lam, g, lam_init))
    o_ref[...] = jnp.concatenate(outs, axis=1).astype(o_ref.dtype)


def _attn_sample(q16, k16, v16, past_k, past_v, layer, lam, g, n_seq, t, lam_init):
    width = q16.shape[1]
    past = past_k.shape[1]
    hw = _tile(width, 4 * LANES)
    s0 = layer * n_seq
    new_spec = pl.BlockSpec((t, hw), lambda s, j: (s, j))
    past_spec = pl.BlockSpec((None, past, hw), lambda s, j: (s0 + s, 0, j))
    blk = 4 * _nbytes((t, hw), _BF16) + 2 * _nbytes((past, hw), _F32)
    return pl.pallas_call(
        functools.partial(_attn_sample_kernel, t=t, past=past, heads=hw // LANES, lam_init=lam_init),
        grid=(n_seq, width // hw),
        in_specs=[new_spec, new_spec, new_spec, past_spec, past_spec,
                  pl.BlockSpec(lam.shape, lambda s, j: (0, 0)),
                  pl.BlockSpec((1, V_DIM), lambda s, j: (0, 0))],
        out_specs=new_spec,
        out_shape=jax.ShapeDtypeStruct((n_seq * t, width), _BF16),
        compiler_params=_params(("parallel", "parallel"), blk, 6 * _nbytes((2 * t, past), _F32)),
        name="attn_sample",
    )(q16, k16, v16, past_k, past_v, lam, g)


def _res_ln(acc, res, g, b, alpha):
    y = alpha * res + acc
    mu = jnp.mean(y, axis=1, keepdims=True)
    yc = y - mu
    var = jnp.mean(yc * yc, axis=1, keepdims=True)
    return yc * lax.rsqrt(var + LN_EPS) * g + b


def _mm_res_ln_kernel(*refs, nk, alpha, part_starts):
    n_parts = len(part_starts)
    a_refs = refs[:n_parts]
    w_ref, res_ref, g_ref, b_ref, o32_ref, o16_ref = refs[n_parts:n_parts + 6]
    scratch = refs[n_parts + 6:]

    def finish(acc):
        y = _res_ln(acc, res_ref[...], g_ref[...], b_ref[...], alpha)
        o32_ref[...] = y
        o16_ref[...] = y.astype(_BF16)

    a = a_refs[0][...]
    for p in range(1, n_parts):
        a = jnp.where(pl.program_id(0) >= part_starts[p], a_refs[p][...], a)
    part = jnp.dot(a, w_ref[...], preferred_element_type=_F32)
    if nk == 1:
        finish(part)
        return
    acc_ref, = scratch
    kk = pl.program_id(1)

    @pl.when(kk == 0)
    def _():
        acc_ref[...] = part

    @pl.when(kk > 0)
    def _():
        acc_ref[...] += part

    @pl.when(kk == nk - 1)
    def _():
        finish(acc_ref[...])


def _mm_res_ln(parts, w16, res, g, b, alpha, name):
    m = sum(a.shape[0] for a in parts)
    k = parts[0].shape[1]
    d = w16.shape[1]
    tm = _tile(functools.reduce(math.gcd, [a.shape[0] for a in parts]), 512)
    tk = k if k <= 2048 else _tile(k, 512)
    nk = k // tk
    blk = (len(parts) * _nbytes((tm, tk), _BF16) + _nbytes((tk, d), _BF16) + 2 * _nbytes((tm, d), _F32)
           + _nbytes((tm, d), _BF16))
    scratch = [pltpu.VMEM((tm, d), _F32)] if nk > 1 else []
    part_starts, part_specs, start = [], [], 0
    for a in parts:
        nblk = a.shape[0] // tm
        part_starts.append(start)
        part_specs.append(pl.BlockSpec(
            (tm, tk), lambda i, kk, start=start, nblk=nblk: (jnp.clip(i - start, 0, nblk - 1), kk)))
        start += nblk
    return pl.pallas_call(
        functools.partial(_mm_res_ln_kernel, nk=nk, alpha=alpha, part_starts=tuple(part_starts)),
        grid=(m // tm, nk),
        in_specs=part_specs + [
                  pl.BlockSpec((tk, d), lambda i, kk: (kk, 0)),
                  pl.BlockSpec((tm, d), lambda i, kk: (i, 0)),
                  pl.BlockSpec((1, d), lambda i, kk: (0, 0)),
                  pl.BlockSpec((1, d), lambda i, kk: (0, 0))],
        out_specs=[pl.BlockSpec((tm, d), lambda i, kk: (i, 0)),
                   pl.BlockSpec((tm, d), lambda i, kk: (i, 0))],
        out_shape=[jax.ShapeDtypeStruct((m, d), _F32), jax.ShapeDtypeStruct((m, d), _BF16)],
        scratch_shapes=scratch,
        compiler_params=_params(("parallel", "arbitrary"), blk, 2 * _nbytes((tm, d), _F32)),
        name=name,
    )(*parts, w16, res, g.reshape(1, d), b.reshape(1, d))


def _swiglu(g, u):
    return g * jax.nn.sigmoid(g) * u


def _gate_up_kernel(x_ref, wg_ref, wu_ref, h_ref):
    x = x_ref[...]
    g = jnp.dot(x, wg_ref[...], preferred_element_type=_F32)
    u = jnp.dot(x, wu_ref[...], preferred_element_type=_F32)
    h_ref[...] = _swiglu(g, u).astype(h_ref.dtype)


def _gate_up(x16, w_gu16):
    m, k = x16.shape
    f = w_gu16.shape[1] // 2
    tm = _tile(m, 1024)
    tf = _tile(f, 512)
    nf = f // tf
    blk = _nbytes((tm, k), _BF16) + 2 * _nbytes((k, tf), _BF16) + _nbytes((tm, tf), _BF16)
    return pl.pallas_call(
        _gate_up_kernel,
        grid=(m // tm, nf),
        in_specs=[pl.BlockSpec((tm, k), lambda i, j: (i, 0)),
                  pl.BlockSpec((k, tf), lambda i, j: (0, j)),
                  pl.BlockSpec((k, tf), lambda i, j: (0, nf + j))],
        out_specs=pl.BlockSpec((tm, tf), lambda i, j: (i, j)),
        out_shape=jax.ShapeDtypeStruct((m, f), _BF16),
        compiler_params=_params(("parallel", "parallel"), blk, 3 * _nbytes((tm, tf), _F32)),
        name="ffn_gate_up",
    )(x16, w_gu16, w_gu16)


def _pool_kernel(halo_ref, u_ref, w_ref, scale_ref, z_ref, ext_ref, *, tm, pos0, zero_first):
    i = pl.program_id(1)
    halo = halo_ref[...]
    if zero_first:
        halo = jnp.where(i == 0, jnp.zeros_like(halo), halo)
    ext_ref[0:HALO, :] = halo
    ext_ref[HALO:HALO + tm, :] = u_ref[...]
    pos = pos0 + i * tm + lax.broadcasted_iota(jnp.int32, (tm, 1), 0)
    gd = u_ref.shape[1] // len(POOL_WINDOWS)
    for g, w in enumerate(POOL_WINDOWS):
        cols = slice(g * gd, (g + 1) * gd)
        win = ext_ref[HALO:HALO + tm, cols]
        for back in range(1, w):
            win = win + ext_ref[HALO - back:HALO - back + tm, cols]
        count = jnp.minimum(w, pos + 1).astype(_F32)
        d = win / count - u_ref[:, cols]
        zg = jnp.dot(d.astype(_BF16), w_ref[g], preferred_element_type=_F32)
        z_ref[:, cols] = (zg * scale_ref[:, cols]).astype(z_ref.dtype)


def _pool_call(halo_arr, halo_spec, u, u_spec, w_grp16, scale, grid, tm, pos0, zero_first,
               out_rows, out_spec, name):
    d = u.shape[1]
    blk = (_nbytes((HALO, d), _F32) + _nbytes((tm, d), _F32) + _nbytes(w_grp16.shape, _BF16)
           + _nbytes((tm, d), _BF16))
    return pl.pallas_call(
        functools.partial(_pool_kernel, tm=tm, pos0=pos0, zero_first=zero_first),
        grid=grid,
        in_specs=[halo_spec, u_spec,
                  pl.BlockSpec(w_grp16.shape, lambda b, i: (0, 0, 0)),
                  pl.BlockSpec((1, d), lambda b, i: (0, 0))],
        out_specs=out_spec,
        out_shape=jax.ShapeDtypeStruct((out_rows, d), _BF16),
        scratch_shapes=[pltpu.VMEM((HALO + tm, d), _F32)],
        compiler_params=_params(("parallel", "arbitrary"), blk, 3 * _nbytes((HALO + tm, d), _F32)),
        name=name,
    )(halo_arr, u, w_grp16, scale.reshape(1, d))


def _pool_prompt(u, w_grp16, scale, batch, seq):
    d = u.shape[1]
    tm = _tile(seq, 256)
    nt = seq // tm
    per = tm // HALO
    halo_spec = pl.BlockSpec((HALO, d), lambda b, i: (jnp.maximum((b * nt + i) * per - 1, 0), 0))
    u_spec = pl.BlockSpec((tm, d), lambda b, i: (b * nt + i, 0))
    out_spec = pl.BlockSpec((tm, d), lambda b, i: (b * nt + i, 0))
    return _pool_call(u, halo_spec, u, u_spec, w_grp16, scale, (batch, nt), tm, 0, True,
                      batch * seq, out_spec, "pool_prompt")


def _pool_sample(u, ctx, layer, w_grp16, scale, row0, n_seq, t, pos0):
    d = u.shape[1]
    r0 = row0 // t
    s0 = layer * n_seq
    halo_spec = pl.BlockSpec((None, HALO, d), lambda s, i: (s0 + s, 0, 0))
    u_spec = pl.BlockSpec((t, d), lambda s, i: (r0 + s, 0))
    out_spec = pl.BlockSpec((t, d), lambda s, i: (s, 0))
    return _pool_call(ctx, halo_spec, u, u_spec, w_grp16, scale, (n_seq, 1), t, pos0, False,
                      n_seq * t, out_spec, "pool_sample")


def _split_bf16(x):
    hi = x.astype(_BF16)
    lo = (x - hi.astype(_F32)).astype(_BF16)
    return hi, lo


def _router_kernel(x_ref, w_ref, meta_ref, count_ref, carry_ref, *, tm, n_experts):
    i = pl.program_id(0)

    @pl.when(i == 0)
    def _():
        carry_ref[...] = jnp.zeros_like(carry_ref)

    xh, xl = _split_bf16(x_ref[...])
    wh, wl = _split_bf16(w_ref[...])
    logits = (jnp.dot(xh, wh, preferred_element_type=_F32)
              + (jnp.dot(xh, wl, preferred_element_type=_F32)
                 + jnp.dot(xl, wh, preferred_element_type=_F32)))
    lane = lax.broadcasted_iota(jnp.int32, (tm, LANES), 1)
    lg = jnp.where(lane < n_experts, logits, -jnp.inf)
    m1 = jnp.max(lg, axis=1, keepdims=True)
    i1 = jnp.min(jnp.where(lg == m1, lane, LANES), axis=1, keepdims=True)
    lg2 = jnp.where(lane == i1, -jnp.inf, lg)
    m2 = jnp.max(lg2, axis=1, keepdims=True)
    i2 = jnp.min(jnp.where(lg2 == m2, lane, LANES), axis=1, keepdims=True)
    e = jnp.exp(m2 - m1)
    g1 = 1.0 / (1.0 + e)
    g2 = e / (1.0 + e)

    sel1 = lane == i1
    sel2 = lane == i2
    cnt = jnp.where(sel1, 1.0, 0.0) + jnp.where(sel2, 1.0, 0.0)
    row = lax.broadcasted_iota(jnp.int32, (tm, tm), 0)
    col = lax.broadcasted_iota(jnp.int32, (tm, tm), 1)
    lower = jnp.where(col < row, 1.0, 0.0).astype(_BF16)
    before = jnp.dot(lower, cnt.astype(_BF16), preferred_element_type=_F32) + carry_ref[0:1, :]
    r1 = jnp.sum(jnp.where(sel1, before, 0.0), axis=1, keepdims=True)
    r2 = jnp.sum(jnp.where(sel2, before, 0.0), axis=1, keepdims=True)
    carry_ref[0:1, :] = carry_ref[0:1, :] + jnp.sum(cnt, axis=0, keepdims=True)
    count_ref[...] = carry_ref[...]

    meta = jnp.zeros((tm, LANES), _F32)
    for k, v in enumerate((i1.astype(_F32), i2.astype(_F32), r1, r2, g1, g2)):
        meta = jnp.where(lane == k, v, meta)
    meta_ref[...] = meta


def _router(x32, w_router):
    n, d = x32.shape
    n_experts = w_router.shape[1]
    tm = _tile(n, 256)
    w_pad = jnp.pad(w_router, ((0, 0), (0, LANES - n_experts)))
    blk = _nbytes((tm, d), _F32) + _nbytes((d, LANES), _F32) + 2 * _nbytes((tm, LANES), _F32)
    return pl.pallas_call(
        functools.partial(_router_kernel, tm=tm, n_experts=n_experts),
        grid=(n // tm,),
        in_specs=[pl.BlockSpec((tm, d), lambda i: (i, 0)),
                  pl.BlockSpec((d, LANES), lambda i: (0, 0))],
        out_specs=[pl.BlockSpec((tm, LANES), lambda i: (i, 0)),
                   pl.BlockSpec((8, LANES), lambda i: (0, 0))],
        out_shape=[jax.ShapeDtypeStruct((n, LANES), _F32), jax.ShapeDtypeStruct((8, LANES), _F32)],
        scratch_shapes=[pltpu.VMEM((8, LANES), _F32)],
        compiler_params=_params(("arbitrary",), blk, 4 * _nbytes((tm, d), _F32)),
        name="moe_router",
    )(x32, w_pad)


def _row_copy(src_hbm, row, dst_vmem, r, sem):
    return pltpu.make_async_copy(src_hbm.at[pl.ds(row, 1), :], dst_vmem.at[pl.ds(r, 1), :], sem)


def _gather_rows(src_hbm, idx_ref, base, dst_vmem, sem, n):
    def issue(r, c):
        _row_copy(src_hbm, idx_ref[base + r], dst_vmem, r, sem).start()
        return c
    lax.fori_loop(0, n, issue, 0, unroll=4)


def _wait_rows(src_hbm, dst_vmem, sem, n):
    def drain(r, c):
        _row_copy(src_hbm, 0, dst_vmem, r, sem).wait()
        return c
    lax.fori_loop(0, n, drain, 0, unroll=8)


def _moe_gather_kernel(tok_ref, nu_ref, x_hbm, xs_hbm, in_ref, out_ref, in_sem, out_sem, *, tg):
    n_used = nu_ref[0]

    def out_copy(b, slot):
        return pltpu.make_async_copy(out_ref.at[slot], xs_hbm.at[pl.ds(pl.multiple_of(b * tg, tg), tg), :],
                                     out_sem.at[slot])

    _gather_rows(x_hbm, tok_ref, 0, in_ref.at[0], in_sem.at[0], tg)

    def body(b, c):
        slot = b % 2

        @pl.when(b + 1 < n_used)
        def _():
            _gather_rows(x_hbm, tok_ref, (b + 1) * tg, in_ref.at[1 - slot], in_sem.at[1 - slot], tg)

        _wait_rows(x_hbm, in_ref.at[slot], in_sem.at[slot], tg)

        @pl.when(b >= 2)
        def _():
            out_copy(b - 2, slot).wait()

        out_ref[slot] = in_ref[slot].astype(out_ref.dtype)
        out_copy(b, slot).start()
        return c

    lax.fori_loop(0, n_used, body, 0)

    @pl.when(n_used >= 2)
    def _():
        out_copy(n_used - 2, n_used % 2).wait()

    out_copy(n_used - 1, (n_used - 1) % 2).wait()

    n_blocks = xs_hbm.shape[0] // tg
    out_ref[0] = jnp.zeros(out_ref.shape[1:], out_ref.dtype)

    def start_zero(b, c):
        out_copy(b, 0).start()
        return c

    def wait_zero(b, c):
        out_copy(b, 0).wait()
        return c

    lax.fori_loop(n_used, n_blocks, start_zero, 0)
    lax.fori_loop(n_used, n_blocks, wait_zero, 0)


def _moe_gather(x32, row_tok, n_used):
    rows = row_tok.shape[0]
    d = x32.shape[1]
    tg = MOE_BLOCK
    return pl.pallas_call(
        functools.partial(_moe_gather_kernel, tg=tg),
        grid_spec=pltpu.PrefetchScalarGridSpec(
            num_scalar_prefetch=2,
            grid=(1,),
            in_specs=[pl.BlockSpec(memory_space=pl.ANY)],
            out_specs=pl.BlockSpec(memory_space=pl.ANY),
            scratch_shapes=[pltpu.VMEM((2, tg, d), _F32), pltpu.VMEM((2, tg, d), _BF16),
                            pltpu.SemaphoreType.DMA((2,)), pltpu.SemaphoreType.DMA((2,))]),
        out_shape=jax.ShapeDtypeStruct((rows, d), _BF16),
        compiler_params=_params(("arbitrary",), 0, 2 * _nbytes((tg, d), _F32) + 2 * _nbytes((tg, d), _BF16)),
        name="moe_gather",
    )(row_tok, n_used, x32)


def _grouped_rows(gs_ref, gb_ref, e, src_hbm, dst_hbm, in_ref, out_ref, in_sem, out_sem, col, compute, tm):
    r0 = gs_ref[e]
    nblk = gb_ref[e]

    def rows(b):
        return pl.ds(pl.multiple_of(r0 + b * tm, tm), tm)

    def in_copy(b, slot):
        return pltpu.make_async_copy(src_hbm.at[rows(b), :], in_ref.at[slot], in_sem.at[slot])

    def out_copy(b, slot):
        return pltpu.make_async_copy(out_ref.at[slot], dst_hbm.at[rows(b), col], out_sem.at[slot])

    @pl.when(nblk > 0)
    def _():
        in_copy(0, 0).start()

    def body(b, c):
        slot = b % 2
        in_copy(b, slot).wait()

        @pl.when(b + 1 < nblk)
        def _():
            in_copy(b + 1, 1 - slot).start()

        @pl.when(b >= 2)
        def _():
            out_copy(b - 2, slot).wait()

        out_ref[slot] = compute(in_ref[slot])
        out_copy(b, slot).start()
        return c

    lax.fori_loop(0, nblk, body, 0)

    @pl.when(nblk >= 2)
    def _():
        out_copy(nblk - 2, nblk % 2).wait()

    @pl.when(nblk >= 1)
    def _():
        out_copy(nblk - 1, (nblk - 1) % 2).wait()


def _zero_rows(gs_ref, gb_ref, e, dst_hbm, out_ref, out_sem, col, tm):
    r0 = gs_ref[e]
    out_ref[0] = jnp.zeros(out_ref.shape[1:], out_ref.dtype)

    def copy(b):
        return pltpu.make_async_copy(out_ref.at[0], dst_hbm.at[pl.ds(pl.multiple_of(r0 + b * tm, tm), tm), col],
                                     out_sem.at[0])

    def start(b, c):
        copy(b).start()
        return c

    def wait(b, c):
        copy(b).wait()
        return c

    lax.fori_loop(0, gb_ref[e], start, 0)
    lax.fori_loop(0, gb_ref[e], wait, 0)


def _moe_gate_up_kernel(gs_ref, gb_ref, xs_hbm, wg_ref, wu_ref, h_hbm, wg16_ref, wu16_ref,
                        in_ref, out_ref, in_sem, out_sem, *, tm, tf, n_experts):
    j = pl.program_id(0)
    e = pl.program_id(1)
    col = pl.ds(pl.multiple_of(j * tf, tf), tf)

    def compute(x):
        g = jnp.dot(x, wg16_ref[...], preferred_element_type=_F32)
        u = jnp.dot(x, wu16_ref[...], preferred_element_type=_F32)
        return _swiglu(g, u).astype(_BF16)

    @pl.when(e < n_experts)
    def _():
        wg16_ref[...] = wg_ref[...].astype(_BF16)
        wu16_ref[...] = wu_ref[...].astype(_BF16)
        _grouped_rows(gs_ref, gb_ref, e, xs_hbm, h_hbm, in_ref, out_ref, in_sem, out_sem, col, compute, tm)

    @pl.when(e == n_experts)
    def _():
        _zero_rows(gs_ref, gb_ref, e, h_hbm, out_ref, out_sem, col, tm)


def _moe_gate_up(xs16, w_gu, gstart, gblocks):
    rows, d = xs16.shape
    n_experts = w_gu.shape[0]
    f = w_gu.shape[2] // 2
    tm = MOE_BLOCK
    tf = _tile(f, 512)
    nf = f // tf
    blk = 2 * _nbytes((d, tf), _F32)
    scratch_bytes = (2 * _nbytes((d, tf), _BF16) + 2 * _nbytes((tm, d), _BF16) + 2 * _nbytes((tm, tf), _BF16)
                     + 3 * _nbytes((tm, tf), _F32))
    last = n_experts - 1
    return pl.pallas_call(
        functools.partial(_moe_gate_up_kernel, tm=tm, tf=tf, n_experts=n_experts),
        grid_spec=pltpu.PrefetchScalarGridSpec(
            num_scalar_prefetch=2,
            grid=(nf, n_experts + 1),
            in_specs=[pl.BlockSpec(memory_space=pl.ANY),
                      pl.BlockSpec((None, d, tf), lambda j, e, gs, gb: (jnp.minimum(e, last), 0, j)),
                      pl.BlockSpec((None, d, tf), lambda j, e, gs, gb: (jnp.minimum(e, last), 0, nf + j))],
            out_specs=pl.BlockSpec(memory_space=pl.ANY),
            scratch_shapes=[pltpu.VMEM((d, tf), _BF16), pltpu.VMEM((d, tf), _BF16),
                            pltpu.VMEM((2, tm, d), _BF16), pltpu.VMEM((2, tm, tf), _BF16),
                            pltpu.SemaphoreType.DMA((2,)), pltpu.SemaphoreType.DMA((2,))]),
        out_shape=jax.ShapeDtypeStruct((rows, f), _BF16),
        compiler_params=_params(("arbitrary", "arbitrary"), blk, scratch_bytes),
        name="moe_gate_up",
    )(gstart, gblocks, xs16, w_gu, w_gu)


def _moe_down_kernel(gs_ref, gb_ref, h_hbm, w_ref, y_hbm, w16_ref, in_ref, out_ref, in_sem, out_sem,
                     *, tm, tn, n_experts):
    j = pl.program_id(0)
    e = pl.program_id(1)
    col = pl.ds(pl.multiple_of(j * tn, tn), tn)

    def compute(h):
        return jnp.dot(h, w16_ref[...], preferred_element_type=_F32)

    @pl.when(e < n_experts)
    def _():
        w16_ref[...] = w_ref[...].astype(_BF16)
        _grouped_rows(gs_ref, gb_ref, e, h_hbm, y_hbm, in_ref, out_ref, in_sem, out_sem, col, compute, tm)

    @pl.when(e == n_experts)
    def _():
        _zero_rows(gs_ref, gb_ref, e, y_hbm, out_ref, out_sem, col, tm)


def _moe_down(h16, w_down, gstart, gblocks):
    rows, f = h16.shape
    n_experts = w_down.shape[0]
    d = w_down.shape[2]
    tm = MOE_BLOCK
    tn = _tile(d, 512)
    blk = _nbytes((f, tn), _F32)
    scratch_bytes = (_nbytes((f, tn), _BF16) + 2 * _nbytes((tm, f), _BF16) + 3 * _nbytes((tm, tn), _F32))
    last = n_experts - 1
    return pl.pallas_call(
        functools.partial(_moe_down_kernel, tm=tm, tn=tn, n_experts=n_experts),
        grid_spec=pltpu.PrefetchScalarGridSpec(
            num_scalar_prefetch=2,
            grid=(d // tn, n_experts + 1),
            in_specs=[pl.BlockSpec(memory_space=pl.ANY),
                      pl.BlockSpec((None, f, tn), lambda j, e, gs, gb: (jnp.minimum(e, last), 0, j))],
            out_specs=pl.BlockSpec(memory_space=pl.ANY),
            scratch_shapes=[pltpu.VMEM((f, tn), _BF16),
                            pltpu.VMEM((2, tm, f), _BF16), pltpu.VMEM((2, tm, tn), _F32),
                            pltpu.SemaphoreType.DMA((2,)), pltpu.SemaphoreType.DMA((2,))]),
        out_shape=jax.ShapeDtypeStruct((rows, d), _F32),
        compiler_params=_params(("arbitrary", "arbitrary"), blk, scratch_bytes),
        name="moe_down",
    )(gstart, gblocks, h16, w_down)


def _moe_combine_kernel(d1_ref, d2_ref, y_hbm, meta_ref, res_ref, g_ref, b_ref, *rest, tc, alpha, n_split):
    outs, (buf_ref, sem) = rest[:-2], rest[-2:]
    i = pl.program_id(0)
    slot = i % 2

    def fetch(step, s):
        _gather_rows(y_hbm, d1_ref, step * tc, buf_ref.at[s, 0], sem.at[s], tc)
        _gather_rows(y_hbm, d2_ref, step * tc, buf_ref.at[s, 1], sem.at[s], tc)

    @pl.when(i == 0)
    def _():
        fetch(0, 0)

    @pl.when(i + 1 < pl.num_programs(0))
    def _():
        fetch(i + 1, 1 - slot)

    _wait_rows(y_hbm, buf_ref.at[slot, 0], sem.at[slot], tc)
    _wait_rows(y_hbm, buf_ref.at[slot, 1], sem.at[slot], tc)
    meta = meta_ref[...]
    f = buf_ref[slot, 0] * meta[:, 4:5] + buf_ref[slot, 1] * meta[:, 5:6]
    y = _res_ln(f, res_ref[...], g_ref[...], b_ref[...], alpha)
    if n_split is None:
        outs[0][...] = y
        outs[1][...] = y.astype(_BF16)
    else:
        @pl.when(i < n_split)
        def _():
            outs[0][...] = y

        @pl.when(i >= n_split)
        def _():
            outs[1][...] = y


def _moe_combine_ln(y_rows, dest1, dest2, meta, res, g, b, alpha, split_rows):
    n, d = res.shape
    tc = _tile(n if split_rows is None else math.gcd(n, split_rows), 128)
    blk = _nbytes((tc, LANES), _F32) + 2 * _nbytes((tc, d), _F32) + _nbytes((tc, d), _BF16)
    tile_spec = pl.BlockSpec((tc, d), lambda i, a, c: (i, 0))
    if split_rows is None:
        n_split = None
        out_specs = [tile_spec, tile_spec]
        out_shape = [jax.ShapeDtypeStruct((n, d), _F32), jax.ShapeDtypeStruct((n, d), _BF16)]
    else:
        n_split = split_rows // tc
        out_specs = [pl.BlockSpec((tc, d), lambda i, a, c: (jnp.minimum(i, n_split - 1), 0)),
                     pl.BlockSpec((tc, d), lambda i, a, c: (jnp.maximum(i - n_split, 0), 0))]
        out_shape = [jax.ShapeDtypeStruct((split_rows, d), _F32),
                     jax.ShapeDtypeStruct((n - split_rows, d), _F32)]
    return pl.pallas_call(
        functools.partial(_moe_combine_kernel, tc=tc, alpha=alpha, n_split=n_split),
        grid_spec=pltpu.PrefetchScalarGridSpec(
            num_scalar_prefetch=2,
            grid=(n // tc,),
            in_specs=[pl.BlockSpec(memory_space=pl.ANY),
                      pl.BlockSpec((tc, LANES), lambda i, a, c: (i, 0)),
                      tile_spec,
                      pl.BlockSpec((1, d), lambda i, a, c: (0, 0)),
                      pl.BlockSpec((1, d), lambda i, a, c: (0, 0))],
            out_specs=out_specs,
            scratch_shapes=[pltpu.VMEM((2, 2, tc, d), _F32), pltpu.SemaphoreType.DMA((2,))]),
        out_shape=out_shape,
        compiler_params=_params(("arbitrary",), blk, 8 * _nbytes((tc, d), _F32)),
        name="moe_combine_ln",
    )(dest1, dest2, y_rows, meta, res, g.reshape(1, d), b.reshape(1, d))


def _moe_layer(x32, w_router, w_gu, w_down, g, b, alpha, split_rows):
    n, d = x32.shape
    n_experts = w_router.shape[1]
    meta, counts = _router(x32, w_router)
    e1 = meta[:, 0].astype(jnp.int32)
    e2 = meta[:, 1].astype(jnp.int32)
    counts = counts[0, :n_experts].astype(jnp.int32)
    gblocks = (counts + MOE_BLOCK - 1) // MOE_BLOCK
    gend = jnp.cumsum(gblocks) * MOE_BLOCK
    gstart = gend - gblocks * MOE_BLOCK
    dest1 = gstart[e1] + meta[:, 2].astype(jnp.int32)
    dest2 = gstart[e2] + meta[:, 3].astype(jnp.int32)
    nb = (n * TOP_K) // MOE_BLOCK + n_experts
    rows = nb * MOE_BLOCK
    tok = jnp.arange(n, dtype=jnp.int32)
    row_tok = jnp.zeros((rows,), jnp.int32).at[dest1].set(tok).at[dest2].set(tok)
    n_used = (gend[-1] // MOE_BLOCK).astype(jnp.int32).reshape(1)
    gstart = jnp.concatenate([gstart, gend[-1:]]).astype(jnp.int32)
    gblocks = jnp.concatenate([gblocks, nb - n_used]).astype(jnp.int32)

    xs16 = _moe_gather(x32, row_tok, n_used)
    h16 = _moe_gate_up(xs16, w_gu, gstart, gblocks)
    y_rows = _moe_down(h16, w_down, gstart, gblocks)
    return _moe_combine_ln(y_rows, dest1, dest2, meta, x32, g, b, alpha, split_rows)


def kernel(x_prompt, x_sample, cache_k, cache_v, state_pool, attn_w_in, attn_w_o, attn_lambda, attn_subln_g, pool_w_in, pool_w_grp, pool_scale, pool_w_o, ln_mix_g, ln_mix_b, ln_ffn_g, ln_ffn_b, ffn_w_gu, ffn_w_down, moe_w_router, moe_w_gu, moe_w_down):
    batch, seq, d = x_prompt.shape
    n_seq, t, _ = x_sample.shape
    past = cache_k.shape[2]
    depth = ln_mix_g.shape[0]
    alpha = (2 * depth) ** 0.25
    n_p = batch * seq
    n_s = n_seq * t
    n = n_p + n_s
    n_heads = cache_k.shape[3]
    qk_width = n_heads * 2 * HEAD_DIM
    v_width = n_heads * V_DIM

    x32 = jnp.concatenate([x_prompt.reshape(n_p, d), x_sample.reshape(n_s, d)], axis=0)
    x16 = x32.astype(_BF16)
    tab_p = _rope_tables(np.arange(seq))
    tab_s = _rope_tables(np.tile(past + np.arange(t), n_seq))
    past_k = cache_k.reshape(-1, past, qk_width)
    past_v = cache_v.reshape(-1, past, v_width)
    ctx = jnp.pad(state_pool, ((0, 0), (0, 0), (HALO - POOL_CTX, 0), (0, 0))).reshape(-1, HALO, d)

    kp_l, vp_l, ks_l, vs_l, pp_l, ps_l = [], [], [], [], [], []
    y_p = y_s = None
    for i in range(depth):
        j = i // 2
        last = i == depth - 1
        if i % 2 == 0:
            lam_init = 0.8 - 0.6 * math.exp(-0.3 * i)
            w_in16 = attn_w_in[j].astype(_BF16)
            g = attn_subln_g[j].reshape(1, V_DIM)
            rows_p, rows_s = (0, n_p, tab_p), (n_p, n_s, tab_s)
            (q_p,), (q_s,) = (_proj(x16, w_in16, 0, qk_width, r0, nr, tab, False, True, "proj_q", Q_SCALE)
                              for r0, nr, tab in (rows_p, rows_s))
            (k32_p, k_p), (k32_s, k_s) = (_proj(x16, w_in16, qk_width, qk_width, r0, nr, tab, True, True, "proj_k")
                                          for r0, nr, tab in (rows_p, rows_s))
            (v32_p, v_p), (v32_s, v_s) = (_proj(x16, w_in16, 2 * qk_width, v_width, r0, nr, None, True, True, "proj_v")
                                          for r0, nr, tab in (rows_p, rows_s))
            mix = [_attn_prompt(q_p, k_p, v_p, attn_lambda[j], g, batch, seq, lam_init),
                   _attn_sample(q_s, k_s, v_s, past_k, past_v, j, attn_lambda[j], g, n_seq, t, lam_init)]
            w_o16 = attn_w_o[j].astype(_BF16)
            kp_l.append(k32_p.reshape(batch, seq, n_heads, 2 * HEAD_DIM))
            vp_l.append(v32_p.reshape(batch, seq, n_heads, V_DIM))
            ks_l.append(k32_s.reshape(n_seq, t, n_heads, 2 * HEAD_DIM))
            vs_l.append(v32_s.reshape(n_seq, t, n_heads, V_DIM))
        else:
            u32, = _proj(x16, pool_w_in[j].astype(_BF16), 0, d, 0, n, None, True, False, "pool_in")
            w_grp16 = pool_w_grp[j].astype(_BF16)
            mix = [_pool_prompt(u32, w_grp16, pool_scale[j], batch, seq),
                   _pool_sample(u32, ctx, j, w_grp16, pool_scale[j], n_p, n_seq, t, past)]
            w_o16 = pool_w_o[j].astype(_BF16)
            u_p = u32[:n_p].reshape(batch, seq, d)
            u_s = u32[n_p:].reshape(n_seq, t, d)
            pp_l.append(jnp.concatenate([jnp.zeros((batch, POOL_CTX, d), _F32), u_p[:, -POOL_CTX:]], 1)[:, -POOL_CTX:])
            ps_l.append(jnp.concatenate([state_pool[j], u_s[:, -POOL_CTX:]], 1)[:, -POOL_CTX:])
        x32, x16 = _mm_res_ln(mix, w_o16, x32, ln_mix_g[i], ln_mix_b[i], alpha, "mix_out_ln")
        if i % 2 == 0:
            h16 = _gate_up(x16, ffn_w_gu[j].astype(_BF16))
            x32, x16 = _mm_res_ln([h16], ffn_w_down[j].astype(_BF16), x32, ln_ffn_g[i], ln_ffn_b[i],
                                  alpha, "ffn_down_ln")
        elif last:
            y_p, y_s = _moe_layer(x32, moe_w_router[j], moe_w_gu[j], moe_w_down[j],
                                  ln_ffn_g[i], ln_ffn_b[i], alpha, n_p)
        else:
            x32, x16 = _moe_layer(x32, moe_w_router[j], moe_w_gu[j], moe_w_down[j],
                                  ln_ffn_g[i], ln_ffn_b[i], alpha, None)
    if y_p is None:
        y_p, y_s = x32[:n_p], x32[n_p:]
    return (y_p.reshape(batch, seq, d), y_s.reshape(n_seq, t, d), jnp.stack(kp_l), jnp.stack(vp_l),
            jnp.stack(pp_l), jnp.stack(ks_l), jnp.stack(vs_l), jnp.stack(ps_l))
```

```python
import functools
import math

import numpy as np
import jax
import jax.numpy as jnp
from jax import lax
from jax.experimental import pallas as pl
from jax.experimental.pallas import tpu as pltpu

CHUNK = 64
HEAD_DIM = 64
V_DIM = 2 * HEAD_DIM
ROT_DIM = HEAD_DIM // 4
ROPE_THETA = 500000.0
POOL_WINDOWS = (2, 4, 8, 16)
POOL_CTX = max(POOL_WINDOWS) - 1
TOP_K = 2
LN_EPS = 1e-5

LANES = 128
V7X_VMEM_REQUEST_CAP = 56 * 1024 * 1024
COMPILER_SCRATCH_ALLOWANCE = 6 * 1024 * 1024

HALO = 16
MOE_BLOCK = 256
MOE_CHUNK_BLOCKS = 4
Q_SCALE = HEAD_DIM ** -0.5 * math.log2(math.e)

_F32 = jnp.float32
_BF16 = jnp.bfloat16
_NT = (((1,), (1,)), ((), ()))


def _nbytes(shape, dtype):
    return math.prod(shape) * jnp.dtype(dtype).itemsize


def _params(semantics, pipelined_bytes, scratch_bytes=0):
    need = 2 * pipelined_bytes + scratch_bytes + COMPILER_SCRATCH_ALLOWANCE
    return pltpu.CompilerParams(dimension_semantics=semantics,
                                vmem_limit_bytes=min(need, V7X_VMEM_REQUEST_CAP))


def _tile(n, pref):
    t = min(n, pref)
    while n % t:
        t //= 2
    return t


def _rope(acc, cos, sin_lo, sin_hi):
    half = ROT_DIM // 2
    pieces = []
    for g in range(acc.shape[1] // LANES):
        xg = acc[:, g * LANES:(g + 1) * LANES]
        pieces.append(xg * cos
                      + pltpu.roll(xg, half, 1) * sin_hi
                      + pltpu.roll(xg, LANES - half, 1) * sin_lo)
    return pieces[0] if len(pieces) == 1 else jnp.concatenate(pieces, axis=1)


def _proj_kernel(*refs, rope, want32, want16, scale):
    x_ref, w_ref = refs[0], refs[1]
    acc = jnp.dot(x_ref[...], w_ref[...], preferred_element_type=_F32)
    pos = 2
    if rope:
        acc = _rope(acc, refs[2][...], refs[3][...], refs[4][...])
        pos = 5
    if scale is not None:
        acc = acc * scale
    if want32:
        refs[pos][...] = acc
        pos += 1
    if want16:
        refs[pos][...] = acc.astype(_BF16)


def _proj(xb, w, col0, ncols, row0, nrows, tables, want32, want16, name, scale=None):
    k = xb.shape[1]
    period = nrows if tables is None else tables[0].shape[0]
    tm = _tile(math.gcd(math.gcd(row0, nrows), period) if row0 else math.gcd(nrows, period), 1024)
    tn = _tile(ncols, 512)
    off = col0 // tn
    rb0 = row0 // tm
    tper = period // tm
    in_specs = [pl.BlockSpec((tm, k), lambda i, j: (rb0 + i, 0)),
                pl.BlockSpec((k, tn), lambda i, j: (0, j + off))]
    args = [xb, w]
    blk = _nbytes((tm, k), _BF16) + _nbytes((k, tn), _BF16)
    if tables is not None:
        for t in tables:
            in_specs.append(pl.BlockSpec((tm, LANES), lambda i, j: (i % tper, 0)))
            args.append(t)
            blk += _nbytes((tm, LANES), _F32)
    out_shape, out_specs = [], []
    for want, dt in ((want32, _F32), (want16, _BF16)):
        if want:
            out_shape.append(jax.ShapeDtypeStruct((nrows, ncols), dt))
            out_specs.append(pl.BlockSpec((tm, tn), lambda i, j: (i, j)))
            blk += _nbytes((tm, tn), dt)
    return pl.pallas_call(
        functools.partial(_proj_kernel, rope=tables is not None, want32=want32, want16=want16,
                          scale=scale),
        grid=(nrows // tm, ncols // tn),
        in_specs=in_specs, out_specs=out_specs, out_shape=out_shape,
        compiler_params=_params(("parallel", "parallel"), blk, _nbytes((tm, tn), _F32)),
        name=name,
    )(*args)


def _rope_tables(pos):
    half = ROT_DIM // 2
    inv_freq = ROPE_THETA ** (-np.arange(0, ROT_DIM, 2, dtype=np.float64) / ROT_DIM)
    ang = np.asarray(pos, np.float64)[:, None] * inv_freq[None, :]
    cos, sin = np.cos(ang), np.sin(ang)
    t = ang.shape[0]
    ones = np.ones((t, HEAD_DIM - ROT_DIM))
    zeros = np.zeros((t, HEAD_DIM - ROT_DIM))
    zh = np.zeros((t, half))
    cos_map = np.concatenate([cos, cos, ones], 1)
    lo_map = np.concatenate([-sin, zh, zeros], 1)
    hi_map = np.concatenate([zh, sin, zeros], 1)
    return tuple(jnp.asarray(np.concatenate([a, a], 1), _F32) for a in (cos_map, lo_map, hi_map))


def _stack_maps(q):
    lane = lax.broadcasted_iota(jnp.int32, q.shape, 1)
    zero = jnp.zeros_like(q)
    return jnp.concatenate([jnp.where(lane < HEAD_DIM, q, zero),
                            jnp.where(lane >= HEAD_DIM, q, zero)], axis=0)


def _diff_lambda(lam_ref, lam_init):
    lf = lam_ref[...]
    a = jnp.sum(lf[0:1] * lf[1:2], axis=1, keepdims=True)
    b = jnp.sum(lf[2:3] * lf[3:4], axis=1, keepdims=True)
    return jnp.exp(a) - jnp.exp(b) + lam_init


def _diff_finish(acc, l, tq, lam, g, lam_init):
    o = acc / l
    o = o[:tq] - lam * o[tq:]
    ms = jnp.mean(o * o, axis=1, keepdims=True)
    return (o * lax.rsqrt(ms + LN_EPS) * g) * (1.0 - lam_init)


def _attn_prompt_kernel(q_ref, k_ref, v_ref, lam_ref, g_ref, o_ref, sa_ref, sb_ref, *, tq, tk, lam_init):
    q0 = pl.program_id(2) * tq
    qs = _stack_maps(q_ref[...])
    n_full = q0 // tk

    def scores(kb, dst_ref):
        k0 = pl.multiple_of(kb * tk, tk)
        dst_ref[...] = lax.dot_general(qs, k_ref[pl.ds(k0, tk), :], _NT, preferred_element_type=_F32)

    def consume(kb, src_ref, carry, masked):
        m, l, acc = carry
        k0 = pl.multiple_of(kb * tk, tk)
        s = src_ref[...]
        if masked:
            qc = (q0 + lax.broadcasted_iota(jnp.int32, (tq, tk), 0)) // CHUNK
            kc = (k0 + lax.broadcasted_iota(jnp.int32, (tq, tk), 1)) // CHUNK
            vis = kc <= qc
            s = jnp.where(jnp.concatenate([vis, vis], axis=0), s, -jnp.inf)
        m_new = jnp.maximum(m, jnp.max(s, axis=1, keepdims=True))
        p = jnp.exp2(s - m_new)
        alpha = jnp.exp2(m - m_new)
        l = alpha * l + jnp.sum(p, axis=1, keepdims=True)
        acc = alpha * acc + jnp.dot(p.astype(_BF16), v_ref[pl.ds(k0, tk), :], preferred_element_type=_F32)
        return m_new, l, acc

    def pair(p, carry):
        kb = 2 * p
        scores(kb + 1, sb_ref)
        carry = consume(kb, sa_ref, carry, False)
        scores(kb + 2, sa_ref)
        return consume(kb + 1, sb_ref, carry, False)

    def odd_tail(carry):
        scores(n_full, sb_ref)
        carry = consume(n_full - 1, sa_ref, carry, False)
        return consume(n_full, sb_ref, carry, True)

    def even_tail(carry):
        return consume(n_full, sa_ref, carry, True)

    carry = (jnp.full((2 * tq, 1), -jnp.inf, _F32), jnp.zeros((2 * tq, 1), _F32),
             jnp.zeros((2 * tq, V_DIM), _F32))
    scores(0, sa_ref)
    carry = lax.fori_loop(0, n_full // 2, pair, carry)
    _, l, acc = lax.cond(n_full % 2 == 1, odd_tail, even_tail, carry)
    lam = _diff_lambda(lam_ref, lam_init)
    o_ref[...] = _diff_finish(acc, l, tq, lam, g_ref[...], lam_init).astype(o_ref.dtype)


def _attn_prompt(q16, k16, v16, lam, g, batch, seq, lam_init):
    n_heads = q16.shape[1] // LANES
    tq = _tile(seq, 256)
    tk = _tile(seq, 512)
    assert tk % tq == 0
    nq = seq // tq
    blk = (_nbytes((tq, LANES), _BF16) * 2 + 2 * _nbytes((seq, LANES), _BF16))
    return pl.pallas_call(
        functools.partial(_attn_prompt_kernel, tq=tq, tk=tk, lam_init=lam_init),
        grid=(batch, n_heads, nq),
        in_specs=[pl.BlockSpec((tq, LANES), lambda b, h, i: (b * nq + i, h)),
                  pl.BlockSpec((seq, LANES), lambda b, h, i: (b, h)),
                  pl.BlockSpec((seq, LANES), lambda b, h, i: (b, h)),
                  pl.BlockSpec(lam.shape, lambda b, h, i: (0, 0)),
                  pl.BlockSpec((1, V_DIM), lambda b, h, i: (0, 0))],
        out_specs=pl.BlockSpec((tq, LANES), lambda b, h, i: (b * nq + i, h)),
        out_shape=jax.ShapeDtypeStruct((batch * seq, n_heads * V_DIM), _BF16),
        scratch_shapes=[pltpu.VMEM((2 * tq, tk), _F32), pltpu.VMEM((2 * tq, tk), _F32)],
        compiler_params=_params(("parallel", "parallel", "parallel"), blk,
                                6 * _nbytes((2 * tq, tk), _F32)),
        name="attn_prompt",
    )(q16, k16, v16, lam, g)


def _attn_sample_kernel(q_ref, kn_ref, vn_ref, kp_ref, vp_ref, lam_ref, g_ref, o_ref, *,
                        t, past, heads, lam_init):
    lam = _diff_lambda(lam_ref, lam_init)
    g = g_ref[...]
    qc = (past + lax.broadcasted_iota(jnp.int32, (t, t), 0)) // CHUNK
    kc = (past + lax.broadcasted_iota(jnp.int32, (t, t), 1)) // CHUNK
    vis = jnp.concatenate([kc <= qc, kc <= qc], axis=0)
    outs = []
    for h in range(heads):
        sl = slice(h * LANES, (h + 1) * LANES)
        qs = _stack_maps(q_ref[:, sl])
        kp = kp_ref[pl.ds(h, past, stride=heads), :].astype(_BF16)
        vp = vp_ref[pl.ds(h, past, stride=heads), :].astype(_BF16)
        s_p = lax.dot_general(qs, kp, _NT, preferred_element_type=_F32)
        s_n = lax.dot_general(qs, kn_ref[:, sl], _NT, preferred_element_type=_F32)
        s_n = jnp.where(vis, s_n, -jnp.inf)
        m = jnp.maximum(jnp.max(s_p, axis=1, keepdims=True), jnp.max(s_n, axis=1, keepdims=True))
        p_p = jnp.exp2(s_p - m)
        p_n = jnp.exp2(s_n - m)
        l = jnp.sum(p_p, axis=1, keepdims=True) + jnp.sum(p_n, axis=1, keepdims=True)
        acc = (jnp.dot(p_p.astype(_BF16), vp, preferred_element_type=_F32)
               + jnp.dot(p_n.astype(_BF16), vn_ref[:, sl], preferred_element_type=_F32))
        outs.append(_diff_finish(acc, l, t, lam, g, lam_init))
    o_ref[...] = jnp.concatenate(outs, axis=1).astype(o_ref.dtype)


def _attn_sample(q16, k16, v16, past_k, past_v, layer, lam, g, n_seq, t, lam_init):
    width = q16.shape[1]
    heads = width // LANES
    past = past_k.shape[1] // heads
    s0 = layer * n_seq
    new_spec = pl.BlockSpec((t, width), lambda s: (s, 0))
    past_spec = pl.BlockSpec((None, past * heads, LANES), lambda s: (s0 + s, 0, 0))
    blk = 4 * _nbytes((t, width), _BF16) + 2 * _nbytes((past * heads, LANES), _F32)
    return pl.pallas_call(
        functools.partial(_attn_sample_kernel, t=t, past=past, heads=heads, lam_init=lam_init),
        grid=(n_seq,),
        in_specs=[new_spec, new_spec, new_spec, past_spec, past_spec,
                  pl.BlockSpec(lam.shape, lambda s: (0, 0)),
                  pl.BlockSpec((1, V_DIM), lambda s: (0, 0))],
        out_specs=new_spec,
        out_shape=jax.ShapeDtypeStruct((n_seq * t, width), _BF16),
        compiler_params=_params(("parallel",), blk, 6 * _nbytes((2 * t, past), _F32)),
        name="attn_sample",
    )(q16, k16, v16, past_k, past_v, lam, g)


def _res_ln(acc, res, g, b, alpha):
    y = alpha * res + acc
    mu = jnp.mean(y, axis=1, keepdims=True)
    yc = y - mu
    var = jnp.mean(yc * yc, axis=1, keepdims=True)
    return yc * lax.rsqrt(var + LN_EPS) * g + b


def _mm_res_ln_kernel(*refs, nk, alpha, part_starts):
    n_parts = len(part_starts)
    a_refs = refs[:n_parts]
    w_ref, res_ref, g_ref, b_ref, o32_ref, o16_ref = refs[n_parts:n_parts + 6]
    scratch = refs[n_parts + 6:]

    def finish(acc):
        y = _res_ln(acc, res_ref[...], g_ref[...], b_ref[...], alpha)
        o32_ref[...] = y
        o16_ref[...] = y.astype(_BF16)

    a = a_refs[0][...]
    for p in range(1, n_parts):
        a = jnp.where(pl.program_id(0) >= part_starts[p], a_refs[p][...], a)
    part = jnp.dot(a, w_ref[...], preferred_element_type=_F32)
    if nk == 1:
        finish(part)
        return
    acc_ref, = scratch
    kk = pl.program_id(1)

    @pl.when(kk == 0)
    def _():
        acc_ref[...] = part

    @pl.when(kk > 0)
    def _():
        acc_ref[...] += part

    @pl.when(kk == nk - 1)
    def _():
        finish(acc_ref[...])


def _mm_res_ln(parts, w16, res, g, b, alpha, name):
    m = sum(a.shape[0] for a in parts)
    k = parts[0].shape[1]
    d = w16.shape[1]
    tm = _tile(functools.reduce(math.gcd, [a.shape[0] for a in parts]), 512)
    tk = k if k <= 2048 else _tile(k, 512)
    nk = k // tk
    blk = (len(parts) * _nbytes((tm, tk), _BF16) + _nbytes((tk, d), _BF16) + 2 * _nbytes((tm, d), _F32)
           + _nbytes((tm, d), _BF16))
    scratch = [pltpu.VMEM((tm, d), _F32)] if nk > 1 else []
    part_starts, part_specs, start = [], [], 0
    for a in parts:
        nblk = a.shape[0] // tm
        part_starts.append(start)
        part_specs.append(pl.BlockSpec(
            (tm, tk), lambda i, kk, start=start, nblk=nblk: (jnp.clip(i - start, 0, nblk - 1), kk)))
        start += nblk
    return pl.pallas_call(
        functools.partial(_mm_res_ln_kernel, nk=nk, alpha=alpha, part_starts=tuple(part_starts)),
        grid=(m // tm, nk),
        in_specs=part_specs + [
                  pl.BlockSpec((tk, d), lambda i, kk: (kk, 0)),
                  pl.BlockSpec((tm, d), lambda i, kk: (i, 0)),
                  pl.BlockSpec((1, d), lambda i, kk: (0, 0)),
                  pl.BlockSpec((1, d), lambda i, kk: (0, 0))],
        out_specs=[pl.BlockSpec((tm, d), lambda i, kk: (i, 0)),
                   pl.BlockSpec((tm, d), lambda i, kk: (i, 0))],
        out_shape=[jax.ShapeDtypeStruct((m, d), _F32), jax.ShapeDtypeStruct((m, d), _BF16)],
        scratch_shapes=scratch,
        compiler_params=_params(("parallel", "arbitrary"), blk, 2 * _nbytes((tm, d), _F32)),
        name=name,
    )(*parts, w16, res, g.reshape(1, d), b.reshape(1, d))


def _swiglu(g, u):
    return g * jax.nn.sigmoid(g) * u


def _gate_up_kernel(x_ref, wg_ref, wu_ref, h_ref):
    x = x_ref[...]
    g = jnp.dot(x, wg_ref[...], preferred_element_type=_F32)
    u = jnp.dot(x, wu_ref[...], preferred_element_type=_F32)
    h_ref[...] = _swiglu(g, u).astype(h_ref.dtype)


def _gate_up(x16, w_gu16):
    m, k = x16.shape
    f = w_gu16.shape[1] // 2
    tm = _tile(m, 1024)
    tf = _tile(f, 512)
    nf = f // tf
    blk = _nbytes((tm, k), _BF16) + 2 * _nbytes((k, tf), _BF16) + _nbytes((tm, tf), _BF16)
    return pl.pallas_call(
        _gate_up_kernel,
        grid=(m // tm, nf),
        in_specs=[pl.BlockSpec((tm, k), lambda i, j: (i, 0)),
                  pl.BlockSpec((k, tf), lambda i, j: (0, j)),
                  pl.BlockSpec((k, tf), lambda i, j: (0, nf + j))],
        out_specs=pl.BlockSpec((tm, tf), lambda i, j: (i, j)),
        out_shape=jax.ShapeDtypeStruct((m, f), _BF16),
        compiler_params=_params(("parallel", "parallel"), blk, 3 * _nbytes((tm, tf), _F32)),
        name="ffn_gate_up",
    )(x16, w_gu16, w_gu16)


def _pool_kernel(halo_ref, u_ref, w_ref, scale_ref, z_ref, ext_ref, *, tm, pos0, zero_first):
    i = pl.program_id(1)
    halo = halo_ref[...]
    if zero_first:
        halo = jnp.where(i == 0, jnp.zeros_like(halo), halo)
    ext_ref[0:HALO, :] = halo
    ext_ref[HALO:HALO + tm, :] = u_ref[...]
    pos = pos0 + i * tm + lax.broadcasted_iota(jnp.int32, (tm, 1), 0)
    gd = u_ref.shape[1] // len(POOL_WINDOWS)
    for g, w in enumerate(POOL_WINDOWS):
        cols = slice(g * gd, (g + 1) * gd)
        win = ext_ref[HALO:HALO + tm, cols]
        for back in range(1, w):
            win = win + ext_ref[HALO - back:HALO - back + tm, cols]
        count = jnp.minimum(w, pos + 1).astype(_F32)
        d = win / count - u_ref[:, cols]
        zg = jnp.dot(d.astype(_BF16), w_ref[g], preferred_element_type=_F32)
        z_ref[:, cols] = (zg * scale_ref[:, cols]).astype(z_ref.dtype)


def _pool_call(halo_arr, halo_spec, u, u_spec, w_grp16, scale, grid, tm, pos0, zero_first,
               out_rows, out_spec, name):
    d = u.shape[1]
    blk = (_nbytes((HALO, d), _F32) + _nbytes((tm, d), _F32) + _nbytes(w_grp16.shape, _BF16)
           + _nbytes((tm, d), _BF16))
    return pl.pallas_call(
        functools.partial(_pool_kernel, tm=tm, pos0=pos0, zero_first=zero_first),
        grid=grid,
        in_specs=[halo_spec, u_spec,
                  pl.BlockSpec(w_grp16.shape, lambda b, i: (0, 0, 0)),
                  pl.BlockSpec((1, d), lambda b, i: (0, 0))],
        out_specs=out_spec,
        out_shape=jax.ShapeDtypeStruct((out_rows, d), _BF16),
        scratch_shapes=[pltpu.VMEM((HALO + tm, d), _F32)],
        compiler_params=_params(("parallel", "arbitrary"), blk, 3 * _nbytes((HALO + tm, d), _F32)),
        name=name,
    )(halo_arr, u, w_grp16, scale.reshape(1, d))


def _pool_prompt(u, w_grp16, scale, batch, seq):
    d = u.shape[1]
    tm = _tile(seq, 256)
    nt = seq // tm
    per = tm // HALO
    halo_spec = pl.BlockSpec((HALO, d), lambda b, i: (jnp.maximum((b * nt + i) * per - 1, 0), 0))
    u_spec = pl.BlockSpec((tm, d), lambda b, i: (b * nt + i, 0))
    out_spec = pl.BlockSpec((tm, d), lambda b, i: (b * nt + i, 0))
    return _pool_call(u, halo_spec, u, u_spec, w_grp16, scale, (batch, nt), tm, 0, True,
                      batch * seq, out_spec, "pool_prompt")


def _pool_sample(u, ctx, layer, w_grp16, scale, row0, n_seq, t, pos0):
    d = u.shape[1]
    r0 = row0 // t
    s0 = layer * n_seq
    halo_spec = pl.BlockSpec((None, HALO, d), lambda s, i: (s0 + s, 0, 0))
    u_spec = pl.BlockSpec((t, d), lambda s, i: (r0 + s, 0))
    out_spec = pl.BlockSpec((t, d), lambda s, i: (s, 0))
    return _pool_call(ctx, halo_spec, u, u_spec, w_grp16, scale, (n_seq, 1), t, pos0, False,
                      n_seq * t, out_spec, "pool_sample")


def _split_bf16(x):
    hi = x.astype(_BF16)
    lo = (x - hi.astype(_F32)).astype(_BF16)
    return hi, lo


def _router_kernel(x_ref, w_ref, meta_ref, count_ref, carry_ref, *, tm, n_experts):
    i = pl.program_id(0)

    @pl.when(i == 0)
    def _():
        carry_ref[...] = jnp.zeros_like(carry_ref)

    xh, xl = _split_bf16(x_ref[...])
    wh, wl = _split_bf16(w_ref[...])
    logits = (jnp.dot(xh, wh, preferred_element_type=_F32)
              + (jnp.dot(xh, wl, preferred_element_type=_F32)
                 + jnp.dot(xl, wh, preferred_element_type=_F32)))
    lane = lax.broadcasted_iota(jnp.int32, (tm, LANES), 1)
    lg = jnp.where(lane < n_experts, logits, -jnp.inf)
    m1 = jnp.max(lg, axis=1, keepdims=True)
    i1 = jnp.min(jnp.where(lg == m1, lane, LANES), axis=1, keepdims=True)
    lg2 = jnp.where(lane == i1, -jnp.inf, lg)
    m2 = jnp.max(lg2, axis=1, keepdims=True)
    i2 = jnp.min(jnp.where(lg2 == m2, lane, LANES), axis=1, keepdims=True)
    e = jnp.exp(m2 - m1)
    g1 = 1.0 / (1.0 + e)
    g2 = e / (1.0 + e)

    sel1 = lane == i1
    sel2 = lane == i2
    cnt = jnp.where(sel1, 1.0, 0.0) + jnp.where(sel2, 1.0, 0.0)
    row = lax.broadcasted_iota(jnp.int32, (tm, tm), 0)
    col = lax.broadcasted_iota(jnp.int32, (tm, tm), 1)
    lower = jnp.where(col < row, 1.0, 0.0).astype(_BF16)
    before = jnp.dot(lower, cnt.astype(_BF16), preferred_element_type=_F32) + carry_ref[0:1, :]
    r1 = jnp.sum(jnp.where(sel1, before, 0.0), axis=1, keepdims=True)
    r2 = jnp.sum(jnp.where(sel2, before, 0.0), axis=1, keepdims=True)
    carry_ref[0:1, :] = carry_ref[0:1, :] + jnp.sum(cnt, axis=0, keepdims=True)
    count_ref[...] = carry_ref[...]

    meta = jnp.zeros((tm, LANES), _F32)
    for k, v in enumerate((i1.astype(_F32), i2.astype(_F32), r1, r2, g1, g2)):
        meta = jnp.where(lane == k, v, meta)
    meta_ref[...] = meta


def _router(x32, w_router):
    n, d = x32.shape
    n_experts = w_router.shape[1]
    tm = _tile(n, 256)
    w_pad = jnp.pad(w_router, ((0, 0), (0, LANES - n_experts)))
    blk = _nbytes((tm, d), _F32) + _nbytes((d, LANES), _F32) + 2 * _nbytes((tm, LANES), _F32)
    return pl.pallas_call(
        functools.partial(_router_kernel, tm=tm, n_experts=n_experts),
        grid=(n // tm,),
        in_specs=[pl.BlockSpec((tm, d), lambda i: (i, 0)),
                  pl.BlockSpec((d, LANES), lambda i: (0, 0))],
        out_specs=[pl.BlockSpec((tm, LANES), lambda i: (i, 0)),
                   pl.BlockSpec((8, LANES), lambda i: (0, 0))],
        out_shape=[jax.ShapeDtypeStruct((n, LANES), _F32), jax.ShapeDtypeStruct((8, LANES), _F32)],
        scratch_shapes=[pltpu.VMEM((8, LANES), _F32)],
        compiler_params=_params(("arbitrary",), blk, 4 * _nbytes((tm, d), _F32)),
        name="moe_router",
    )(x32, w_pad)


def _row_copy(src_hbm, row, dst_vmem, r, sem):
    return pltpu.make_async_copy(src_hbm.at[pl.ds(row, 1), :], dst_vmem.at[pl.ds(r, 1), :], sem)


def _gather_rows(src_hbm, idx_ref, base, dst_vmem, sem, n):
    def issue(r, c):
        _row_copy(src_hbm, idx_ref[base + r], dst_vmem, r, sem).start()
        return c
    lax.fori_loop(0, n, issue, 0, unroll=4)


def _wait_rows(src_hbm, dst_vmem, sem, n):
    def drain(r, c):
        _row_copy(src_hbm, 0, dst_vmem, r, sem).wait()
        return c
    lax.fori_loop(0, n, drain, 0, unroll=8)


def _moe_gather_kernel(tok_ref, nu_ref, x_hbm, xs_hbm, in_ref, out_ref, in_sem, out_sem, *, tg):
    n_used = nu_ref[0]

    def out_copy(b, slot):
        return pltpu.make_async_copy(out_ref.at[slot], xs_hbm.at[pl.ds(pl.multiple_of(b * tg, tg), tg), :],
                                     out_sem.at[slot])

    _gather_rows(x_hbm, tok_ref, 0, in_ref.at[0], in_sem.at[0], tg)

    def body(b, c):
        slot = b % 2

        @pl.when(b + 1 < n_used)
        def _():
            _gather_rows(x_hbm, tok_ref, (b + 1) * tg, in_ref.at[1 - slot], in_sem.at[1 - slot], tg)

        _wait_rows(x_hbm, in_ref.at[slot], in_sem.at[slot], tg)

        @pl.when(b >= 2)
        def _():
            out_copy(b - 2, slot).wait()

        out_ref[slot] = in_ref[slot].astype(out_ref.dtype)
        out_copy(b, slot).start()
        return c

    lax.fori_loop(0, n_used, body, 0)

    @pl.when(n_used >= 2)
    def _():
        out_copy(n_used - 2, n_used % 2).wait()

    out_copy(n_used - 1, (n_used - 1) % 2).wait()

    n_blocks = xs_hbm.shape[0] // tg
    out_ref[0] = jnp.zeros(out_ref.shape[1:], out_ref.dtype)

    def start_zero(b, c):
        out_copy(b, 0).start()
        return c

    def wait_zero(b, c):
        out_copy(b, 0).wait()
        return c

    lax.fori_loop(n_used, n_blocks, start_zero, 0)
    lax.fori_loop(n_used, n_blocks, wait_zero, 0)


def _moe_gather(x32, row_tok, n_used):
    rows = row_tok.shape[0]
    d = x32.shape[1]
    tg = MOE_BLOCK
    return pl.pallas_call(
        functools.partial(_moe_gather_kernel, tg=tg),
        grid_spec=pltpu.PrefetchScalarGridSpec(
            num_scalar_prefetch=2,
            grid=(1,),
            in_specs=[pl.BlockSpec(memory_space=pl.ANY)],
            out_specs=pl.BlockSpec(memory_space=pl.ANY),
            scratch_shapes=[pltpu.VMEM((2, tg, d), _F32), pltpu.VMEM((2, tg, d), _BF16),
                            pltpu.SemaphoreType.DMA((2,)), pltpu.SemaphoreType.DMA((2,))]),
        out_shape=jax.ShapeDtypeStruct((rows, d), _BF16),
        compiler_params=_params(("arbitrary",), 0, 2 * _nbytes((tg, d), _F32) + 2 * _nbytes((tg, d), _BF16)),
        name="moe_gather",
    )(row_tok, n_used, x32)


def _grouped_rows(gs_ref, gb_ref, e, src_hbm, dst_hbm, in_ref, out_ref, in_sem, out_sem, col, compute, tm, big):
    assert big in (1, 2, 4)
    r0 = gs_ref[e]
    nblk = gb_ref[e]
    n_main = nblk // big
    tails = [big >> k for k in range(1, big.bit_length())]

    def rows(off, nb):
        return pl.ds(pl.multiple_of(r0 + off * tm, tm), nb * tm)

    def in_copy(off, nb, slot):
        return pltpu.make_async_copy(src_hbm.at[rows(off, nb), :], in_ref.at[slot, pl.ds(0, nb * tm), :],
                                     in_sem.at[slot])

    def out_copy(off, nb, slot):
        return pltpu.make_async_copy(out_ref.at[slot, pl.ds(0, nb * tm), :], dst_hbm.at[rows(off, nb), col],
                                     out_sem.at[slot])

    def tail_off(t):
        return (nblk // (2 * t)) * (2 * t)

    def start_first_tail(first, slot, enable):
        pending = enable
        for t in tails[first:]:
            present = (nblk & t) != 0

            @pl.when(jnp.logical_and(pending, present))
            def _():
                in_copy(tail_off(t), t, slot).start()

            pending = jnp.logical_and(pending, jnp.logical_not(present))

    @pl.when(n_main > 0)
    def _():
        in_copy(0, big, 0).start()

    start_first_tail(0, 0, n_main == 0)

    def body(b, c):
        slot = b % 2
        in_copy(b * big, big, slot).wait()

        @pl.when(b + 1 < n_main)
        def _():
            in_copy((b + 1) * big, big, 1 - slot).start()

        start_first_tail(0, 1 - slot, b + 1 == n_main)

        @pl.when(b >= 2)
        def _():
            out_copy((b - 2) * big, big, slot).wait()

        out_ref[slot] = compute(in_ref[slot])
        out_copy(b * big, big, slot).start()
        return c

    lax.fori_loop(0, n_main, body, 0)

    @pl.when(n_main >= 2)
    def _():
        out_copy((n_main - 2) * big, big, n_main % 2).wait()

    @pl.when(n_main >= 1)
    def _():
        out_copy((n_main - 1) * big, big, (n_main - 1) % 2).wait()

    slot = n_main % 2
    tail_slots = []
    for k, t in enumerate(tails):
        present = (nblk & t) != 0
        tail_slots.append(slot)
        cur = slot

        @pl.when(present)
        def _():
            in_copy(tail_off(t), t, cur).wait()
            start_first_tail(k + 1, 1 - cur, True)
            out_ref[cur, 0:t * tm, :] = compute(in_ref[cur, 0:t * tm, :])
            out_copy(tail_off(t), t, cur).start()

        slot = jnp.where(present, 1 - slot, slot)

    for k, t in enumerate(tails):
        @pl.when((nblk & t) != 0)
        def _():
            out_copy(tail_off(t), t, tail_slots[k]).wait()


def _zero_rows(gs_ref, gb_ref, e, dst_hbm, out_ref, out_sem, col, tm):
    r0 = gs_ref[e]
    out_ref[0, 0:tm, :] = jnp.zeros((tm, out_ref.shape[2]), out_ref.dtype)

    def copy(b):
        return pltpu.make_async_copy(out_ref.at[0, pl.ds(0, tm), :],
                                     dst_hbm.at[pl.ds(pl.multiple_of(r0 + b * tm, tm), tm), col], out_sem.at[0])

    def start(b, c):
        copy(b).start()
        return c

    def wait(b, c):
        copy(b).wait()
        return c

    lax.fori_loop(0, gb_ref[e], start, 0)
    lax.fori_loop(0, gb_ref[e], wait, 0)


def _moe_gate_up_kernel(gs_ref, gb_ref, xs_hbm, wg_ref, wu_ref, h_hbm, wg16_ref, wu16_ref,
                        in_ref, out_ref, in_sem, out_sem, *, tm, tf, n_experts):
    j = pl.program_id(0)
    e = pl.program_id(1)
    col = pl.ds(pl.multiple_of(j * tf, tf), tf)

    def compute(x):
        g = jnp.dot(x, wg16_ref[...], preferred_element_type=_F32)
        u = jnp.dot(x, wu16_ref[...], preferred_element_type=_F32)
        return _swiglu(g, u).astype(_BF16)

    @pl.when(e < n_experts)
    def _():
        wg16_ref[...] = wg_ref[...].astype(_BF16)
        wu16_ref[...] = wu_ref[...].astype(_BF16)
        _grouped_rows(gs_ref, gb_ref, e, xs_hbm, h_hbm, in_ref, out_ref, in_sem, out_sem, col, compute, tm,
                      in_ref.shape[1] // tm)

    @pl.when(e == n_experts)
    def _():
        _zero_rows(gs_ref, gb_ref, e, h_hbm, out_ref, out_sem, col, tm)


def _moe_gate_up(xs16, w_gu, gstart, gblocks):
    rows, d = xs16.shape
    n_experts = w_gu.shape[0]
    f = w_gu.shape[2] // 2
    tm = MOE_BLOCK
    tf = _tile(f, 512)
    nf = f // tf
    blk = 2 * _nbytes((d, tf), _F32)
    tc = MOE_CHUNK_BLOCKS * tm
    scratch_bytes = (2 * _nbytes((d, tf), _BF16) + 2 * _nbytes((tc, d), _BF16) + 2 * _nbytes((tc, tf), _BF16)
                     + 3 * _nbytes((tc, tf), _F32))
    last = n_experts - 1
    return pl.pallas_call(
        functools.partial(_moe_gate_up_kernel, tm=tm, tf=tf, n_experts=n_experts),
        grid_spec=pltpu.PrefetchScalarGridSpec(
            num_scalar_prefetch=2,
            grid=(nf, n_experts + 1),
            in_specs=[pl.BlockSpec(memory_space=pl.ANY),
                      pl.BlockSpec((None, d, tf), lambda j, e, gs, gb: (jnp.minimum(e, last), 0, j)),
                      pl.BlockSpec((None, d, tf), lambda j, e, gs, gb: (jnp.minimum(e, last), 0, nf + j))],
            out_specs=pl.BlockSpec(memory_space=pl.ANY),
            scratch_shapes=[pltpu.VMEM((d, tf), _BF16), pltpu.VMEM((d, tf), _BF16),
                            pltpu.VMEM((2, tc, d), _BF16), pltpu.VMEM((2, tc, tf), _BF16),
                            pltpu.SemaphoreType.DMA((2,)), pltpu.SemaphoreType.DMA((2,))]),
        out_shape=jax.ShapeDtypeStruct((rows, f), _BF16),
        compiler_params=_params(("arbitrary", "arbitrary"), blk, scratch_bytes),
        name="moe_gate_up",
    )(gstart, gblocks, xs16, w_gu, w_gu)


def _moe_down_kernel(gs_ref, gb_ref, h_hbm, w_ref, y_hbm, w16_ref, in_ref, out_ref, in_sem, out_sem,
                     *, tm, tn, n_experts):
    j = pl.program_id(0)
    e = pl.program_id(1)
    col = pl.ds(pl.multiple_of(j * tn, tn), tn)

    def compute(h):
        return jnp.dot(h, w16_ref[...], preferred_element_type=_F32)

    @pl.when(e < n_experts)
    def _():
        w16_ref[...] = w_ref[...].astype(_BF16)
        _grouped_rows(gs_ref, gb_ref, e, h_hbm, y_hbm, in_ref, out_ref, in_sem, out_sem, col, compute, tm,
                      in_ref.shape[1] // tm)

    @pl.when(e == n_experts)
    def _():
        _zero_rows(gs_ref, gb_ref, e, y_hbm, out_ref, out_sem, col, tm)


def _moe_down(h16, w_down, gstart, gblocks):
    rows, f = h16.shape
    n_experts = w_down.shape[0]
    d = w_down.shape[2]
    tm = MOE_BLOCK
    tn = _tile(d, 512)
    blk = _nbytes((f, tn), _F32)
    scratch_bytes = (_nbytes((f, tn), _BF16) + 2 * _nbytes((tm, f), _BF16) + 3 * _nbytes((tm, tn), _F32))
    last = n_experts - 1
    return pl.pallas_call(
        functools.partial(_moe_down_kernel, tm=tm, tn=tn, n_experts=n_experts),
        grid_spec=pltpu.PrefetchScalarGridSpec(
            num_scalar_prefetch=2,
            grid=(d // tn, n_experts + 1),
            in_specs=[pl.BlockSpec(memory_space=pl.ANY),
                      pl.BlockSpec((None, f, tn), lambda j, e, gs, gb: (jnp.minimum(e, last), 0, j))],
            out_specs=pl.BlockSpec(memory_space=pl.ANY),
            scratch_shapes=[pltpu.VMEM((f, tn), _BF16),
                            pltpu.VMEM((2, tm, f), _BF16), pltpu.VMEM((2, tm, tn), _F32),
                            pltpu.SemaphoreType.DMA((2,)), pltpu.SemaphoreType.DMA((2,))]),
        out_shape=jax.ShapeDtypeStruct((rows, d), _F32),
        compiler_params=_params(("arbitrary", "arbitrary"), blk, scratch_bytes),
        name="moe_down",
    )(gstart, gblocks, h16, w_down)


def _moe_combine_kernel(d1_ref, d2_ref, y_hbm, meta_ref, res_ref, g_ref, b_ref, *rest, tc, alpha, n_split):
    outs, (buf_ref, sem) = rest[:-2], rest[-2:]
    i = pl.program_id(0)
    slot = i % 2

    def fetch(step, s):
        _gather_rows(y_hbm, d1_ref, step * tc, buf_ref.at[s, 0], sem.at[s], tc)
        _gather_rows(y_hbm, d2_ref, step * tc, buf_ref.at[s, 1], sem.at[s], tc)

    @pl.when(i == 0)
    def _():
        fetch(0, 0)

    @pl.when(i + 1 < pl.num_programs(0))
    def _():
        fetch(i + 1, 1 - slot)

    _wait_rows(y_hbm, buf_ref.at[slot, 0], sem.at[slot], tc)
    _wait_rows(y_hbm, buf_ref.at[slot, 1], sem.at[slot], tc)
    meta = meta_ref[...]
    f = buf_ref[slot, 0] * meta[:, 4:5] + buf_ref[slot, 1] * meta[:, 5:6]
    y = _res_ln(f, res_ref[...], g_ref[...], b_ref[...], alpha)
    if n_split is None:
        outs[0][...] = y
        outs[1][...] = y.astype(_BF16)
    else:
        @pl.when(i < n_split)
        def _():
            outs[0][...] = y

        @pl.when(i >= n_split)
        def _():
            outs[1][...] = y


def _moe_combine_ln(y_rows, dest1, dest2, meta, res, g, b, alpha, split_rows):
    n, d = res.shape
    tc = _tile(n if split_rows is None else math.gcd(n, split_rows), 128)
    blk = _nbytes((tc, LANES), _F32) + 2 * _nbytes((tc, d), _F32) + _nbytes((tc, d), _BF16)
    tile_spec = pl.BlockSpec((tc, d), lambda i, a, c: (i, 0))
    if split_rows is None:
        n_split = None
        out_specs = [tile_spec, tile_spec]
        out_shape = [jax.ShapeDtypeStruct((n, d), _F32), jax.ShapeDtypeStruct((n, d), _BF16)]
    else:
        n_split = split_rows // tc
        out_specs = [pl.BlockSpec((tc, d), lambda i, a, c: (jnp.minimum(i, n_split - 1), 0)),
                     pl.BlockSpec((tc, d), lambda i, a, c: (jnp.maximum(i - n_split, 0), 0))]
        out_shape = [jax.ShapeDtypeStruct((split_rows, d), _F32),
                     jax.ShapeDtypeStruct((n - split_rows, d), _F32)]
    return pl.pallas_call(
        functools.partial(_moe_combine_kernel, tc=tc, alpha=alpha, n_split=n_split),
        grid_spec=pltpu.PrefetchScalarGridSpec(
            num_scalar_prefetch=2,
            grid=(n // tc,),
            in_specs=[pl.BlockSpec(memory_space=pl.ANY),
                      pl.BlockSpec((tc, LANES), lambda i, a, c: (i, 0)),
                      tile_spec,
                      pl.BlockSpec((1, d), lambda i, a, c: (0, 0)),
                      pl.BlockSpec((1, d), lambda i, a, c: (0, 0))],
            out_specs=out_specs,
            scratch_shapes=[pltpu.VMEM((2, 2, tc, d), _F32), pltpu.SemaphoreType.DMA((2,))]),
        out_shape=out_shape,
        compiler_params=_params(("arbitrary",), blk, 8 * _nbytes((tc, d), _F32)),
        name="moe_combine_ln",
    )(dest1, dest2, y_rows, meta, res, g.reshape(1, d), b.reshape(1, d))


def _moe_layer(x32, w_router, w_gu, w_down, g, b, alpha, split_rows):
    n, d = x32.shape
    n_experts = w_router.shape[1]
    meta, counts = _router(x32, w_router)
    e1 = meta[:, 0].astype(jnp.int32)
    e2 = meta[:, 1].astype(jnp.int32)
    counts = counts[0, :n_experts].astype(jnp.int32)
    gblocks = (counts + MOE_BLOCK - 1) // MOE_BLOCK
    gend = jnp.cumsum(gblocks) * MOE_BLOCK
    gstart = gend - gblocks * MOE_BLOCK
    dest1 = gstart[e1] + meta[:, 2].astype(jnp.int32)
    dest2 = gstart[e2] + meta[:, 3].astype(jnp.int32)
    nb = (n * TOP_K) // MOE_BLOCK + n_experts
    rows = nb * MOE_BLOCK
    tok = jnp.arange(n, dtype=jnp.int32)
    row_tok = jnp.zeros((rows,), jnp.int32).at[dest1].set(tok).at[dest2].set(tok)
    n_used = (gend[-1] // MOE_BLOCK).astype(jnp.int32).reshape(1)
    gstart = jnp.concatenate([gstart, gend[-1:]]).astype(jnp.int32)
    gblocks = jnp.concatenate([gblocks, nb - n_used]).astype(jnp.int32)

    xs16 = _moe_gather(x32, row_tok, n_used)
    h16 = _moe_gate_up(xs16, w_gu, gstart, gblocks)
    y_rows = _moe_down(h16, w_down, gstart, gblocks)
    return _moe_combine_ln(y_rows, dest1, dest2, meta, x32, g, b, alpha, split_rows)


def kernel(x_prompt, x_sample, cache_k, cache_v, state_pool, attn_w_in, attn_w_o, attn_lambda, attn_subln_g, pool_w_in, pool_w_grp, pool_scale, pool_w_o, ln_mix_g, ln_mix_b, ln_ffn_g, ln_ffn_b, ffn_w_gu, ffn_w_down, moe_w_router, moe_w_gu, moe_w_down):
    batch, seq, d = x_prompt.shape
    n_seq, t, _ = x_sample.shape
    past = cache_k.shape[2]
    depth = ln_mix_g.shape[0]
    alpha = (2 * depth) ** 0.25
    n_p = batch * seq
    n_s = n_seq * t
    n = n_p + n_s
    n_heads = cache_k.shape[3]
    qk_width = n_heads * 2 * HEAD_DIM
    v_width = n_heads * V_DIM

    x32 = jnp.concatenate([x_prompt.reshape(n_p, d), x_sample.reshape(n_s, d)], axis=0)
    x16 = x32.astype(_BF16)
    tab_p = _rope_tables(np.arange(seq))
    tab_s = _rope_tables(np.tile(past + np.arange(t), n_seq))
    past_k = cache_k.reshape(-1, past * n_heads, 2 * HEAD_DIM)
    past_v = cache_v.reshape(-1, past * n_heads, V_DIM)
    ctx = jnp.pad(state_pool, ((0, 0), (0, 0), (HALO - POOL_CTX, 0), (0, 0))).reshape(-1, HALO, d)

    kp_l, vp_l, ks_l, vs_l, pp_l, ps_l = [], [], [], [], [], []
    y_p = y_s = None
    for i in range(depth):
        j = i // 2
        last = i == depth - 1
        if i % 2 == 0:
            lam_init = 0.8 - 0.6 * math.exp(-0.3 * i)
            w_in16 = attn_w_in[j].astype(_BF16)
            g = attn_subln_g[j].reshape(1, V_DIM)
            rows_p, rows_s = (0, n_p, tab_p), (n_p, n_s, tab_s)
            (q_p,), (q_s,) = (_proj(x16, w_in16, 0, qk_width, r0, nr, tab, False, True, "proj_q", Q_SCALE)
                              for r0, nr, tab in (rows_p, rows_s))
            (k32_p, k_p), (k32_s, k_s) = (_proj(x16, w_in16, qk_width, qk_width, r0, nr, tab, True, True, "proj_k")
                                          for r0, nr, tab in (rows_p, rows_s))
            (v32_p, v_p), (v32_s, v_s) = (_proj(x16, w_in16, 2 * qk_width, v_width, r0, nr, None, True, True, "proj_v")
                                          for r0, nr, tab in (rows_p, rows_s))
            mix = [_attn_prompt(q_p, k_p, v_p, attn_lambda[j], g, batch, seq, lam_init),
                   _attn_sample(q_s, k_s, v_s, past_k, past_v, j, attn_lambda[j], g, n_seq, t, lam_init)]
            w_o16 = attn_w_o[j].astype(_BF16)
            kp_l.append(k32_p.reshape(batch, seq, n_heads, 2 * HEAD_DIM))
            vp_l.append(v32_p.reshape(batch, seq, n_heads, V_DIM))
            ks_l.append(k32_s.reshape(n_seq, t, n_heads, 2 * HEAD_DIM))
            vs_l.append(v32_s.reshape(n_seq, t, n_heads, V_DIM))
        else:
            u32, = _proj(x16, pool_w_in[j].astype(_BF16), 0, d, 0, n, None, True, False, "pool_in")
            w_grp16 = pool_w_grp[j].astype(_BF16)
            mix = [_pool_prompt(u32, w_grp16, pool_scale[j], batch, seq),
                   _pool_sample(u32, ctx, j, w_grp16, pool_scale[j], n_p, n_seq, t, past)]
            w_o16 = pool_w_o[j].astype(_BF16)
            u_p = u32[:n_p].reshape(batch, seq, d)
            u_s = u32[n_p:].reshape(n_seq, t, d)
            pp_l.append(jnp.concatenate([jnp.zeros((batch, POOL_CTX, d), _F32), u_p[:, -POOL_CTX:]], 1)[:, -POOL_CTX:])
            ps_l.append(jnp.concatenate([state_pool[j], u_s[:, -POOL_CTX:]], 1)[:, -POOL_CTX:])
        x32, x16 = _mm_res_ln(mix, w_o16, x32, ln_mix_g[i], ln_mix_b[i], alpha, "mix_out_ln")
        if i % 2 == 0:
            h16 = _gate_up(x16, ffn_w_gu[j].astype(_BF16))
            x32, x16 = _mm_res_ln([h16], ffn_w_down[j].astype(_BF16), x32, ln_ffn_g[i], ln_ffn_b[i],
                                  alpha, "ffn_down_ln")
        elif last:
            y_p, y_s = _moe_layer(x32, moe_w_router[j], moe_w_gu[j], moe_w_down[j],
                                  ln_ffn_g[i], ln_ffn_b[i], alpha, n_p)
        else:
            x32, x16 = _moe_layer(x32, moe_w_router[j], moe_w_gu[j], moe_w_down[j],
                                  ln_ffn_g[i], ln_ffn_b[i], alpha, None)
    if y_p is None:
        y_p, y_s = x32[:n_p], x32[n_p:]
    return (y_p.reshape(batch, seq, d), y_s.reshape(n_seq, t, d), jnp.stack(kp_l), jnp.stack(vp_l),
            jnp.stack(pp_l), jnp.stack(ks_l), jnp.stack(vs_l), jnp.stack(ps_l))
```

```python
import functools
import math

import numpy as np
import jax
import jax.numpy as jnp
from jax import lax
from jax.experimental import pallas as pl
from jax.experimental.pallas import tpu as pltpu

CHUNK = 64
HEAD_DIM = 64
V_DIM = 2 * HEAD_DIM
ROT_DIM = HEAD_DIM // 4
ROPE_THETA = 500000.0
POOL_WINDOWS = (2, 4, 8, 16)
POOL_CTX = max(POOL_WINDOWS) - 1
TOP_K = 2
LN_EPS = 1e-5

LANES = 128
V7X_VMEM_REQUEST_CAP = 56 * 1024 * 1024
COMPILER_SCRATCH_ALLOWANCE = 6 * 1024 * 1024

HALO = 16
MOE_BLOCK = 256
MOE_CHUNK_BLOCKS = 4
ROW_DMA_PRIORITY = 1
Q_SCALE = HEAD_DIM ** -0.5 * math.log2(math.e)

_F32 = jnp.float32
_BF16 = jnp.bfloat16
_NT = (((1,), (1,)), ((), ()))


def _nbytes(shape, dtype):
    return math.prod(shape) * jnp.dtype(dtype).itemsize


def _params(semantics, pipelined_bytes, scratch_bytes=0):
    need = 2 * pipelined_bytes + scratch_bytes + COMPILER_SCRATCH_ALLOWANCE
    return pltpu.CompilerParams(dimension_semantics=semantics,
                                vmem_limit_bytes=min(need, V7X_VMEM_REQUEST_CAP))


def _tile(n, pref):
    t = min(n, pref)
    while n % t:
        t //= 2
    return t


def _rope(acc, cos, sin_lo, sin_hi):
    half = ROT_DIM // 2
    pieces = []
    for g in range(acc.shape[1] // LANES):
        xg = acc[:, g * LANES:(g + 1) * LANES]
        pieces.append(xg * cos
                      + pltpu.roll(xg, half, 1) * sin_hi
                      + pltpu.roll(xg, LANES - half, 1) * sin_lo)
    return pieces[0] if len(pieces) == 1 else jnp.concatenate(pieces, axis=1)


def _proj_kernel(*refs, rope, want32, want16, scale):
    x_ref, w_ref = refs[0], refs[1]
    acc = jnp.dot(x_ref[...], w_ref[...], preferred_element_type=_F32)
    pos = 2
    if rope:
        acc = _rope(acc, refs[2][...], refs[3][...], refs[4][...])
        pos = 5
    if scale is not None:
        acc = acc * scale
    if want32:
        refs[pos][...] = acc
        pos += 1
    if want16:
        refs[pos][...] = acc.astype(_BF16)


def _proj(xb, w, col0, ncols, row0, nrows, tables, want32, want16, name, scale=None):
    k = xb.shape[1]
    period = nrows if tables is None else tables[0].shape[0]
    tm = _tile(math.gcd(math.gcd(row0, nrows), period) if row0 else math.gcd(nrows, period), 1024)
    tn = _tile(ncols, 512)
    off = col0 // tn
    rb0 = row0 // tm
    tper = period // tm
    in_specs = [pl.BlockSpec((tm, k), lambda i, j: (rb0 + i, 0)),
                pl.BlockSpec((k, tn), lambda i, j: (0, j + off))]
    args = [xb, w]
    blk = _nbytes((tm, k), _BF16) + _nbytes((k, tn), _BF16)
    if tables is not None:
        for t in tables:
            in_specs.append(pl.BlockSpec((tm, LANES), lambda i, j: (i % tper, 0)))
            args.append(t)
            blk += _nbytes((tm, LANES), _F32)
    out_shape, out_specs = [], []
    for want, dt in ((want32, _F32), (want16, _BF16)):
        if want:
            out_shape.append(jax.ShapeDtypeStruct((nrows, ncols), dt))
            out_specs.append(pl.BlockSpec((tm, tn), lambda i, j: (i, j)))
            blk += _nbytes((tm, tn), dt)
    return pl.pallas_call(
        functools.partial(_proj_kernel, rope=tables is not None, want32=want32, want16=want16,
                          scale=scale),
        grid=(nrows // tm, ncols // tn),
        in_specs=in_specs, out_specs=out_specs, out_shape=out_shape,
        compiler_params=_params(("parallel", "parallel"), blk, _nbytes((tm, tn), _F32)),
        name=name,
    )(*args)


def _rope_tables(pos):
    half = ROT_DIM // 2
    inv_freq = ROPE_THETA ** (-np.arange(0, ROT_DIM, 2, dtype=np.float64) / ROT_DIM)
    ang = np.asarray(pos, np.float64)[:, None] * inv_freq[None, :]
    cos, sin = np.cos(ang), np.sin(ang)
    t = ang.shape[0]
    ones = np.ones((t, HEAD_DIM - ROT_DIM))
    zeros = np.zeros((t, HEAD_DIM - ROT_DIM))
    zh = np.zeros((t, half))
    cos_map = np.concatenate([cos, cos, ones], 1)
    lo_map = np.concatenate([-sin, zh, zeros], 1)
    hi_map = np.concatenate([zh, sin, zeros], 1)
    return tuple(jnp.asarray(np.concatenate([a, a], 1), _F32) for a in (cos_map, lo_map, hi_map))


def _stack_maps(q):
    lane = lax.broadcasted_iota(jnp.int32, q.shape, 1)
    zero = jnp.zeros_like(q)
    return jnp.concatenate([jnp.where(lane < HEAD_DIM, q, zero),
                            jnp.where(lane >= HEAD_DIM, q, zero)], axis=0)


def _diff_lambda(lam_ref, lam_init):
    lf = lam_ref[...]
    a = jnp.sum(lf[0:1] * lf[1:2], axis=1, keepdims=True)
    b = jnp.sum(lf[2:3] * lf[3:4], axis=1, keepdims=True)
    return jnp.exp(a) - jnp.exp(b) + lam_init


def _diff_finish(acc, l, tq, lam, g, lam_init):
    o = acc / l
    o = o[:tq] - lam * o[tq:]
    ms = jnp.mean(o * o, axis=1, keepdims=True)
    return (o * lax.rsqrt(ms + LN_EPS) * g) * (1.0 - lam_init)


def _attn_prompt_kernel(q_ref, k_ref, v_ref, lam_ref, g_ref, o_ref, sa_ref, sb_ref, *, tq, tk, lam_init):
    q0 = pl.program_id(2) * tq
    qs = _stack_maps(q_ref[...])
    n_full = q0 // tk

    def scores(kb, dst_ref):
        k0 = pl.multiple_of(kb * tk, tk)
        dst_ref[...] = lax.dot_general(qs, k_ref[pl.ds(k0, tk), :], _NT, preferred_element_type=_F32)

    def consume(kb, src_ref, carry, masked):
        m, l, acc = carry
        k0 = pl.multiple_of(kb * tk, tk)
        s = src_ref[...]
        if masked:
            qc = (q0 + lax.broadcasted_iota(jnp.int32, (tq, tk), 0)) // CHUNK
            kc = (k0 + lax.broadcasted_iota(jnp.int32, (tq, tk), 1)) // CHUNK
            vis = kc <= qc
            s = jnp.where(jnp.concatenate([vis, vis], axis=0), s, -jnp.inf)
        m_new = jnp.maximum(m, jnp.max(s, axis=1, keepdims=True))
        p = jnp.exp2(s - m_new)
        alpha = jnp.exp2(m - m_new)
        l = alpha * l + jnp.sum(p, axis=1, keepdims=True)
        acc = alpha * acc + jnp.dot(p.astype(_BF16), v_ref[pl.ds(k0, tk), :], preferred_element_type=_F32)
        return m_new, l, acc

    def pair(p, carry):
        kb = 2 * p
        scores(kb + 1, sb_ref)
        carry = consume(kb, sa_ref, carry, False)
        scores(kb + 2, sa_ref)
        return consume(kb + 1, sb_ref, carry, False)

    def odd_tail(carry):
        scores(n_full, sb_ref)
        carry = consume(n_full - 1, sa_ref, carry, False)
        return consume(n_full, sb_ref, carry, True)

    def even_tail(carry):
        return consume(n_full, sa_ref, carry, True)

    carry = (jnp.full((2 * tq, 1), -jnp.inf, _F32), jnp.zeros((2 * tq, 1), _F32),
             jnp.zeros((2 * tq, V_DIM), _F32))
    scores(0, sa_ref)
    carry = lax.fori_loop(0, n_full // 2, pair, carry)
    _, l, acc = lax.cond(n_full % 2 == 1, odd_tail, even_tail, carry)
    lam = _diff_lambda(lam_ref, lam_init)
    o_ref[...] = _diff_finish(acc, l, tq, lam, g_ref[...], lam_init).astype(o_ref.dtype)


def _attn_prompt(q16, k16, v16, lam, g, batch, seq, lam_init):
    n_heads = q16.shape[1] // LANES
    tq = _tile(seq, 256)
    tk = _tile(seq, 512)
    assert tk % tq == 0
    nq = seq // tq
    blk = (_nbytes((tq, LANES), _BF16) * 2 + 2 * _nbytes((seq, LANES), _BF16))
    return pl.pallas_call(
        functools.partial(_attn_prompt_kernel, tq=tq, tk=tk, lam_init=lam_init),
        grid=(batch, n_heads, nq),
        in_specs=[pl.BlockSpec((tq, LANES), lambda b, h, i: (b * nq + i, h)),
                  pl.BlockSpec((seq, LANES), lambda b, h, i: (b, h)),
                  pl.BlockSpec((seq, LANES), lambda b, h, i: (b, h)),
                  pl.BlockSpec(lam.shape, lambda b, h, i: (0, 0)),
                  pl.BlockSpec((1, V_DIM), lambda b, h, i: (0, 0))],
        out_specs=pl.BlockSpec((tq, LANES), lambda b, h, i: (b * nq + i, h)),
        out_shape=jax.ShapeDtypeStruct((batch * seq, n_heads * V_DIM), _BF16),
        scratch_shapes=[pltpu.VMEM((2 * tq, tk), _F32), pltpu.VMEM((2 * tq, tk), _F32)],
        compiler_params=_params(("parallel", "parallel", "parallel"), blk,
                                6 * _nbytes((2 * tq, tk), _F32)),
        name="attn_prompt",
    )(q16, k16, v16, lam, g)


def _attn_sample_kernel(q_ref, kn_ref, vn_ref, kp_ref, vp_ref, lam_ref, g_ref, o_ref, *,
                        t, past, heads, lam_init):
    lam = _diff_lambda(lam_ref, lam_init)
    g = g_ref[...]
    qc = (past + lax.broadcasted_iota(jnp.int32, (t, t), 0)) // CHUNK
    kc = (past + lax.broadcasted_iota(jnp.int32, (t, t), 1)) // CHUNK
    vis = jnp.concatenate([kc <= qc, kc <= qc], axis=0)
    outs = []
    for h in range(heads):
        sl = slice(h * LANES, (h + 1) * LANES)
        qs = _stack_maps(q_ref[:, sl])
        kp = kp_ref[pl.ds(h, past, stride=heads), :].astype(_BF16)
        vp = vp_ref[pl.ds(h, past, stride=heads), :].astype(_BF16)
        s_p = lax.dot_general(qs, kp, _NT, preferred_element_type=_F32)
        s_n = lax.dot_general(qs, kn_ref[:, sl], _NT, preferred_element_type=_F32)
        s_n = jnp.where(vis, s_n, -jnp.inf)
        m = jnp.maximum(jnp.max(s_p, axis=1, keepdims=True), jnp.max(s_n, axis=1, keepdims=True))
        p_p = jnp.exp2(s_p - m)
        p_n = jnp.exp2(s_n - m)
        l = jnp.sum(p_p, axis=1, keepdims=True) + jnp.sum(p_n, axis=1, keepdims=True)
        acc = (jnp.dot(p_p.astype(_BF16), vp, preferred_element_type=_F32)
               + jnp.dot(p_n.astype(_BF16), vn_ref[:, sl], preferred_element_type=_F32))
        outs.append(_diff_finish(acc, l, t, lam, g, lam_init))
    o_ref[...] = jnp.concatenate(outs, axis=1).astype(o_ref.dtype)


def _attn_sample(q16, k16, v16, past_k, past_v, layer, lam, g, n_seq, t, lam_init):
    width = q16.shape[1]
    heads = width // LANES
    past = past_k.shape[1] // heads
    s0 = layer * n_seq
    new_spec = pl.BlockSpec((t, width), lambda s: (s, 0))
    past_spec = pl.BlockSpec((None, past * heads, LANES), lambda s: (s0 + s, 0, 0))
    blk = 4 * _nbytes((t, width), _BF16) + 2 * _nbytes((past * heads, LANES), _F32)
    return pl.pallas_call(
        functools.partial(_attn_sample_kernel, t=t, past=past, heads=heads, lam_init=lam_init),
        grid=(n_seq,),
        in_specs=[new_spec, new_spec, new_spec, past_spec, past_spec,
                  pl.BlockSpec(lam.shape, lambda s: (0, 0)),
                  pl.BlockSpec((1, V_DIM), lambda s: (0, 0))],
        out_specs=new_spec,
        out_shape=jax.ShapeDtypeStruct((n_seq * t, width), _BF16),
        compiler_params=_params(("parallel",), blk, 6 * _nbytes((2 * t, past), _F32)),
        name="attn_sample",
    )(q16, k16, v16, past_k, past_v, lam, g)


def _res_ln(acc, res, g, b, alpha):
    y = alpha * res + acc
    mu = jnp.mean(y, axis=1, keepdims=True)
    yc = y - mu
    var = jnp.mean(yc * yc, axis=1, keepdims=True)
    return yc * lax.rsqrt(var + LN_EPS) * g + b


def _mm_res_ln_kernel(*refs, nk, alpha, part_starts):
    n_parts = len(part_starts)
    a_refs = refs[:n_parts]
    w_ref, res_ref, g_ref, b_ref, o32_ref, o16_ref = refs[n_parts:n_parts + 6]
    scratch = refs[n_parts + 6:]

    def finish(acc):
        y = _res_ln(acc, res_ref[...], g_ref[...], b_ref[...], alpha)
        o32_ref[...] = y
        o16_ref[...] = y.astype(_BF16)

    a = a_refs[0][...]
    for p in range(1, n_parts):
        a = jnp.where(pl.program_id(0) >= part_starts[p], a_refs[p][...], a)
    part = jnp.dot(a, w_ref[...], preferred_element_type=_F32)
    if nk == 1:
        finish(part)
        return
    acc_ref, = scratch
    kk = pl.program_id(1)

    @pl.when(kk == 0)
    def _():
        acc_ref[...] = part

    @pl.when(kk > 0)
    def _():
        acc_ref[...] += part

    @pl.when(kk == nk - 1)
    def _():
        finish(acc_ref[...])


def _mm_res_ln(parts, w16, res, g, b, alpha, name):
    m = sum(a.shape[0] for a in parts)
    k = parts[0].shape[1]
    d = w16.shape[1]
    resident = _nbytes((k, d), _BF16) <= V7X_VMEM_REQUEST_CAP // 2
    tk = k if resident else _tile(k, 512)
    nk = k // tk
    tm = _tile(functools.reduce(math.gcd, [a.shape[0] for a in parts]), 512 if k <= 2048 else 256)
    blk = (len(parts) * _nbytes((tm, tk), _BF16) + 2 * _nbytes((tm, d), _F32) + _nbytes((tm, d), _BF16))
    w_bytes = _nbytes((tk, d), _BF16) * (1 if resident else 2)
    w_mode = dict(pipeline_mode=pl.Buffered(1)) if resident else {}
    scratch = [pltpu.VMEM((tm, d), _F32)] if nk > 1 else []
    part_starts, part_specs, start = [], [], 0
    for a in parts:
        nblk = a.shape[0] // tm
        part_starts.append(start)
        part_specs.append(pl.BlockSpec(
            (tm, tk), lambda i, kk, start=start, nblk=nblk: (jnp.clip(i - start, 0, nblk - 1), kk)))
        start += nblk
    return pl.pallas_call(
        functools.partial(_mm_res_ln_kernel, nk=nk, alpha=alpha, part_starts=tuple(part_starts)),
        grid=(m // tm, nk),
        in_specs=part_specs + [
                  pl.BlockSpec((tk, d), lambda i, kk: (kk, 0), **w_mode),
                  pl.BlockSpec((tm, d), lambda i, kk: (i, 0)),
                  pl.BlockSpec((1, d), lambda i, kk: (0, 0)),
                  pl.BlockSpec((1, d), lambda i, kk: (0, 0))],
        out_specs=[pl.BlockSpec((tm, d), lambda i, kk: (i, 0)),
                   pl.BlockSpec((tm, d), lambda i, kk: (i, 0))],
        out_shape=[jax.ShapeDtypeStruct((m, d), _F32), jax.ShapeDtypeStruct((m, d), _BF16)],
        scratch_shapes=scratch,
        compiler_params=_params(("parallel", "arbitrary"), blk, w_bytes + 2 * _nbytes((tm, d), _F32)),
        name=name,
    )(*parts, w16, res, g.reshape(1, d), b.reshape(1, d))


def _swiglu(g, u):
    return g * jax.nn.sigmoid(g) * u


def _gate_up_kernel(x_ref, wg_ref, wu_ref, h_ref):
    x = x_ref[...]
    g = jnp.dot(x, wg_ref[...], preferred_element_type=_F32)
    u = jnp.dot(x, wu_ref[...], preferred_element_type=_F32)
    h_ref[...] = _swiglu(g, u).astype(h_ref.dtype)


def _gate_up(x16, w_gu16):
    m, k = x16.shape
    f = w_gu16.shape[1] // 2
    tm = _tile(m, 1024)
    tf = _tile(f, 512)
    nf = f // tf
    blk = _nbytes((tm, k), _BF16) + 2 * _nbytes((k, tf), _BF16) + _nbytes((tm, tf), _BF16)
    return pl.pallas_call(
        _gate_up_kernel,
        grid=(m // tm, nf),
        in_specs=[pl.BlockSpec((tm, k), lambda i, j: (i, 0)),
                  pl.BlockSpec((k, tf), lambda i, j: (0, j)),
                  pl.BlockSpec((k, tf), lambda i, j: (0, nf + j))],
        out_specs=pl.BlockSpec((tm, tf), lambda i, j: (i, j)),
        out_shape=jax.ShapeDtypeStruct((m, f), _BF16),
        compiler_params=_params(("parallel", "parallel"), blk, 3 * _nbytes((tm, tf), _F32)),
        name="ffn_gate_up",
    )(x16, w_gu16, w_gu16)


def _pool_kernel(halo_ref, u_ref, w_ref, scale_ref, z_ref, ext_ref, *, tm, pos0, zero_first):
    i = pl.program_id(1)
    halo = halo_ref[...]
    if zero_first:
        halo = jnp.where(i == 0, jnp.zeros_like(halo), halo)
    ext_ref[0:HALO, :] = halo
    ext_ref[HALO:HALO + tm, :] = u_ref[...]
    pos = pos0 + i * tm + lax.broadcasted_iota(jnp.int32, (tm, 1), 0)
    gd = u_ref.shape[1] // len(POOL_WINDOWS)
    for g, w in enumerate(POOL_WINDOWS):
        cols = slice(g * gd, (g + 1) * gd)
        win = ext_ref[HALO:HALO + tm, cols]
        for back in range(1, w):
            win = win + ext_ref[HALO - back:HALO - back + tm, cols]
        count = jnp.minimum(w, pos + 1).astype(_F32)
        d = win / count - u_ref[:, cols]
        zg = jnp.dot(d.astype(_BF16), w_ref[g], preferred_element_type=_F32)
        z_ref[:, cols] = (zg * scale_ref[:, cols]).astype(z_ref.dtype)


def _pool_call(halo_arr, halo_spec, u, u_spec, w_grp16, scale, grid, tm, pos0, zero_first,
               out_rows, out_spec, name):
    d = u.shape[1]
    blk = (_nbytes((HALO, d), _F32) + _nbytes((tm, d), _F32) + _nbytes(w_grp16.shape, _BF16)
           + _nbytes((tm, d), _BF16))
    return pl.pallas_call(
        functools.partial(_pool_kernel, tm=tm, pos0=pos0, zero_first=zero_first),
        grid=grid,
        in_specs=[halo_spec, u_spec,
                  pl.BlockSpec(w_grp16.shape, lambda b, i: (0, 0, 0)),
                  pl.BlockSpec((1, d), lambda b, i: (0, 0))],
        out_specs=out_spec,
        out_shape=jax.ShapeDtypeStruct((out_rows, d), _BF16),
        scratch_shapes=[pltpu.VMEM((HALO + tm, d), _F32)],
        compiler_params=_params(("parallel", "arbitrary"), blk, 3 * _nbytes((HALO + tm, d), _F32)),
        name=name,
    )(halo_arr, u, w_grp16, scale.reshape(1, d))


def _pool_prompt(u, w_grp16, scale, batch, seq):
    d = u.shape[1]
    tm = _tile(seq, 256)
    nt = seq // tm
    per = tm // HALO
    halo_spec = pl.BlockSpec((HALO, d), lambda b, i: (jnp.maximum((b * nt + i) * per - 1, 0), 0))
    u_spec = pl.BlockSpec((tm, d), lambda b, i: (b * nt + i, 0))
    out_spec = pl.BlockSpec((tm, d), lambda b, i: (b * nt + i, 0))
    return _pool_call(u, halo_spec, u, u_spec, w_grp16, scale, (batch, nt), tm, 0, True,
                      batch * seq, out_spec, "pool_prompt")


def _pool_sample(u, ctx, layer, w_grp16, scale, row0, n_seq, t, pos0):
    d = u.shape[1]
    r0 = row0 // t
    s0 = layer * n_seq
    halo_spec = pl.BlockSpec((None, HALO, d), lambda s, i: (s0 + s, 0, 0))
    u_spec = pl.BlockSpec((t, d), lambda s, i: (r0 + s, 0))
    out_spec = pl.BlockSpec((t, d), lambda s, i: (s, 0))
    return _pool_call(ctx, halo_spec, u, u_spec, w_grp16, scale, (n_seq, 1), t, pos0, False,
                      n_seq * t, out_spec, "pool_sample")


def _split_bf16(x):
    hi = x.astype(_BF16)
    lo = (x - hi.astype(_F32)).astype(_BF16)
    return hi, lo


def _router_kernel(x_ref, w_ref, meta_ref, count_ref, carry_ref, *, tm, n_experts):
    i = pl.program_id(0)

    @pl.when(i == 0)
    def _():
        carry_ref[...] = jnp.zeros_like(carry_ref)

    xh, xl = _split_bf16(x_ref[...])
    wh, wl = _split_bf16(w_ref[...])
    logits = (jnp.dot(xh, wh, preferred_element_type=_F32)
              + (jnp.dot(xh, wl, preferred_element_type=_F32)
                 + jnp.dot(xl, wh, preferred_element_type=_F32)))
    lane = lax.broadcasted_iota(jnp.int32, (tm, LANES), 1)
    lg = jnp.where(lane < n_experts, logits, -jnp.inf)
    m1 = jnp.max(lg, axis=1, keepdims=True)
    i1 = jnp.min(jnp.where(lg == m1, lane, LANES), axis=1, keepdims=True)
    lg2 = jnp.where(lane == i1, -jnp.inf, lg)
    m2 = jnp.max(lg2, axis=1, keepdims=True)
    i2 = jnp.min(jnp.where(lg2 == m2, lane, LANES), axis=1, keepdims=True)
    e = jnp.exp(m2 - m1)
    g1 = 1.0 / (1.0 + e)
    g2 = e / (1.0 + e)

    sel1 = lane == i1
    sel2 = lane == i2
    cnt = jnp.where(sel1, 1.0, 0.0) + jnp.where(sel2, 1.0, 0.0)
    row = lax.broadcasted_iota(jnp.int32, (tm, tm), 0)
    col = lax.broadcasted_iota(jnp.int32, (tm, tm), 1)
    lower = jnp.where(col < row, 1.0, 0.0).astype(_BF16)
    before = jnp.dot(lower, cnt.astype(_BF16), preferred_element_type=_F32) + carry_ref[0:1, :]
    r1 = jnp.sum(jnp.where(sel1, before, 0.0), axis=1, keepdims=True)
    r2 = jnp.sum(jnp.where(sel2, before, 0.0), axis=1, keepdims=True)
    carry_ref[0:1, :] = carry_ref[0:1, :] + jnp.sum(cnt, axis=0, keepdims=True)
    count_ref[...] = carry_ref[...]

    meta = jnp.zeros((tm, LANES), _F32)
    for k, v in enumerate((i1.astype(_F32), i2.astype(_F32), r1, r2, g1, g2)):
        meta = jnp.where(lane == k, v, meta)
    meta_ref[...] = meta


def _router(x32, w_router):
    n, d = x32.shape
    n_experts = w_router.shape[1]
    tm = _tile(n, 256)
    w_pad = jnp.pad(w_router, ((0, 0), (0, LANES - n_experts)))
    blk = _nbytes((tm, d), _F32) + _nbytes((d, LANES), _F32) + 2 * _nbytes((tm, LANES), _F32)
    return pl.pallas_call(
        functools.partial(_router_kernel, tm=tm, n_experts=n_experts),
        grid=(n // tm,),
        in_specs=[pl.BlockSpec((tm, d), lambda i: (i, 0)),
                  pl.BlockSpec((d, LANES), lambda i: (0, 0))],
        out_specs=[pl.BlockSpec((tm, LANES), lambda i: (i, 0)),
                   pl.BlockSpec((8, LANES), lambda i: (0, 0))],
        out_shape=[jax.ShapeDtypeStruct((n, LANES), _F32), jax.ShapeDtypeStruct((8, LANES), _F32)],
        scratch_shapes=[pltpu.VMEM((8, LANES), _F32)],
        compiler_params=_params(("arbitrary",), blk, 4 * _nbytes((tm, d), _F32)),
        name="moe_router",
    )(x32, w_pad)


def _row_copy(src_hbm, row, dst_vmem, r, sem):
    return pltpu.make_async_copy(src_hbm.at[pl.ds(row, 1), :], dst_vmem.at[pl.ds(r, 1), :], sem)


def _gather_rows(src_hbm, idx_ref, base, dst_vmem, sem, n):
    def issue(i, c):
        for u in range(2):
            r = 2 * i + u
            _row_copy(src_hbm, idx_ref[base + r], dst_vmem, r, sem).start(priority=u)
        return c
    lax.fori_loop(0, n // 2, issue, 0, unroll=2)


def _wait_rows(src_hbm, dst_vmem, sem, n):
    def drain(r, c):
        _row_copy(src_hbm, 0, dst_vmem, r, sem).wait()
        return c
    lax.fori_loop(0, n, drain, 0, unroll=8)


def _moe_gather_kernel(tok_ref, nu_ref, x_hbm, xs_hbm, in_ref, out_ref, in_sem, out_sem, *, tg):
    n_used = nu_ref[0]

    def out_copy(b, slot):
        return pltpu.make_async_copy(out_ref.at[slot], xs_hbm.at[pl.ds(pl.multiple_of(b * tg, tg), tg), :],
                                     out_sem.at[slot])

    _gather_rows(x_hbm, tok_ref, 0, in_ref.at[0], in_sem.at[0], tg)

    def body(b, c):
        slot = b % 2

        @pl.when(b + 1 < n_used)
        def _():
            _gather_rows(x_hbm, tok_ref, (b + 1) * tg, in_ref.at[1 - slot], in_sem.at[1 - slot], tg)

        _wait_rows(x_hbm, in_ref.at[slot], in_sem.at[slot], tg)

        @pl.when(b >= 2)
        def _():
            out_copy(b - 2, slot).wait()

        out_ref[slot] = in_ref[slot].astype(out_ref.dtype)
        out_copy(b, slot).start()
        return c

    lax.fori_loop(0, n_used, body, 0)

    @pl.when(n_used >= 2)
    def _():
        out_copy(n_used - 2, n_used % 2).wait()

    out_copy(n_used - 1, (n_used - 1) % 2).wait()

    n_blocks = xs_hbm.shape[0] // tg
    out_ref[0] = jnp.zeros(out_ref.shape[1:], out_ref.dtype)

    def start_zero(b, c):
        out_copy(b, 0).start()
        return c

    def wait_zero(b, c):
        out_copy(b, 0).wait()
        return c

    lax.fori_loop(n_used, n_blocks, start_zero, 0)
    lax.fori_loop(n_used, n_blocks, wait_zero, 0)


def _moe_gather(x32, row_tok, n_used):
    rows = row_tok.shape[0]
    d = x32.shape[1]
    tg = MOE_BLOCK
    return pl.pallas_call(
        functools.partial(_moe_gather_kernel, tg=tg),
        grid_spec=pltpu.PrefetchScalarGridSpec(
            num_scalar_prefetch=2,
            grid=(1,),
            in_specs=[pl.BlockSpec(memory_space=pl.ANY)],
            out_specs=pl.BlockSpec(memory_space=pl.ANY),
            scratch_shapes=[pltpu.VMEM((2, tg, d), _F32), pltpu.VMEM((2, tg, d), _BF16),
                            pltpu.SemaphoreType.DMA((2,)), pltpu.SemaphoreType.DMA((2,))]),
        out_shape=jax.ShapeDtypeStruct((rows, d), _BF16),
        compiler_params=_params(("arbitrary",), 0, 2 * _nbytes((tg, d), _F32) + 2 * _nbytes((tg, d), _BF16)),
        name="moe_gather",
    )(row_tok, n_used, x32)


def _grouped_rows(gs_ref, gb_ref, e, src_hbm, dst_hbm, in_ref, out_ref, in_sem, out_sem, col, compute, tm, big):
    assert big in (1, 2, 4)
    r0 = gs_ref[e]
    nblk = gb_ref[e]
    n_main = nblk // big
    tails = [big >> k for k in range(1, big.bit_length())]

    def rows(off, nb):
        return pl.ds(pl.multiple_of(r0 + off * tm, tm), nb * tm)

    def in_copy(off, nb, slot):
        return pltpu.make_async_copy(src_hbm.at[rows(off, nb), :], in_ref.at[slot, pl.ds(0, nb * tm), :],
                                     in_sem.at[slot])

    def out_copy(off, nb, slot):
        return pltpu.make_async_copy(out_ref.at[slot, pl.ds(0, nb * tm), :], dst_hbm.at[rows(off, nb), col],
                                     out_sem.at[slot])

    def tail_off(t):
        return (nblk // (2 * t)) * (2 * t)

    def start_first_tail(first, slot, enable):
        pending = enable
        for t in tails[first:]:
            present = (nblk & t) != 0

            @pl.when(jnp.logical_and(pending, present))
            def _():
                in_copy(tail_off(t), t, slot).start(priority=ROW_DMA_PRIORITY)

            pending = jnp.logical_and(pending, jnp.logical_not(present))

    @pl.when(n_main > 0)
    def _():
        in_copy(0, big, 0).start(priority=ROW_DMA_PRIORITY)

    start_first_tail(0, 0, n_main == 0)

    def body(b, c):
        slot = b % 2
        in_copy(b * big, big, slot).wait()

        @pl.when(b + 1 < n_main)
        def _():
            in_copy((b + 1) * big, big, 1 - slot).start(priority=ROW_DMA_PRIORITY)

        start_first_tail(0, 1 - slot, b + 1 == n_main)

        @pl.when(b >= 2)
        def _():
            out_copy((b - 2) * big, big, slot).wait()

        out_ref[slot] = compute(in_ref[slot])
        out_copy(b * big, big, slot).start()
        return c

    lax.fori_loop(0, n_main, body, 0)

    @pl.when(n_main >= 2)
    def _():
        out_copy((n_main - 2) * big, big, n_main % 2).wait()

    @pl.when(n_main >= 1)
    def _():
        out_copy((n_main - 1) * big, big, (n_main - 1) % 2).wait()

    slot = n_main % 2
    tail_slots = []
    for k, t in enumerate(tails):
        present = (nblk & t) != 0
        tail_slots.append(slot)
        cur = slot

        @pl.when(present)
        def _():
            in_copy(tail_off(t), t, cur).wait()
            start_first_tail(k + 1, 1 - cur, True)
            out_ref[cur, 0:t * tm, :] = compute(in_ref[cur, 0:t * tm, :])
            out_copy(tail_off(t), t, cur).start()

        slot = jnp.where(present, 1 - slot, slot)

    for k, t in enumerate(tails):
        @pl.when((nblk & t) != 0)
        def _():
            out_copy(tail_off(t), t, tail_slots[k]).wait()


def _zero_rows(gs_ref, gb_ref, e, dst_hbm, out_ref, out_sem, col, tm):
    r0 = gs_ref[e]
    out_ref[0, 0:tm, :] = jnp.zeros((tm, out_ref.shape[2]), out_ref.dtype)

    def copy(b):
        return pltpu.make_async_copy(out_ref.at[0, pl.ds(0, tm), :],
                                     dst_hbm.at[pl.ds(pl.multiple_of(r0 + b * tm, tm), tm), col], out_sem.at[0])

    def start(b, c):
        copy(b).start()
        return c

    def wait(b, c):
        copy(b).wait()
        return c

    lax.fori_loop(0, gb_ref[e], start, 0)
    lax.fori_loop(0, gb_ref[e], wait, 0)


def _moe_gate_up_kernel(gs_ref, gb_ref, xs_hbm, wg_ref, wu_ref, h_hbm, wg16_ref, wu16_ref,
                        in_ref, out_ref, in_sem, out_sem, *, tm, tf, n_experts):
    j = pl.program_id(0)
    e = pl.program_id(1)
    col = pl.ds(pl.multiple_of(j * tf, tf), tf)

    def compute(x):
        g = jnp.dot(x, wg16_ref[...], preferred_element_type=_F32)
        u = jnp.dot(x, wu16_ref[...], preferred_element_type=_F32)
        return _swiglu(g, u).astype(_BF16)

    @pl.when(e < n_experts)
    def _():
        wg16_ref[...] = wg_ref[...].astype(_BF16)
        wu16_ref[...] = wu_ref[...].astype(_BF16)
        _grouped_rows(gs_ref, gb_ref, e, xs_hbm, h_hbm, in_ref, out_ref, in_sem, out_sem, col, compute, tm,
                      in_ref.shape[1] // tm)

    @pl.when(e == n_experts)
    def _():
        _zero_rows(gs_ref, gb_ref, e, h_hbm, out_ref, out_sem, col, tm)


def _moe_gate_up(xs16, w_gu, gstart, gblocks):
    rows, d = xs16.shape
    n_experts = w_gu.shape[0]
    f = w_gu.shape[2] // 2
    tm = MOE_BLOCK
    tf = _tile(f, 512)
    nf = f // tf
    blk = 2 * _nbytes((d, tf), _F32)
    tc = MOE_CHUNK_BLOCKS * tm
    scratch_bytes = (2 * _nbytes((d, tf), _BF16) + 2 * _nbytes((tc, d), _BF16) + 2 * _nbytes((tc, tf), _BF16)
                     + 3 * _nbytes((tc, tf), _F32))
    last = n_experts - 1
    return pl.pallas_call(
        functools.partial(_moe_gate_up_kernel, tm=tm, tf=tf, n_experts=n_experts),
        grid_spec=pltpu.PrefetchScalarGridSpec(
            num_scalar_prefetch=2,
            grid=(nf, n_experts + 1),
            in_specs=[pl.BlockSpec(memory_space=pl.ANY),
                      pl.BlockSpec((None, d, tf), lambda j, e, gs, gb: (jnp.minimum(e, last), 0, j)),
                      pl.BlockSpec((None, d, tf), lambda j, e, gs, gb: (jnp.minimum(e, last), 0, nf + j))],
            out_specs=pl.BlockSpec(memory_space=pl.ANY),
            scratch_shapes=[pltpu.VMEM((d, tf), _BF16), pltpu.VMEM((d, tf), _BF16),
                            pltpu.VMEM((2, tc, d), _BF16), pltpu.VMEM((2, tc, tf), _BF16),
                            pltpu.SemaphoreType.DMA((2,)), pltpu.SemaphoreType.DMA((2,))]),
        out_shape=jax.ShapeDtypeStruct((rows, f), _BF16),
        compiler_params=_params(("arbitrary", "arbitrary"), blk, scratch_bytes),
        name="moe_gate_up",
    )(gstart, gblocks, xs16, w_gu, w_gu)


def _moe_down_kernel(gs_ref, gb_ref, h_hbm, w_ref, y_hbm, w16_ref, in_ref, out_ref, in_sem, out_sem,
                     *, tm, tn, n_experts):
    j = pl.program_id(0)
    e = pl.program_id(1)
    col = pl.ds(pl.multiple_of(j * tn, tn), tn)

    def compute(h):
        return jnp.dot(h, w16_ref[...], preferred_element_type=_F32)

    @pl.when(e < n_experts)
    def _():
        w16_ref[...] = w_ref[...].astype(_BF16)
        _grouped_rows(gs_ref, gb_ref, e, h_hbm, y_hbm, in_ref, out_ref, in_sem, out_sem, col, compute, tm,
                      in_ref.shape[1] // tm)

    @pl.when(e == n_experts)
    def _():
        _zero_rows(gs_ref, gb_ref, e, y_hbm, out_ref, out_sem, col, tm)


def _moe_down(h16, w_down, gstart, gblocks):
    rows, f = h16.shape
    n_experts = w_down.shape[0]
    d = w_down.shape[2]
    tm = MOE_BLOCK
    tn = _tile(d, 512)
    blk = _nbytes((f, tn), _F32)
    scratch_bytes = (_nbytes((f, tn), _BF16) + 2 * _nbytes((tm, f), _BF16) + 3 * _nbytes((tm, tn), _F32))
    last = n_experts - 1
    return pl.pallas_call(
        functools.partial(_moe_down_kernel, tm=tm, tn=tn, n_experts=n_experts),
        grid_spec=pltpu.PrefetchScalarGridSpec(
            num_scalar_prefetch=2,
            grid=(d // tn, n_experts + 1),
            in_specs=[pl.BlockSpec(memory_space=pl.ANY),
                      pl.BlockSpec((None, f, tn), lambda j, e, gs, gb: (jnp.minimum(e, last), 0, j))],
            out_specs=pl.BlockSpec(memory_space=pl.ANY),
            scratch_shapes=[pltpu.VMEM((f, tn), _BF16),
                            pltpu.VMEM((2, tm, f), _BF16), pltpu.VMEM((2, tm, tn), _F32),
                            pltpu.SemaphoreType.DMA((2,)), pltpu.SemaphoreType.DMA((2,))]),
        out_shape=jax.ShapeDtypeStruct((rows, d), _F32),
        compiler_params=_params(("arbitrary", "arbitrary"), blk, scratch_bytes),
        name="moe_down",
    )(gstart, gblocks, h16, w_down)


def _moe_combine_kernel(d1_ref, d2_ref, y_hbm, meta_ref, res_ref, g_ref, b_ref, *rest, tc, alpha, n_split):
    outs, (buf_ref, sem) = rest[:-2], rest[-2:]
    i = pl.program_id(0)
    slot = i % 2

    def fetch(step, s):
        _gather_rows(y_hbm, d1_ref, step * tc, buf_ref.at[s, 0], sem.at[s], tc)
        _gather_rows(y_hbm, d2_ref, step * tc, buf_ref.at[s, 1], sem.at[s], tc)

    @pl.when(i == 0)
    def _():
        fetch(0, 0)

    @pl.when(i + 1 < pl.num_programs(0))
    def _():
        fetch(i + 1, 1 - slot)

    _wait_rows(y_hbm, buf_ref.at[slot, 0], sem.at[slot], tc)
    _wait_rows(y_hbm, buf_ref.at[slot, 1], sem.at[slot], tc)
    meta = meta_ref[...]
    f = buf_ref[slot, 0] * meta[:, 4:5] + buf_ref[slot, 1] * meta[:, 5:6]
    y = _res_ln(f, res_ref[...], g_ref[...], b_ref[...], alpha)
    if n_split is None:
        outs[0][...] = y
        outs[1][...] = y.astype(_BF16)
    else:
        @pl.when(i < n_split)
        def _():
            outs[0][...] = y

        @pl.when(i >= n_split)
        def _():
            outs[1][...] = y


def _moe_combine_ln(y_rows, dest1, dest2, meta, res, g, b, alpha, split_rows):
    n, d = res.shape
    tc = _tile(n if split_rows is None else math.gcd(n, split_rows), 128)
    blk = _nbytes((tc, LANES), _F32) + 2 * _nbytes((tc, d), _F32) + _nbytes((tc, d), _BF16)
    tile_spec = pl.BlockSpec((tc, d), lambda i, a, c: (i, 0))
    if split_rows is None:
        n_split = None
        out_specs = [tile_spec, tile_spec]
        out_shape = [jax.ShapeDtypeStruct((n, d), _F32), jax.ShapeDtypeStruct((n, d), _BF16)]
    else:
        n_split = split_rows // tc
        out_specs = [pl.BlockSpec((tc, d), lambda i, a, c: (jnp.minimum(i, n_split - 1), 0)),
                     pl.BlockSpec((tc, d), lambda i, a, c: (jnp.maximum(i - n_split, 0), 0))]
        out_shape = [jax.ShapeDtypeStruct((split_rows, d), _F32),
                     jax.ShapeDtypeStruct((n - split_rows, d), _F32)]
    return pl.pallas_call(
        functools.partial(_moe_combine_kernel, tc=tc, alpha=alpha, n_split=n_split),
        grid_spec=pltpu.PrefetchScalarGridSpec(
            num_scalar_prefetch=2,
            grid=(n // tc,),
            in_specs=[pl.BlockSpec(memory_space=pl.ANY),
                      pl.BlockSpec((tc, LANES), lambda i, a, c: (i, 0)),
                      tile_spec,
                      pl.BlockSpec((1, d), lambda i, a, c: (0, 0)),
                      pl.BlockSpec((1, d), lambda i, a, c: (0, 0))],
            out_specs=out_specs,
            scratch_shapes=[pltpu.VMEM((2, 2, tc, d), _F32), pltpu.SemaphoreType.DMA((2,))]),
        out_shape=out_shape,
        compiler_params=_params(("arbitrary",), blk, 8 * _nbytes((tc, d), _F32)),
        name="moe_combine_ln",
    )(dest1, dest2, y_rows, meta, res, g.reshape(1, d), b.reshape(1, d))


def _moe_layer(x32, w_router, w_gu, w_down, g, b, alpha, split_rows):
    n, d = x32.shape
    n_experts = w_router.shape[1]
    meta, counts = _router(x32, w_router)
    e1 = meta[:, 0].astype(jnp.int32)
    e2 = meta[:, 1].astype(jnp.int32)
    counts = counts[0, :n_experts].astype(jnp.int32)
    gblocks = (counts + MOE_BLOCK - 1) // MOE_BLOCK
    gend = jnp.cumsum(gblocks) * MOE_BLOCK
    gstart = gend - gblocks * MOE_BLOCK
    dest1 = gstart[e1] + meta[:, 2].astype(jnp.int32)
    dest2 = gstart[e2] + meta[:, 3].astype(jnp.int32)
    nb = (n * TOP_K) // MOE_BLOCK + n_experts
    rows = nb * MOE_BLOCK
    tok = jnp.arange(n, dtype=jnp.int32)
    row_tok = jnp.zeros((rows,), jnp.int32).at[dest1].set(tok).at[dest2].set(tok)
    n_used = (gend[-1] // MOE_BLOCK).astype(jnp.int32).reshape(1)
    gstart = jnp.concatenate([gstart, gend[-1:]]).astype(jnp.int32)
    gblocks = jnp.concatenate([gblocks, nb - n_used]).astype(jnp.int32)

    xs16 = _moe_gather(x32, row_tok, n_used)
    h16 = _moe_gate_up(xs16, w_gu, gstart, gblocks)
    y_rows = _moe_down(h16, w_down, gstart, gblocks)
    return _moe_combine_ln(y_rows, dest1, dest2, meta, x32, g, b, alpha, split_rows)


def kernel(x_prompt, x_sample, cache_k, cache_v, state_pool, attn_w_in, attn_w_o, attn_lambda, attn_subln_g, pool_w_in, pool_w_grp, pool_scale, pool_w_o, ln_mix_g, ln_mix_b, ln_ffn_g, ln_ffn_b, ffn_w_gu, ffn_w_down, moe_w_router, moe_w_gu, moe_w_down):
    batch, seq, d = x_prompt.shape
    n_seq, t, _ = x_sample.shape
    past = cache_k.shape[2]
    depth = ln_mix_g.shape[0]
    alpha = (2 * depth) ** 0.25
    n_p = batch * seq
    n_s = n_seq * t
    n = n_p + n_s
    n_heads = cache_k.shape[3]
    qk_width = n_heads * 2 * HEAD_DIM
    v_width = n_heads * V_DIM

    x32 = jnp.concatenate([x_prompt.reshape(n_p, d), x_sample.reshape(n_s, d)], axis=0)
    x16 = x32.astype(_BF16)
    tab_p = _rope_tables(np.arange(seq))
    tab_s = _rope_tables(np.tile(past + np.arange(t), n_seq))
    past_k = cache_k.reshape(-1, past * n_heads, 2 * HEAD_DIM)
    past_v = cache_v.reshape(-1, past * n_heads, V_DIM)
    ctx = jnp.pad(state_pool, ((0, 0), (0, 0), (HALO - POOL_CTX, 0), (0, 0))).reshape(-1, HALO, d)

    kp_l, vp_l, ks_l, vs_l, pp_l, ps_l = [], [], [], [], [], []
    y_p = y_s = None
    for i in range(depth):
        j = i // 2
        last = i == depth - 1
        if i % 2 == 0:
            lam_init = 0.8 - 0.6 * math.exp(-0.3 * i)
            w_in16 = attn_w_in[j].astype(_BF16)
            g = attn_subln_g[j].reshape(1, V_DIM)
            rows_p, rows_s = (0, n_p, tab_p), (n_p, n_s, tab_s)
            (q_p,), (q_s,) = (_proj(x16, w_in16, 0, qk_width, r0, nr, tab, False, True, "proj_q", Q_SCALE)
                              for r0, nr, tab in (rows_p, rows_s))
            (k32_p, k_p), (k32_s, k_s) = (_proj(x16, w_in16, qk_width, qk_width, r0, nr, tab, True, True, "proj_k")
                                          for r0, nr, tab in (rows_p, rows_s))
            (v32_p, v_p), (v32_s, v_s) = (_proj(x16, w_in16, 2 * qk_width, v_width, r0, nr, None, True, True, "proj_v")
                                          for r0, nr, tab in (rows_p, rows_s))
            mix = [_attn_prompt(q_p, k_p, v_p, attn_lambda[j], g, batch, seq, lam_init),
                   _attn_sample(q_s, k_s, v_s, past_k, past_v, j, attn_lambda[j], g, n_seq, t, lam_init)]
            w_o16 = attn_w_o[j].astype(_BF16)
            kp_l.append(k32_p.reshape(batch, seq, n_heads, 2 * HEAD_DIM))
            vp_l.append(v32_p.reshape(batch, seq, n_heads, V_DIM))
            ks_l.append(k32_s.reshape(n_seq, t, n_heads, 2 * HEAD_DIM))
            vs_l.append(v32_s.reshape(n_seq, t, n_heads, V_DIM))
        else:
            u32, = _proj(x16, pool_w_in[j].astype(_BF16), 0, d, 0, n, None, True, False, "pool_in")
            w_grp16 = pool_w_grp[j].astype(_BF16)
            mix = [_pool_prompt(u32, w_grp16, pool_scale[j], batch, seq),
                   _pool_sample(u32, ctx, j, w_grp16, pool_scale[j], n_p, n_seq, t, past)]
            w_o16 = pool_w_o[j].astype(_BF16)
            u_p = u32[:n_p].reshape(batch, seq, d)
            u_s = u32[n_p:].reshape(n_seq, t, d)
            pp_l.append(jnp.concatenate([jnp.zeros((batch, POOL_CTX, d), _F32), u_p[:, -POOL_CTX:]], 1)[:, -POOL_CTX:])
            ps_l.append(jnp.concatenate([state_pool[j], u_s[:, -POOL_CTX:]], 1)[:, -POOL_CTX:])
        x32, x16 = _mm_res_ln(mix, w_o16, x32, ln_mix_g[i], ln_mix_b[i], alpha, "mix_out_ln")
        if i % 2 == 0:
            h16 = _gate_up(x16, ffn_w_gu[j].astype(_BF16))
            x32, x16 = _mm_res_ln([h16], ffn_w_down[j].astype(_BF16), x32, ln_ffn_g[i], ln_ffn_b[i],
                                  alpha, "ffn_down_ln")
        elif last:
            y_p, y_s = _moe_layer(x32, moe_w_router[j], moe_w_gu[j], moe_w_down[j],
                                  ln_ffn_g[i], ln_ffn_b[i], alpha, n_p)
        else:
            x32, x16 = _moe_layer(x32, moe_w_router[j], moe_w_gu[j], moe_w_down[j],
                                  ln_ffn_g[i], ln_ffn_b[i], alpha, None)
    if y_p is None:
        y_p, y_s = x32[:n_p], x32[n_p:]
    return (y_p.reshape(batch, seq, d), y_s.reshape(n_seq, t, d), jnp.stack(kp_l), jnp.stack(vp_l),
            jnp.stack(pp_l), jnp.stack(ks_l), jnp.stack(vs_l), jnp.stack(ps_l))
```

```python
import functools
import math

import numpy as np
import jax
import jax.numpy as jnp
from jax import lax
from jax.experimental import pallas as pl
from jax.experimental.pallas import tpu as pltpu

CHUNK = 64
HEAD_DIM = 64
V_DIM = 2 * HEAD_DIM
ROT_DIM = HEAD_DIM // 4
ROPE_THETA = 500000.0
POOL_WINDOWS = (2, 4, 8, 16)
POOL_CTX = max(POOL_WINDOWS) - 1
TOP_K = 2
LN_EPS = 1e-5

LANES = 128
V7X_VMEM_REQUEST_CAP = 56 * 1024 * 1024
COMPILER_SCRATCH_ALLOWANCE = 6 * 1024 * 1024

HALO = 16
MOE_BLOCK = 256
MOE_CHUNK_BLOCKS = 2
ROW_DMA_PRIORITY = 1
Q_SCALE = HEAD_DIM ** -0.5 * math.log2(math.e)

_F32 = jnp.float32
_BF16 = jnp.bfloat16
_NT = (((1,), (1,)), ((), ()))


def _nbytes(shape, dtype):
    return math.prod(shape) * jnp.dtype(dtype).itemsize


def _params(semantics, pipelined_bytes, scratch_bytes=0):
    need = 2 * pipelined_bytes + scratch_bytes + COMPILER_SCRATCH_ALLOWANCE
    return pltpu.CompilerParams(dimension_semantics=semantics,
                                vmem_limit_bytes=min(need, V7X_VMEM_REQUEST_CAP))


def _tile(n, pref):
    t = min(n, pref)
    while n % t:
        t //= 2
    return t


def _rope(acc, cos, sin_lo, sin_hi):
    half = ROT_DIM // 2
    pieces = []
    for g in range(acc.shape[1] // LANES):
        xg = acc[:, g * LANES:(g + 1) * LANES]
        pieces.append(xg * cos
                      + pltpu.roll(xg, half, 1) * sin_hi
                      + pltpu.roll(xg, LANES - half, 1) * sin_lo)
    return pieces[0] if len(pieces) == 1 else jnp.concatenate(pieces, axis=1)


def _proj_kernel(*refs, rope, want32, want16, scale):
    x_ref, w_ref = refs[0], refs[1]
    acc = jnp.dot(x_ref[...], w_ref[...], preferred_element_type=_F32)
    pos = 2
    if rope:
        acc = _rope(acc, refs[2][...], refs[3][...], refs[4][...])
        pos = 5
    if scale is not None:
        acc = acc * scale
    if want32:
        refs[pos][...] = acc
        pos += 1
    if want16:
        refs[pos][...] = acc.astype(_BF16)


def _proj(xb, w, col0, ncols, row0, nrows, tables, want32, want16, name, scale=None):
    k = xb.shape[1]
    period = nrows if tables is None else tables[0].shape[0]
    tm = _tile(math.gcd(math.gcd(row0, nrows), period) if row0 else math.gcd(nrows, period), 1024)
    tn = _tile(ncols, 512)
    off = col0 // tn
    rb0 = row0 // tm
    tper = period // tm
    in_specs = [pl.BlockSpec((tm, k), lambda i, j: (rb0 + i, 0)),
                pl.BlockSpec((k, tn), lambda i, j: (0, j + off))]
    args = [xb, w]
    blk = _nbytes((tm, k), _BF16) + _nbytes((k, tn), _BF16)
    if tables is not None:
        for t in tables:
            in_specs.append(pl.BlockSpec((tm, LANES), lambda i, j: (i % tper, 0)))
            args.append(t)
            blk += _nbytes((tm, LANES), _F32)
    out_shape, out_specs = [], []
    for want, dt in ((want32, _F32), (want16, _BF16)):
        if want:
            out_shape.append(jax.ShapeDtypeStruct((nrows, ncols), dt))
            out_specs.append(pl.BlockSpec((tm, tn), lambda i, j: (i, j)))
            blk += _nbytes((tm, tn), dt)
    return pl.pallas_call(
        functools.partial(_proj_kernel, rope=tables is not None, want32=want32, want16=want16,
                          scale=scale),
        grid=(nrows // tm, ncols // tn),
        in_specs=in_specs, out_specs=out_specs, out_shape=out_shape,
        compiler_params=_params(("parallel", "parallel"), blk, _nbytes((tm, tn), _F32)),
        name=name,
    )(*args)


def _rope_tables(pos):
    half = ROT_DIM // 2
    inv_freq = ROPE_THETA ** (-np.arange(0, ROT_DIM, 2, dtype=np.float64) / ROT_DIM)
    ang = np.asarray(pos, np.float64)[:, None] * inv_freq[None, :]
    cos, sin = np.cos(ang), np.sin(ang)
    t = ang.shape[0]
    ones = np.ones((t, HEAD_DIM - ROT_DIM))
    zeros = np.zeros((t, HEAD_DIM - ROT_DIM))
    zh = np.zeros((t, half))
    cos_map = np.concatenate([cos, cos, ones], 1)
    lo_map = np.concatenate([-sin, zh, zeros], 1)
    hi_map = np.concatenate([zh, sin, zeros], 1)
    return tuple(jnp.asarray(np.concatenate([a, a], 1), _F32) for a in (cos_map, lo_map, hi_map))


def _stack_maps(q):
    lane = lax.broadcasted_iota(jnp.int32, q.shape, 1)
    zero = jnp.zeros_like(q)
    return jnp.concatenate([jnp.where(lane < HEAD_DIM, q, zero),
                            jnp.where(lane >= HEAD_DIM, q, zero)], axis=0)


def _diff_lambda(lam_ref, lam_init):
    lf = lam_ref[...]
    a = jnp.sum(lf[0:1] * lf[1:2], axis=1, keepdims=True)
    b = jnp.sum(lf[2:3] * lf[3:4], axis=1, keepdims=True)
    return jnp.exp(a) - jnp.exp(b) + lam_init


def _diff_finish(acc, l, tq, lam, g, lam_init):
    o = acc / l
    o = o[:tq] - lam * o[tq:]
    ms = jnp.mean(o * o, axis=1, keepdims=True)
    return (o * lax.rsqrt(ms + LN_EPS) * g) * (1.0 - lam_init)


def _attn_prompt_kernel(q_ref, k_ref, v_ref, lam_ref, g_ref, o_ref, sa_ref, sb_ref, *, tq, tk, lam_init):
    q0 = pl.program_id(2) * tq
    qs = _stack_maps(q_ref[...])
    n_full = q0 // tk

    def scores(kb, dst_ref):
        k0 = pl.multiple_of(kb * tk, tk)
        dst_ref[...] = lax.dot_general(qs, k_ref[pl.ds(k0, tk), :], _NT, preferred_element_type=_F32)

    def consume(kb, src_ref, carry, masked):
        m, l, acc = carry
        k0 = pl.multiple_of(kb * tk, tk)
        s = src_ref[...]
        if masked:
            qc = (q0 + lax.broadcasted_iota(jnp.int32, (tq, tk), 0)) // CHUNK
            kc = (k0 + lax.broadcasted_iota(jnp.int32, (tq, tk), 1)) // CHUNK
            vis = kc <= qc
            s = jnp.where(jnp.concatenate([vis, vis], axis=0), s, -jnp.inf)
        m_new = jnp.maximum(m, jnp.max(s, axis=1, keepdims=True))
        p = jnp.exp2(s - m_new)
        alpha = jnp.exp2(m - m_new)
        l = alpha * l + jnp.sum(p, axis=1, keepdims=True)
        acc = alpha * acc + jnp.dot(p.astype(_BF16), v_ref[pl.ds(k0, tk), :], preferred_element_type=_F32)
        return m_new, l, acc

    def pair(p, carry):
        kb = 2 * p
        scores(kb + 1, sb_ref)
        carry = consume(kb, sa_ref, carry, False)
        scores(kb + 2, sa_ref)
        return consume(kb + 1, sb_ref, carry, False)

    def odd_tail(carry):
        scores(n_full, sb_ref)
        carry = consume(n_full - 1, sa_ref, carry, False)
        return consume(n_full, sb_ref, carry, True)

    def even_tail(carry):
        return consume(n_full, sa_ref, carry, True)

    carry = (jnp.full((2 * tq, 1), -jnp.inf, _F32), jnp.zeros((2 * tq, 1), _F32),
             jnp.zeros((2 * tq, V_DIM), _F32))
    scores(0, sa_ref)
    carry = lax.fori_loop(0, n_full // 2, pair, carry)
    _, l, acc = lax.cond(n_full % 2 == 1, odd_tail, even_tail, carry)
    lam = _diff_lambda(lam_ref, lam_init)
    o_ref[...] = _diff_finish(acc, l, tq, lam, g_ref[...], lam_init).astype(o_ref.dtype)


def _attn_prompt(q16, k16, v16, lam, g, batch, seq, lam_init):
    n_heads = q16.shape[1] // LANES
    tq = _tile(seq, 256)
    tk = _tile(seq, 512)
    assert tk % tq == 0
    nq = seq // tq
    blk = (_nbytes((tq, LANES), _BF16) * 2 + 2 * _nbytes((seq, LANES), _BF16))
    return pl.pallas_call(
        functools.partial(_attn_prompt_kernel, tq=tq, tk=tk, lam_init=lam_init),
        grid=(batch, n_heads, nq),
        in_specs=[pl.BlockSpec((tq, LANES), lambda b, h, i: (b * nq + i, h)),
                  pl.BlockSpec((seq, LANES), lambda b, h, i: (b, h)),
                  pl.BlockSpec((seq, LANES), lambda b, h, i: (b, h)),
                  pl.BlockSpec(lam.shape, lambda b, h, i: (0, 0)),
                  pl.BlockSpec((1, V_DIM), lambda b, h, i: (0, 0))],
        out_specs=pl.BlockSpec((tq, LANES), lambda b, h, i: (b * nq + i, h)),
        out_shape=jax.ShapeDtypeStruct((batch * seq, n_heads * V_DIM), _BF16),
        scratch_shapes=[pltpu.VMEM((2 * tq, tk), _F32), pltpu.VMEM((2 * tq, tk), _F32)],
        compiler_params=_params(("parallel", "parallel", "parallel"), blk,
                                6 * _nbytes((2 * tq, tk), _F32)),
        name="attn_prompt",
    )(q16, k16, v16, lam, g)


def _attn_sample_kernel(q_ref, kn_ref, vn_ref, kp_ref, vp_ref, lam_ref, g_ref, o_ref, *,
                        t, past, heads, lam_init):
    lam = _diff_lambda(lam_ref, lam_init)
    g = g_ref[...]
    qc = (past + lax.broadcasted_iota(jnp.int32, (t, t), 0)) // CHUNK
    kc = (past + lax.broadcasted_iota(jnp.int32, (t, t), 1)) // CHUNK
    vis = jnp.concatenate([kc <= qc, kc <= qc], axis=0)
    outs = []
    for h in range(heads):
        sl = slice(h * LANES, (h + 1) * LANES)
        qs = _stack_maps(q_ref[:, sl])
        kp = kp_ref[pl.ds(h, past, stride=heads), :].astype(_BF16)
        vp = vp_ref[pl.ds(h, past, stride=heads), :].astype(_BF16)
        s_p = lax.dot_general(qs, kp, _NT, preferred_element_type=_F32)
        s_n = lax.dot_general(qs, kn_ref[:, sl], _NT, preferred_element_type=_F32)
        s_n = jnp.where(vis, s_n, -jnp.inf)
        m = jnp.maximum(jnp.max(s_p, axis=1, keepdims=True), jnp.max(s_n, axis=1, keepdims=True))
        p_p = jnp.exp2(s_p - m)
        p_n = jnp.exp2(s_n - m)
        l = jnp.sum(p_p, axis=1, keepdims=True) + jnp.sum(p_n, axis=1, keepdims=True)
        acc = (jnp.dot(p_p.astype(_BF16), vp, preferred_element_type=_F32)
               + jnp.dot(p_n.astype(_BF16), vn_ref[:, sl], preferred_element_type=_F32))
        outs.append(_diff_finish(acc, l, t, lam, g, lam_init))
    o_ref[...] = jnp.concatenate(outs, axis=1).astype(o_ref.dtype)


def _attn_sample(q16, k16, v16, past_k, past_v, layer, lam, g, n_seq, t, lam_init):
    width = q16.shape[1]
    heads = width // LANES
    past = past_k.shape[1] // heads
    s0 = layer * n_seq
    new_spec = pl.BlockSpec((t, width), lambda s: (s, 0))
    past_spec = pl.BlockSpec((None, past * heads, LANES), lambda s: (s0 + s, 0, 0))
    blk = 4 * _nbytes((t, width), _BF16) + 2 * _nbytes((past * heads, LANES), _F32)
    return pl.pallas_call(
        functools.partial(_attn_sample_kernel, t=t, past=past, heads=heads, lam_init=lam_init),
        grid=(n_seq,),
        in_specs=[new_spec, new_spec, new_spec, past_spec, past_spec,
                  pl.BlockSpec(lam.shape, lambda s: (0, 0)),
                  pl.BlockSpec((1, V_DIM), lambda s: (0, 0))],
        out_specs=new_spec,
        out_shape=jax.ShapeDtypeStruct((n_seq * t, width), _BF16),
        compiler_params=_params(("parallel",), blk, 6 * _nbytes((2 * t, past), _F32)),
        name="attn_sample",
    )(q16, k16, v16, past_k, past_v, lam, g)


def _res_ln(acc, res, g, b, alpha):
    y = alpha * res + acc
    mu = jnp.mean(y, axis=1, keepdims=True)
    yc = y - mu
    var = jnp.mean(yc * yc, axis=1, keepdims=True)
    return yc * lax.rsqrt(var + LN_EPS) * g + b


def _mm_res_ln_kernel(*refs, nk, alpha, part_starts):
    n_parts = len(part_starts)
    a_refs = refs[:n_parts]
    w_ref, res_ref, g_ref, b_ref, o32_ref, o16_ref = refs[n_parts:n_parts + 6]
    scratch = refs[n_parts + 6:]

    def finish(acc):
        y = _res_ln(acc, res_ref[...], g_ref[...], b_ref[...], alpha)
        o32_ref[...] = y
        o16_ref[...] = y.astype(_BF16)

    a = a_refs[0][...]
    for p in range(1, n_parts):
        a = jnp.where(pl.program_id(0) >= part_starts[p], a_refs[p][...], a)
    part = jnp.dot(a, w_ref[...], preferred_element_type=_F32)
    if nk == 1:
        finish(part)
        return
    acc_ref, = scratch
    kk = pl.program_id(1)

    @pl.when(kk == 0)
    def _():
        acc_ref[...] = part

    @pl.when(kk > 0)
    def _():
        acc_ref[...] += part

    @pl.when(kk == nk - 1)
    def _():
        finish(acc_ref[...])


def _mm_res_ln(parts, w16, res, g, b, alpha, name):
    m = sum(a.shape[0] for a in parts)
    k = parts[0].shape[1]
    d = w16.shape[1]
    resident = _nbytes((k, d), _BF16) <= V7X_VMEM_REQUEST_CAP // 2
    tk = k if resident else _tile(k, 512)
    nk = k // tk
    tm = _tile(functools.reduce(math.gcd, [a.shape[0] for a in parts]), 512 if k <= 2048 else 256)
    blk = (len(parts) * _nbytes((tm, tk), _BF16) + 2 * _nbytes((tm, d), _F32) + _nbytes((tm, d), _BF16))
    w_bytes = _nbytes((tk, d), _BF16) * (1 if resident else 2)
    w_mode = dict(pipeline_mode=pl.Buffered(1)) if resident else {}
    scratch = [pltpu.VMEM((tm, d), _F32)] if nk > 1 else []
    part_starts, part_specs, start = [], [], 0
    for a in parts:
        nblk = a.shape[0] // tm
        part_starts.append(start)
        part_specs.append(pl.BlockSpec(
            (tm, tk), lambda i, kk, start=start, nblk=nblk: (jnp.clip(i - start, 0, nblk - 1), kk)))
        start += nblk
    return pl.pallas_call(
        functools.partial(_mm_res_ln_kernel, nk=nk, alpha=alpha, part_starts=tuple(part_starts)),
        grid=(m // tm, nk),
        in_specs=part_specs + [
                  pl.BlockSpec((tk, d), lambda i, kk: (kk, 0), **w_mode),
                  pl.BlockSpec((tm, d), lambda i, kk: (i, 0)),
                  pl.BlockSpec((1, d), lambda i, kk: (0, 0)),
                  pl.BlockSpec((1, d), lambda i, kk: (0, 0))],
        out_specs=[pl.BlockSpec((tm, d), lambda i, kk: (i, 0)),
                   pl.BlockSpec((tm, d), lambda i, kk: (i, 0))],
        out_shape=[jax.ShapeDtypeStruct((m, d), _F32), jax.ShapeDtypeStruct((m, d), _BF16)],
        scratch_shapes=scratch,
        compiler_params=_params(("parallel", "arbitrary"), blk, w_bytes + 2 * _nbytes((tm, d), _F32)),
        name=name,
    )(*parts, w16, res, g.reshape(1, d), b.reshape(1, d))


def _swiglu(g, u):
    return g * jax.nn.sigmoid(g) * u


def _gate_up_kernel(x_ref, wg_ref, wu_ref, h_ref):
    x = x_ref[...]
    g = jnp.dot(x, wg_ref[...], preferred_element_type=_F32)
    u = jnp.dot(x, wu_ref[...], preferred_element_type=_F32)
    h_ref[...] = _swiglu(g, u).astype(h_ref.dtype)


def _gate_up(x16, w_gu16):
    m, k = x16.shape
    f = w_gu16.shape[1] // 2
    tm = _tile(m, 1024)
    tf = _tile(f, 512)
    nf = f // tf
    blk = _nbytes((tm, k), _BF16) + 2 * _nbytes((k, tf), _BF16) + _nbytes((tm, tf), _BF16)
    return pl.pallas_call(
        _gate_up_kernel,
        grid=(m // tm, nf),
        in_specs=[pl.BlockSpec((tm, k), lambda i, j: (i, 0)),
                  pl.BlockSpec((k, tf), lambda i, j: (0, j)),
                  pl.BlockSpec((k, tf), lambda i, j: (0, nf + j))],
        out_specs=pl.BlockSpec((tm, tf), lambda i, j: (i, j)),
        out_shape=jax.ShapeDtypeStruct((m, f), _BF16),
        compiler_params=_params(("parallel", "parallel"), blk, 3 * _nbytes((tm, tf), _F32)),
        name="ffn_gate_up",
    )(x16, w_gu16, w_gu16)


def _pool_kernel(halo_ref, u_ref, w_ref, scale_ref, z_ref, ext_ref, *, tm, pos0, zero_first):
    i = pl.program_id(1)
    halo = halo_ref[...]
    if zero_first:
        halo = jnp.where(i == 0, jnp.zeros_like(halo), halo)
    ext_ref[0:HALO, :] = halo
    ext_ref[HALO:HALO + tm, :] = u_ref[...]
    pos = pos0 + i * tm + lax.broadcasted_iota(jnp.int32, (tm, 1), 0)
    gd = u_ref.shape[1] // len(POOL_WINDOWS)
    for g, w in enumerate(POOL_WINDOWS):
        cols = slice(g * gd, (g + 1) * gd)
        win = ext_ref[HALO:HALO + tm, cols]
        for back in range(1, w):
            win = win + ext_ref[HALO - back:HALO - back + tm, cols]
        count = jnp.minimum(w, pos + 1).astype(_F32)
        d = win / count - u_ref[:, cols]
        zg = jnp.dot(d.astype(_BF16), w_ref[g], preferred_element_type=_F32)
        z_ref[:, cols] = (zg * scale_ref[:, cols]).astype(z_ref.dtype)


def _pool_call(halo_arr, halo_spec, u, u_spec, w_grp16, scale, grid, tm, pos0, zero_first,
               out_rows, out_spec, name):
    d = u.shape[1]
    blk = (_nbytes((HALO, d), _F32) + _nbytes((tm, d), _F32) + _nbytes(w_grp16.shape, _BF16)
           + _nbytes((tm, d), _BF16))
    return pl.pallas_call(
        functools.partial(_pool_kernel, tm=tm, pos0=pos0, zero_first=zero_first),
        grid=grid,
        in_specs=[halo_spec, u_spec,
                  pl.BlockSpec(w_grp16.shape, lambda b, i: (0, 0, 0)),
                  pl.BlockSpec((1, d), lambda b, i: (0, 0))],
        out_specs=out_spec,
        out_shape=jax.ShapeDtypeStruct((out_rows, d), _BF16),
        scratch_shapes=[pltpu.VMEM((HALO + tm, d), _F32)],
        compiler_params=_params(("parallel", "arbitrary"), blk, 3 * _nbytes((HALO + tm, d), _F32)),
        name=name,
    )(halo_arr, u, w_grp16, scale.reshape(1, d))


def _pool_prompt(u, w_grp16, scale, batch, seq):
    d = u.shape[1]
    tm = _tile(seq, 256)
    nt = seq // tm
    per = tm // HALO
    halo_spec = pl.BlockSpec((HALO, d), lambda b, i: (jnp.maximum((b * nt + i) * per - 1, 0), 0))
    u_spec = pl.BlockSpec((tm, d), lambda b, i: (b * nt + i, 0))
    out_spec = pl.BlockSpec((tm, d), lambda b, i: (b * nt + i, 0))
    return _pool_call(u, halo_spec, u, u_spec, w_grp16, scale, (batch, nt), tm, 0, True,
                      batch * seq, out_spec, "pool_prompt")


def _pool_sample(u, ctx, layer, w_grp16, scale, row0, n_seq, t, pos0):
    d = u.shape[1]
    r0 = row0 // t
    s0 = layer * n_seq
    halo_spec = pl.BlockSpec((None, HALO, d), lambda s, i: (s0 + s, 0, 0))
    u_spec = pl.BlockSpec((t, d), lambda s, i: (r0 + s, 0))
    out_spec = pl.BlockSpec((t, d), lambda s, i: (s, 0))
    return _pool_call(ctx, halo_spec, u, u_spec, w_grp16, scale, (n_seq, 1), t, pos0, False,
                      n_seq * t, out_spec, "pool_sample")


def _split_bf16(x):
    hi = x.astype(_BF16)
    lo = (x - hi.astype(_F32)).astype(_BF16)
    return hi, lo


def _router_kernel(x_ref, w_ref, meta_ref, count_ref, carry_ref, *, tm, n_experts):
    i = pl.program_id(0)

    @pl.when(i == 0)
    def _():
        carry_ref[...] = jnp.zeros_like(carry_ref)

    xh, xl = _split_bf16(x_ref[...])
    wh, wl = _split_bf16(w_ref[...])
    logits = (jnp.dot(xh, wh, preferred_element_type=_F32)
              + (jnp.dot(xh, wl, preferred_element_type=_F32)
                 + jnp.dot(xl, wh, preferred_element_type=_F32)))
    lane = lax.broadcasted_iota(jnp.int32, (tm, LANES), 1)
    lg = jnp.where(lane < n_experts, logits, -jnp.inf)
    m1 = jnp.max(lg, axis=1, keepdims=True)
    i1 = jnp.min(jnp.where(lg == m1, lane, LANES), axis=1, keepdims=True)
    lg2 = jnp.where(lane == i1, -jnp.inf, lg)
    m2 = jnp.max(lg2, axis=1, keepdims=True)
    i2 = jnp.min(jnp.where(lg2 == m2, lane, LANES), axis=1, keepdims=True)
    e = jnp.exp(m2 - m1)
    g1 = 1.0 / (1.0 + e)
    g2 = e / (1.0 + e)

    sel1 = lane == i1
    sel2 = lane == i2
    cnt = jnp.where(sel1, 1.0, 0.0) + jnp.where(sel2, 1.0, 0.0)
    row = lax.broadcasted_iota(jnp.int32, (tm, tm), 0)
    col = lax.broadcasted_iota(jnp.int32, (tm, tm), 1)
    lower = jnp.where(col < row, 1.0, 0.0).astype(_BF16)
    before = jnp.dot(lower, cnt.astype(_BF16), preferred_element_type=_F32) + carry_ref[0:1, :]
    r1 = jnp.sum(jnp.where(sel1, before, 0.0), axis=1, keepdims=True)
    r2 = jnp.sum(jnp.where(sel2, before, 0.0), axis=1, keepdims=True)
    carry_ref[0:1, :] = carry_ref[0:1, :] + jnp.sum(cnt, axis=0, keepdims=True)
    count_ref[...] = carry_ref[...]

    meta = jnp.zeros((tm, LANES), _F32)
    for k, v in enumerate((i1.astype(_F32), i2.astype(_F32), r1, r2, g1, g2)):
        meta = jnp.where(lane == k, v, meta)
    meta_ref[...] = meta


def _router(x32, w_router):
    n, d = x32.shape
    n_experts = w_router.shape[1]
    tm = _tile(n, 256)
    w_pad = jnp.pad(w_router, ((0, 0), (0, LANES - n_experts)))
    blk = _nbytes((tm, d), _F32) + _nbytes((d, LANES), _F32) + 2 * _nbytes((tm, LANES), _F32)
    return pl.pallas_call(
        functools.partial(_router_kernel, tm=tm, n_experts=n_experts),
        grid=(n // tm,),
        in_specs=[pl.BlockSpec((tm, d), lambda i: (i, 0)),
                  pl.BlockSpec((d, LANES), lambda i: (0, 0))],
        out_specs=[pl.BlockSpec((tm, LANES), lambda i: (i, 0)),
                   pl.BlockSpec((8, LANES), lambda i: (0, 0))],
        out_shape=[jax.ShapeDtypeStruct((n, LANES), _F32), jax.ShapeDtypeStruct((8, LANES), _F32)],
        scratch_shapes=[pltpu.VMEM((8, LANES), _F32)],
        compiler_params=_params(("arbitrary",), blk, 4 * _nbytes((tm, d), _F32)),
        name="moe_router",
    )(x32, w_pad)


def _row_copy(src_hbm, row, dst_vmem, r, sem):
    return pltpu.make_async_copy(src_hbm.at[pl.ds(row, 1), :], dst_vmem.at[pl.ds(r, 1), :], sem)


def _gather_rows(src_hbm, idx_ref, base, dst_vmem, sem, n):
    def issue(i, c):
        for u in range(2):
            r = 2 * i + u
            _row_copy(src_hbm, idx_ref[base + r], dst_vmem, r, sem).start(priority=u)
        return c
    lax.fori_loop(0, n // 2, issue, 0, unroll=2)


def _wait_rows(src_hbm, dst_vmem, sem, n):
    def drain(r, c):
        _row_copy(src_hbm, 0, dst_vmem, r, sem).wait()
        return c
    lax.fori_loop(0, n, drain, 0, unroll=8)


def _moe_gather_kernel(tok_ref, nu_ref, x_hbm, xs_hbm, in_ref, out_ref, in_sem, out_sem, *, tg):
    n_used = nu_ref[0]

    def out_copy(b, slot):
        return pltpu.make_async_copy(out_ref.at[slot], xs_hbm.at[pl.ds(pl.multiple_of(b * tg, tg), tg), :],
                                     out_sem.at[slot])

    _gather_rows(x_hbm, tok_ref, 0, in_ref.at[0], in_sem.at[0], tg)

    def body(b, c):
        slot = b % 2

        @pl.when(b + 1 < n_used)
        def _():
            _gather_rows(x_hbm, tok_ref, (b + 1) * tg, in_ref.at[1 - slot], in_sem.at[1 - slot], tg)

        _wait_rows(x_hbm, in_ref.at[slot], in_sem.at[slot], tg)

        @pl.when(b >= 2)
        def _():
            out_copy(b - 2, slot).wait()

        out_ref[slot] = in_ref[slot].astype(out_ref.dtype)
        out_copy(b, slot).start()
        return c

    lax.fori_loop(0, n_used, body, 0)

    @pl.when(n_used >= 2)
    def _():
        out_copy(n_used - 2, n_used % 2).wait()

    out_copy(n_used - 1, (n_used - 1) % 2).wait()

    n_blocks = xs_hbm.shape[0] // tg
    out_ref[0] = jnp.zeros(out_ref.shape[1:], out_ref.dtype)

    def start_zero(b, c):
        out_copy(b, 0).start()
        return c

    def wait_zero(b, c):
        out_copy(b, 0).wait()
        return c

    lax.fori_loop(n_used, n_blocks, start_zero, 0)
    lax.fori_loop(n_used, n_blocks, wait_zero, 0)


def _moe_gather(x32, row_tok, n_used):
    rows = row_tok.shape[0]
    d = x32.shape[1]
    tg = MOE_BLOCK
    return pl.pallas_call(
        functools.partial(_moe_gather_kernel, tg=tg),
        grid_spec=pltpu.PrefetchScalarGridSpec(
            num_scalar_prefetch=2,
            grid=(1,),
            in_specs=[pl.BlockSpec(memory_space=pl.ANY)],
            out_specs=pl.BlockSpec(memory_space=pl.ANY),
            scratch_shapes=[pltpu.VMEM((2, tg, d), _F32), pltpu.VMEM((2, tg, d), _BF16),
                            pltpu.SemaphoreType.DMA((2,)), pltpu.SemaphoreType.DMA((2,))]),
        out_shape=jax.ShapeDtypeStruct((rows, d), _BF16),
        compiler_params=_params(("arbitrary",), 0, 2 * _nbytes((tg, d), _F32) + 2 * _nbytes((tg, d), _BF16)),
        name="moe_gather",
    )(row_tok, n_used, x32)


def _grouped_rows(gs_ref, gb_ref, e, src_hbm, dst_hbm, in_ref, out_ref, in_sem, out_sem, col, compute, tm, big):
    assert big in (1, 2, 4)
    r0 = gs_ref[e]
    nblk = gb_ref[e]
    n_main = nblk // big
    tails = [big >> k for k in range(1, big.bit_length())]

    def rows(off, nb):
        return pl.ds(pl.multiple_of(r0 + off * tm, tm), nb * tm)

    def in_copy(off, nb, slot):
        return pltpu.make_async_copy(src_hbm.at[rows(off, nb), :], in_ref.at[slot, pl.ds(0, nb * tm), :],
                                     in_sem.at[slot])

    def out_copy(off, nb, slot):
        return pltpu.make_async_copy(out_ref.at[slot, pl.ds(0, nb * tm), :], dst_hbm.at[rows(off, nb), col],
                                     out_sem.at[slot])

    def tail_off(t):
        return (nblk // (2 * t)) * (2 * t)

    def start_first_tail(first, slot, enable):
        pending = enable
        for t in tails[first:]:
            present = (nblk & t) != 0

            @pl.when(jnp.logical_and(pending, present))
            def _():
                in_copy(tail_off(t), t, slot).start(priority=ROW_DMA_PRIORITY)

            pending = jnp.logical_and(pending, jnp.logical_not(present))

    @pl.when(n_main > 0)
    def _():
        in_copy(0, big, 0).start(priority=ROW_DMA_PRIORITY)

    start_first_tail(0, 0, n_main == 0)

    def body(b, c):
        slot = b % 2
        in_copy(b * big, big, slot).wait()

        @pl.when(b + 1 < n_main)
        def _():
            in_copy((b + 1) * big, big, 1 - slot).start(priority=ROW_DMA_PRIORITY)

        start_first_tail(0, 1 - slot, b + 1 == n_main)

        @pl.when(b >= 2)
        def _():
            out_copy((b - 2) * big, big, slot).wait()

        out_ref[slot] = compute(in_ref[slot])
        out_copy(b * big, big, slot).start()
        return c

    lax.fori_loop(0, n_main, body, 0)

    @pl.when(n_main >= 2)
    def _():
        out_copy((n_main - 2) * big, big, n_main % 2).wait()

    @pl.when(n_main >= 1)
    def _():
        out_copy((n_main - 1) * big, big, (n_main - 1) % 2).wait()

    slot = n_main % 2
    tail_slots = []
    for k, t in enumerate(tails):
        present = (nblk & t) != 0
        tail_slots.append(slot)
        cur = slot

        @pl.when(present)
        def _():
            in_copy(tail_off(t), t, cur).wait()
            start_first_tail(k + 1, 1 - cur, True)
            out_ref[cur, 0:t * tm, :] = compute(in_ref[cur, 0:t * tm, :])
            out_copy(tail_off(t), t, cur).start()

        slot = jnp.where(present, 1 - slot, slot)

    for k, t in enumerate(tails):
        @pl.when((nblk & t) != 0)
        def _():
            out_copy(tail_off(t), t, tail_slots[k]).wait()


def _zero_rows(gs_ref, gb_ref, e, dst_hbm, out_ref, out_sem, col, tm):
    r0 = gs_ref[e]
    out_ref[0, 0:tm, :] = jnp.zeros((tm, out_ref.shape[2]), out_ref.dtype)

    def copy(b):
        return pltpu.make_async_copy(out_ref.at[0, pl.ds(0, tm), :],
                                     dst_hbm.at[pl.ds(pl.multiple_of(r0 + b * tm, tm), tm), col], out_sem.at[0])

    def start(b, c):
        copy(b).start()
        return c

    def wait(b, c):
        copy(b).wait()
        return c

    lax.fori_loop(0, gb_ref[e], start, 0)
    lax.fori_loop(0, gb_ref[e], wait, 0)


def _moe_gate_up_kernel(gs_ref, gb_ref, xs_hbm, wg_ref, wu_ref, h_hbm, wg16_ref, wu16_ref,
                        in_ref, out_ref, in_sem, out_sem, *, tm, tf, n_experts):
    j = pl.program_id(0)
    e = pl.program_id(1)
    col = pl.ds(pl.multiple_of(j * tf, tf), tf)

    def compute(x):
        g = jnp.dot(x, wg16_ref[...], preferred_element_type=_F32)
        u = jnp.dot(x, wu16_ref[...], preferred_element_type=_F32)
        return _swiglu(g, u).astype(_BF16)

    @pl.when(e < n_experts)
    def _():
        wg16_ref[...] = wg_ref[...].astype(_BF16)
        wu16_ref[...] = wu_ref[...].astype(_BF16)
        _grouped_rows(gs_ref, gb_ref, e, xs_hbm, h_hbm, in_ref, out_ref, in_sem, out_sem, col, compute, tm,
                      in_ref.shape[1] // tm)

    @pl.when(e == n_experts)
    def _():
        _zero_rows(gs_ref, gb_ref, e, h_hbm, out_ref, out_sem, col, tm)


def _moe_gate_up(xs16, w_gu, gstart, gblocks):
    rows, d = xs16.shape
    n_experts = w_gu.shape[0]
    f = w_gu.shape[2] // 2
    tm = MOE_BLOCK
    tf = _tile(f, 1024)
    nf = f // tf
    blk = 2 * _nbytes((d, tf), _F32)
    tc = MOE_CHUNK_BLOCKS * tm
    scratch_bytes = (2 * _nbytes((d, tf), _BF16) + 2 * _nbytes((tc, d), _BF16) + 2 * _nbytes((tc, tf), _BF16)
                     + 3 * _nbytes((tc, tf), _F32))
    last = n_experts - 1
    return pl.pallas_call(
        functools.partial(_moe_gate_up_kernel, tm=tm, tf=tf, n_experts=n_experts),
        grid_spec=pltpu.PrefetchScalarGridSpec(
            num_scalar_prefetch=2,
            grid=(nf, n_experts + 1),
            in_specs=[pl.BlockSpec(memory_space=pl.ANY),
                      pl.BlockSpec((None, d, tf), lambda j, e, gs, gb: (jnp.minimum(e, last), 0, j)),
                      pl.BlockSpec((None, d, tf), lambda j, e, gs, gb: (jnp.minimum(e, last), 0, nf + j))],
            out_specs=pl.BlockSpec(memory_space=pl.ANY),
            scratch_shapes=[pltpu.VMEM((d, tf), _BF16), pltpu.VMEM((d, tf), _BF16),
                            pltpu.VMEM((2, tc, d), _BF16), pltpu.VMEM((2, tc, tf), _BF16),
                            pltpu.SemaphoreType.DMA((2,)), pltpu.SemaphoreType.DMA((2,))]),
        out_shape=jax.ShapeDtypeStruct((rows, f), _BF16),
        compiler_params=_params(("arbitrary", "arbitrary"), blk, scratch_bytes),
        name="moe_gate_up",
    )(gstart, gblocks, xs16, w_gu, w_gu)


def _moe_down_kernel(gs_ref, gb_ref, h_hbm, w_ref, y_hbm, w16_ref, in_ref, out_ref, in_sem, out_sem,
                     *, tm, tn, n_experts):
    j = pl.program_id(0)
    e = pl.program_id(1)
    col = pl.ds(pl.multiple_of(j * tn, tn), tn)

    def compute(h):
        return jnp.dot(h, w16_ref[...], preferred_element_type=_F32)

    @pl.when(e < n_experts)
    def _():
        w16_ref[...] = w_ref[...].astype(_BF16)
        _grouped_rows(gs_ref, gb_ref, e, h_hbm, y_hbm, in_ref, out_ref, in_sem, out_sem, col, compute, tm,
                      in_ref.shape[1] // tm)

    @pl.when(e == n_experts)
    def _():
        _zero_rows(gs_ref, gb_ref, e, y_hbm, out_ref, out_sem, col, tm)


def _moe_down(h16, w_down, gstart, gblocks):
    rows, f = h16.shape
    n_experts = w_down.shape[0]
    d = w_down.shape[2]
    tm = MOE_BLOCK
    tn = _tile(d, 256)
    tc = MOE_CHUNK_BLOCKS * tm
    blk = _nbytes((f, tn), _F32)
    scratch_bytes = (_nbytes((f, tn), _BF16) + 2 * _nbytes((tc, f), _BF16) + 3 * _nbytes((tc, tn), _F32))
    last = n_experts - 1
    return pl.pallas_call(
        functools.partial(_moe_down_kernel, tm=tm, tn=tn, n_experts=n_experts),
        grid_spec=pltpu.PrefetchScalarGridSpec(
            num_scalar_prefetch=2,
            grid=(d // tn, n_experts + 1),
            in_specs=[pl.BlockSpec(memory_space=pl.ANY),
                      pl.BlockSpec((None, f, tn), lambda j, e, gs, gb: (jnp.minimum(e, last), 0, j))],
            out_specs=pl.BlockSpec(memory_space=pl.ANY),
            scratch_shapes=[pltpu.VMEM((f, tn), _BF16),
                            pltpu.VMEM((2, tc, f), _BF16), pltpu.VMEM((2, tc, tn), _F32),
                            pltpu.SemaphoreType.DMA((2,)), pltpu.SemaphoreType.DMA((2,))]),
        out_shape=jax.ShapeDtypeStruct((rows, d), _F32),
        compiler_params=_params(("arbitrary", "arbitrary"), blk, scratch_bytes),
        name="moe_down",
    )(gstart, gblocks, h16, w_down)


def _moe_combine_kernel(d1_ref, d2_ref, y_hbm, meta_ref, res_ref, g_ref, b_ref, *rest, tc, alpha, n_split):
    outs, (buf_ref, sem) = rest[:-2], rest[-2:]
    i = pl.program_id(0)
    slot = i % 2

    def fetch(step, s):
        _gather_rows(y_hbm, d1_ref, step * tc, buf_ref.at[s, 0], sem.at[s], tc)
        _gather_rows(y_hbm, d2_ref, step * tc, buf_ref.at[s, 1], sem.at[s], tc)

    @pl.when(i == 0)
    def _():
        fetch(0, 0)

    @pl.when(i + 1 < pl.num_programs(0))
    def _():
        fetch(i + 1, 1 - slot)

    _wait_rows(y_hbm, buf_ref.at[slot, 0], sem.at[slot], tc)
    _wait_rows(y_hbm, buf_ref.at[slot, 1], sem.at[slot], tc)
    meta = meta_ref[...]
    f = buf_ref[slot, 0] * meta[:, 4:5] + buf_ref[slot, 1] * meta[:, 5:6]
    y = _res_ln(f, res_ref[...], g_ref[...], b_ref[...], alpha)
    if n_split is None:
        outs[0][...] = y
        outs[1][...] = y.astype(_BF16)
    else:
        @pl.when(i < n_split)
        def _():
            outs[0][...] = y

        @pl.when(i >= n_split)
        def _():
            outs[1][...] = y


def _moe_combine_ln(y_rows, dest1, dest2, meta, res, g, b, alpha, split_rows):
    n, d = res.shape
    tc = _tile(n if split_rows is None else math.gcd(n, split_rows), 128)
    blk = _nbytes((tc, LANES), _F32) + 2 * _nbytes((tc, d), _F32) + _nbytes((tc, d), _BF16)
    tile_spec = pl.BlockSpec((tc, d), lambda i, a, c: (i, 0))
    if split_rows is None:
        n_split = None
        out_specs = [tile_spec, tile_spec]
        out_shape = [jax.ShapeDtypeStruct((n, d), _F32), jax.ShapeDtypeStruct((n, d), _BF16)]
    else:
        n_split = split_rows // tc
        out_specs = [pl.BlockSpec((tc, d), lambda i, a, c: (jnp.minimum(i, n_split - 1), 0)),
                     pl.BlockSpec((tc, d), lambda i, a, c: (jnp.maximum(i - n_split, 0), 0))]
        out_shape = [jax.ShapeDtypeStruct((split_rows, d), _F32),
                     jax.ShapeDtypeStruct((n - split_rows, d), _F32)]
    return pl.pallas_call(
        functools.partial(_moe_combine_kernel, tc=tc, alpha=alpha, n_split=n_split),
        grid_spec=pltpu.PrefetchScalarGridSpec(
            num_scalar_prefetch=2,
            grid=(n // tc,),
            in_specs=[pl.BlockSpec(memory_space=pl.ANY),
                      pl.BlockSpec((tc, LANES), lambda i, a, c: (i, 0)),
                      tile_spec,
                      pl.BlockSpec((1, d), lambda i, a, c: (0, 0)),
                      pl.BlockSpec((1, d), lambda i, a, c: (0, 0))],
            out_specs=out_specs,
            scratch_shapes=[pltpu.VMEM((2, 2, tc, d), _F32), pltpu.SemaphoreType.DMA((2,))]),
        out_shape=out_shape,
        compiler_params=_params(("arbitrary",), blk, 8 * _nbytes((tc, d), _F32)),
        name="moe_combine_ln",
    )(dest1, dest2, y_rows, meta, res, g.reshape(1, d), b.reshape(1, d))


def _moe_layer(x32, w_router, w_gu, w_down, g, b, alpha, split_rows):
    n, d = x32.shape
    n_experts = w_router.shape[1]
    meta, counts = _router(x32, w_router)
    e1 = meta[:, 0].astype(jnp.int32)
    e2 = meta[:, 1].astype(jnp.int32)
    counts = counts[0, :n_experts].astype(jnp.int32)
    gblocks = (counts + MOE_BLOCK - 1) // MOE_BLOCK
    gend = jnp.cumsum(gblocks) * MOE_BLOCK
    gstart = gend - gblocks * MOE_BLOCK
    dest1 = gstart[e1] + meta[:, 2].astype(jnp.int32)
    dest2 = gstart[e2] + meta[:, 3].astype(jnp.int32)
    nb = (n * TOP_K) // MOE_BLOCK + n_experts
    rows = nb * MOE_BLOCK
    tok = jnp.arange(n, dtype=jnp.int32)
    row_tok = jnp.zeros((rows,), jnp.int32).at[dest1].set(tok).at[dest2].set(tok)
    n_used = (gend[-1] // MOE_BLOCK).astype(jnp.int32).reshape(1)
    gstart = jnp.concatenate([gstart, gend[-1:]]).astype(jnp.int32)
    gblocks = jnp.concatenate([gblocks, nb - n_used]).astype(jnp.int32)

    xs16 = _moe_gather(x32, row_tok, n_used)
    h16 = _moe_gate_up(xs16, w_gu, gstart, gblocks)
    y_rows = _moe_down(h16, w_down, gstart, gblocks)
    return _moe_combine_ln(y_rows, dest1, dest2, meta, x32, g, b, alpha, split_rows)


def kernel(x_prompt, x_sample, cache_k, cache_v, state_pool, attn_w_in, attn_w_o, attn_lambda, attn_subln_g, pool_w_in, pool_w_grp, pool_scale, pool_w_o, ln_mix_g, ln_mix_b, ln_ffn_g, ln_ffn_b, ffn_w_gu, ffn_w_down, moe_w_router, moe_w_gu, moe_w_down):
    batch, seq, d = x_prompt.shape
    n_seq, t, _ = x_sample.shape
    past = cache_k.shape[2]
    depth = ln_mix_g.shape[0]
    alpha = (2 * depth) ** 0.25
    n_p = batch * seq
    n_s = n_seq * t
    n = n_p + n_s
    n_heads = cache_k.shape[3]
    qk_width = n_heads * 2 * HEAD_DIM
    v_width = n_heads * V_DIM

    x32 = jnp.concatenate([x_prompt.reshape(n_p, d), x_sample.reshape(n_s, d)], axis=0)
    x16 = x32.astype(_BF16)
    tab_p = _rope_tables(np.arange(seq))
    tab_s = _rope_tables(np.tile(past + np.arange(t), n_seq))
    past_k = cache_k.reshape(-1, past * n_heads, 2 * HEAD_DIM)
    past_v = cache_v.reshape(-1, past * n_heads, V_DIM)
    ctx = jnp.pad(state_pool, ((0, 0), (0, 0), (HALO - POOL_CTX, 0), (0, 0))).reshape(-1, HALO, d)

    kp_l, vp_l, ks_l, vs_l, pp_l, ps_l = [], [], [], [], [], []
    y_p = y_s = None
    for i in range(depth):
        j = i // 2
        last = i == depth - 1
        if i % 2 == 0:
            lam_init = 0.8 - 0.6 * math.exp(-0.3 * i)
            w_in16 = attn_w_in[j].astype(_BF16)
            g = attn_subln_g[j].reshape(1, V_DIM)
            rows_p, rows_s = (0, n_p, tab_p), (n_p, n_s, tab_s)
            (q_p,), (q_s,) = (_proj(x16, w_in16, 0, qk_width, r0, nr, tab, False, True, "proj_q", Q_SCALE)
                              for r0, nr, tab in (rows_p, rows_s))
            (k32_p, k_p), (k32_s, k_s) = (_proj(x16, w_in16, qk_width, qk_width, r0, nr, tab, True, True, "proj_k")
                                          for r0, nr, tab in (rows_p, rows_s))
            (v32_p, v_p), (v32_s, v_s) = (_proj(x16, w_in16, 2 * qk_width, v_width, r0, nr, None, True, True, "proj_v")
                                          for r0, nr, tab in (rows_p, rows_s))
            mix = [_attn_prompt(q_p, k_p, v_p, attn_lambda[j], g, batch, seq, lam_init),
                   _attn_sample(q_s, k_s, v_s, past_k, past_v, j, attn_lambda[j], g, n_seq, t, lam_init)]
            w_o16 = attn_w_o[j].astype(_BF16)
            kp_l.append(k32_p.reshape(batch, seq, n_heads, 2 * HEAD_DIM))
            vp_l.append(v32_p.reshape(batch, seq, n_heads, V_DIM))
            ks_l.append(k32_s.reshape(n_seq, t, n_heads, 2 * HEAD_DIM))
            vs_l.append(v32_s.reshape(n_seq, t, n_heads, V_DIM))
        else:
            u32, = _proj(x16, pool_w_in[j].astype(_BF16), 0, d, 0, n, None, True, False, "pool_in")
            w_grp16 = pool_w_grp[j].astype(_BF16)
            mix = [_pool_prompt(u32, w_grp16, pool_scale[j], batch, seq),
                   _pool_sample(u32, ctx, j, w_grp16, pool_scale[j], n_p, n_seq, t, past)]
            w_o16 = pool_w_o[j].astype(_BF16)
            u_p = u32[:n_p].reshape(batch, seq, d)
            u_s = u32[n_p:].reshape(n_seq, t, d)
            pp_l.append(jnp.concatenate([jnp.zeros((batch, POOL_CTX, d), _F32), u_p[:, -POOL_CTX:]], 1)[:, -POOL_CTX:])
            ps_l.append(jnp.concatenate([state_pool[j], u_s[:, -POOL_CTX:]], 1)[:, -POOL_CTX:])
        x32, x16 = _mm_res_ln(mix, w_o16, x32, ln_mix_g[i], ln_mix_b[i], alpha, "mix_out_ln")
        if i % 2 == 0:
            h16 = _gate_up(x16, ffn_w_gu[j].astype(_BF16))
            x32, x16 = _mm_res_ln([h16], ffn_w_down[j].astype(_BF16), x32, ln_ffn_g[i], ln_ffn_b[i],
                                  alpha, "ffn_down_ln")
        elif last:
            y_p, y_s = _moe_layer(x32, moe_w_router[j], moe_w_gu[j], moe_w_down[j],
                                  ln_ffn_g[i], ln_ffn_b[i], alpha, n_p)
        else:
            x32, x16 = _moe_layer(x32, moe_w_router[j], moe_w_gu[j], moe_w_down[j],
                                  ln_ffn_g[i], ln_ffn_b[i], alpha, None)
    if y_p is None:
        y_p, y_s = x32[:n_p], x32[n_p:]
    return (y_p.reshape(batch, seq, d), y_s.reshape(n_seq, t, d), jnp.stack(kp_l), jnp.stack(vp_l),
            jnp.stack(pp_l), jnp.stack(ks_l), jnp.stack(vs_l), jnp.stack(ps_l))
```

```python
import functools
import math

import numpy as np
import jax
import jax.numpy as jnp
from jax import lax
from jax.experimental import pallas as pl
from jax.experimental.pallas import tpu as pltpu

CHUNK = 64
HEAD_DIM = 64
V_DIM = 2 * HEAD_DIM
ROT_DIM = HEAD_DIM // 4
ROPE_THETA = 500000.0
POOL_WINDOWS = (2, 4, 8, 16)
POOL_CTX = max(POOL_WINDOWS) - 1
TOP_K = 2
LN_EPS = 1e-5

LANES = 128
V7X_VMEM_REQUEST_CAP = 56 * 1024 * 1024
COMPILER_SCRATCH_ALLOWANCE = 6 * 1024 * 1024

HALO = 16
MOE_BLOCK = 256
MOE_CHUNK_BLOCKS = 2
ROW_DMA_PRIORITY = 1
Q_SCALE = HEAD_DIM ** -0.5 * math.log2(math.e)

_F32 = jnp.float32
_BF16 = jnp.bfloat16
_NT = (((1,), (1,)), ((), ()))


def _nbytes(shape, dtype):
    return math.prod(shape) * jnp.dtype(dtype).itemsize


def _params(semantics, pipelined_bytes, scratch_bytes=0):
    need = 2 * pipelined_bytes + scratch_bytes + COMPILER_SCRATCH_ALLOWANCE
    return pltpu.CompilerParams(dimension_semantics=semantics,
                                vmem_limit_bytes=min(need, V7X_VMEM_REQUEST_CAP))


def _tile(n, pref):
    t = min(n, pref)
    while n % t:
        t //= 2
    return t


def _rope(acc, cos, sin_lo, sin_hi):
    half = ROT_DIM // 2
    pieces = []
    for g in range(acc.shape[1] // LANES):
        xg = acc[:, g * LANES:(g + 1) * LANES]
        pieces.append(xg * cos
                      + pltpu.roll(xg, half, 1) * sin_hi
                      + pltpu.roll(xg, LANES - half, 1) * sin_lo)
    return pieces[0] if len(pieces) == 1 else jnp.concatenate(pieces, axis=1)


def _proj_kernel(*refs, rope, want32, want16, scale):
    x_ref, w_ref = refs[0], refs[1]
    acc = jnp.dot(x_ref[...], w_ref[...], preferred_element_type=_F32)
    pos = 2
    if rope:
        acc = _rope(acc, refs[2][...], refs[3][...], refs[4][...])
        pos = 5
    if scale is not None:
        acc = acc * scale
    if want32:
        refs[pos][...] = acc
        pos += 1
    if want16:
        refs[pos][...] = acc.astype(_BF16)


def _proj(xb, w, col0, ncols, row0, nrows, tables, want32, want16, name, scale=None):
    k = xb.shape[1]
    period = nrows if tables is None else tables[0].shape[0]
    tm = _tile(math.gcd(math.gcd(row0, nrows), period) if row0 else math.gcd(nrows, period), 1024)
    tn = _tile(ncols, 512)
    off = col0 // tn
    rb0 = row0 // tm
    tper = period // tm
    in_specs = [pl.BlockSpec((tm, k), lambda i, j: (rb0 + i, 0)),
                pl.BlockSpec((k, tn), lambda i, j: (0, j + off))]
    args = [xb, w]
    blk = _nbytes((tm, k), _BF16) + _nbytes((k, tn), _BF16)
    if tables is not None:
        for t in tables:
            in_specs.append(pl.BlockSpec((tm, LANES), lambda i, j: (i % tper, 0)))
            args.append(t)
            blk += _nbytes((tm, LANES), _F32)
    out_shape, out_specs = [], []
    for want, dt in ((want32, _F32), (want16, _BF16)):
        if want:
            out_shape.append(jax.ShapeDtypeStruct((nrows, ncols), dt))
            out_specs.append(pl.BlockSpec((tm, tn), lambda i, j: (i, j)))
            blk += _nbytes((tm, tn), dt)
    return pl.pallas_call(
        functools.partial(_proj_kernel, rope=tables is not None, want32=want32, want16=want16,
                          scale=scale),
        grid=(nrows // tm, ncols // tn),
        in_specs=in_specs, out_specs=out_specs, out_shape=out_shape,
        compiler_params=_params(("parallel", "parallel"), blk, _nbytes((tm, tn), _F32)),
        name=name,
    )(*args)


def _rope_tables(pos):
    half = ROT_DIM // 2
    inv_freq = ROPE_THETA ** (-np.arange(0, ROT_DIM, 2, dtype=np.float64) / ROT_DIM)
    ang = np.asarray(pos, np.float64)[:, None] * inv_freq[None, :]
    cos, sin = np.cos(ang), np.sin(ang)
    t = ang.shape[0]
    ones = np.ones((t, HEAD_DIM - ROT_DIM))
    zeros = np.zeros((t, HEAD_DIM - ROT_DIM))
    zh = np.zeros((t, half))
    cos_map = np.concatenate([cos, cos, ones], 1)
    lo_map = np.concatenate([-sin, zh, zeros], 1)
    hi_map = np.concatenate([zh, sin, zeros], 1)
    return tuple(jnp.asarray(np.concatenate([a, a], 1), _F32) for a in (cos_map, lo_map, hi_map))


def _stack_maps(q):
    lane = lax.broadcasted_iota(jnp.int32, q.shape, 1)
    zero = jnp.zeros_like(q)
    return jnp.concatenate([jnp.where(lane < HEAD_DIM, q, zero),
                            jnp.where(lane >= HEAD_DIM, q, zero)], axis=0)


def _diff_lambda(lam_ref, lam_init):
    lf = lam_ref[...]
    a = jnp.sum(lf[0:1] * lf[1:2], axis=1, keepdims=True)
    b = jnp.sum(lf[2:3] * lf[3:4], axis=1, keepdims=True)
    return jnp.exp(a) - jnp.exp(b) + lam_init


def _diff_finish(acc, l, tq, lam, g, lam_init):
    o = acc / l
    o = o[:tq] - lam * o[tq:]
    ms = jnp.mean(o * o, axis=1, keepdims=True)
    return (o * lax.rsqrt(ms + LN_EPS) * g) * (1.0 - lam_init)


def _attn_prompt_kernel(q_ref, k_ref, v_ref, lam_ref, g_ref, o_ref, sa_ref, sb_ref, *, tq, tk, lam_init):
    q0 = pl.program_id(2) * tq
    qs = _stack_maps(q_ref[...])
    n_full = q0 // tk

    def scores(kb, dst_ref):
        k0 = pl.multiple_of(kb * tk, tk)
        dst_ref[...] = lax.dot_general(qs, k_ref[pl.ds(k0, tk), :], _NT, preferred_element_type=_F32)

    def consume(kb, src_ref, carry, masked):
        m, l, acc = carry
        k0 = pl.multiple_of(kb * tk, tk)
        s = src_ref[...]
        if masked:
            qc = (q0 + lax.broadcasted_iota(jnp.int32, (tq, tk), 0)) // CHUNK
            kc = (k0 + lax.broadcasted_iota(jnp.int32, (tq, tk), 1)) // CHUNK
            vis = kc <= qc
            s = jnp.where(jnp.concatenate([vis, vis], axis=0), s, -jnp.inf)
        m_new = jnp.maximum(m, jnp.max(s, axis=1, keepdims=True))
        p = jnp.exp2(s - m_new)
        alpha = jnp.exp2(m - m_new)
        l = alpha * l + jnp.sum(p, axis=1, keepdims=True)
        acc = alpha * acc + jnp.dot(p.astype(_BF16), v_ref[pl.ds(k0, tk), :], preferred_element_type=_F32)
        return m_new, l, acc

    def pair(p, carry):
        kb = 2 * p
        scores(kb + 1, sb_ref)
        carry = consume(kb, sa_ref, carry, False)
        scores(kb + 2, sa_ref)
        return consume(kb + 1, sb_ref, carry, False)

    def odd_tail(carry):
        scores(n_full, sb_ref)
        carry = consume(n_full - 1, sa_ref, carry, False)
        return consume(n_full, sb_ref, carry, True)

    def even_tail(carry):
        return consume(n_full, sa_ref, carry, True)

    carry = (jnp.full((2 * tq, 1), -jnp.inf, _F32), jnp.zeros((2 * tq, 1), _F32),
             jnp.zeros((2 * tq, V_DIM), _F32))
    scores(0, sa_ref)
    carry = lax.fori_loop(0, n_full // 2, pair, carry)
    _, l, acc = lax.cond(n_full % 2 == 1, odd_tail, even_tail, carry)
    lam = _diff_lambda(lam_ref, lam_init)
    o_ref[...] = _diff_finish(acc, l, tq, lam, g_ref[...], lam_init).astype(o_ref.dtype)


def _attn_prompt(q16, k16, v16, lam, g, batch, seq, lam_init):
    n_heads = q16.shape[1] // LANES
    tq = _tile(seq, 256)
    tk = _tile(seq, 512)
    assert tk % tq == 0
    nq = seq // tq
    blk = (_nbytes((tq, LANES), _BF16) * 2 + 2 * _nbytes((seq, LANES), _BF16))
    return pl.pallas_call(
        functools.partial(_attn_prompt_kernel, tq=tq, tk=tk, lam_init=lam_init),
        grid=(batch, n_heads, nq),
        in_specs=[pl.BlockSpec((tq, LANES), lambda b, h, i: (b * nq + i, h)),
                  pl.BlockSpec((seq, LANES), lambda b, h, i: (b, h)),
                  pl.BlockSpec((seq, LANES), lambda b, h, i: (b, h)),
                  pl.BlockSpec(lam.shape, lambda b, h, i: (0, 0)),
                  pl.BlockSpec((1, V_DIM), lambda b, h, i: (0, 0))],
        out_specs=pl.BlockSpec((tq, LANES), lambda b, h, i: (b * nq + i, h)),
        out_shape=jax.ShapeDtypeStruct((batch * seq, n_heads * V_DIM), _BF16),
        scratch_shapes=[pltpu.VMEM((2 * tq, tk), _F32), pltpu.VMEM((2 * tq, tk), _F32)],
        compiler_params=_params(("parallel", "parallel", "parallel"), blk,
                                6 * _nbytes((2 * tq, tk), _F32)),
        name="attn_prompt",
    )(q16, k16, v16, lam, g)


def _attn_sample_kernel(q_ref, kn_ref, vn_ref, kp_ref, vp_ref, lam_ref, g_ref, o_ref, *,
                        t, past, heads, lam_init):
    lam = _diff_lambda(lam_ref, lam_init)
    g = g_ref[...]
    qc = (past + lax.broadcasted_iota(jnp.int32, (t, t), 0)) // CHUNK
    kc = (past + lax.broadcasted_iota(jnp.int32, (t, t), 1)) // CHUNK
    vis = jnp.concatenate([kc <= qc, kc <= qc], axis=0)
    outs = []
    for h in range(heads):
        sl = slice(h * LANES, (h + 1) * LANES)
        qs = _stack_maps(q_ref[:, sl])
        kp = kp_ref[pl.ds(h, past, stride=heads), :].astype(_BF16)
        vp = vp_ref[pl.ds(h, past, stride=heads), :].astype(_BF16)
        s_p = lax.dot_general(qs, kp, _NT, preferred_element_type=_F32)
        s_n = lax.dot_general(qs, kn_ref[:, sl], _NT, preferred_element_type=_F32)
        s_n = jnp.where(vis, s_n, -jnp.inf)
        m = jnp.maximum(jnp.max(s_p, axis=1, keepdims=True), jnp.max(s_n, axis=1, keepdims=True))
        p_p = jnp.exp2(s_p - m)
        p_n = jnp.exp2(s_n - m)
        l = jnp.sum(p_p, axis=1, keepdims=True) + jnp.sum(p_n, axis=1, keepdims=True)
        acc = (jnp.dot(p_p.astype(_BF16), vp, preferred_element_type=_F32)
               + jnp.dot(p_n.astype(_BF16), vn_ref[:, sl], preferred_element_type=_F32))
        outs.append(_diff_finish(acc, l, t, lam, g, lam_init))
    o_ref[...] = jnp.concatenate(outs, axis=1).astype(o_ref.dtype)


def _attn_sample(q16, k16, v16, past_k, past_v, layer, lam, g, n_seq, t, lam_init):
    width = q16.shape[1]
    heads = width // LANES
    past = past_k.shape[1] // heads
    s0 = layer * n_seq
    new_spec = pl.BlockSpec((t, width), lambda s: (s, 0))
    past_spec = pl.BlockSpec((None, past * heads, LANES), lambda s: (s0 + s, 0, 0))
    blk = 4 * _nbytes((t, width), _BF16) + 2 * _nbytes((past * heads, LANES), _F32)
    return pl.pallas_call(
        functools.partial(_attn_sample_kernel, t=t, past=past, heads=heads, lam_init=lam_init),
        grid=(n_seq,),
        in_specs=[new_spec, new_spec, new_spec, past_spec, past_spec,
                  pl.BlockSpec(lam.shape, lambda s: (0, 0)),
                  pl.BlockSpec((1, V_DIM), lambda s: (0, 0))],
        out_specs=new_spec,
        out_shape=jax.ShapeDtypeStruct((n_seq * t, width), _BF16),
        compiler_params=_params(("parallel",), blk, 6 * _nbytes((2 * t, past), _F32)),
        name="attn_sample",
    )(q16, k16, v16, past_k, past_v, lam, g)


def _res_ln(acc, res, g, b, alpha):
    y = alpha * res + acc
    mu = jnp.mean(y, axis=1, keepdims=True)
    yc = y - mu
    var = jnp.mean(yc * yc, axis=1, keepdims=True)
    return yc * lax.rsqrt(var + LN_EPS) * g + b


def _mm_res_ln_kernel(*refs, nk, alpha, part_starts):
    n_parts = len(part_starts)
    a_refs = refs[:n_parts]
    w_ref, res_ref, g_ref, b_ref, o32_ref, o16_ref = refs[n_parts:n_parts + 6]
    scratch = refs[n_parts + 6:]

    def finish(acc):
        y = _res_ln(acc, res_ref[...], g_ref[...], b_ref[...], alpha)
        o32_ref[...] = y
        o16_ref[...] = y.astype(_BF16)

    a = a_refs[0][...]
    for p in range(1, n_parts):
        a = jnp.where(pl.program_id(0) >= part_starts[p], a_refs[p][...], a)
    part = jnp.dot(a, w_ref[...], preferred_element_type=_F32)
    if nk == 1:
        finish(part)
        return
    acc_ref, = scratch
    kk = pl.program_id(1)

    @pl.when(kk == 0)
    def _():
        acc_ref[...] = part

    @pl.when(kk > 0)
    def _():
        acc_ref[...] += part

    @pl.when(kk == nk - 1)
    def _():
        finish(acc_ref[...])


def _mm_res_ln(parts, w16, res, g, b, alpha, name):
    m = sum(a.shape[0] for a in parts)
    k = parts[0].shape[1]
    d = w16.shape[1]
    resident = _nbytes((k, d), _BF16) <= V7X_VMEM_REQUEST_CAP // 2
    tk = k if resident else _tile(k, 512)
    nk = k // tk
    tm = _tile(functools.reduce(math.gcd, [a.shape[0] for a in parts]), 512 if k <= 2048 else 256)
    blk = (len(parts) * _nbytes((tm, tk), _BF16) + 2 * _nbytes((tm, d), _F32) + _nbytes((tm, d), _BF16))
    w_bytes = _nbytes((tk, d), _BF16) * (1 if resident else 2)
    w_mode = dict(pipeline_mode=pl.Buffered(1)) if resident else {}
    scratch = [pltpu.VMEM((tm, d), _F32)] if nk > 1 else []
    part_starts, part_specs, start = [], [], 0
    for a in parts:
        nblk = a.shape[0] // tm
        part_starts.append(start)
        part_specs.append(pl.BlockSpec(
            (tm, tk), lambda i, kk, start=start, nblk=nblk: (jnp.clip(i - start, 0, nblk - 1), kk)))
        start += nblk
    return pl.pallas_call(
        functools.partial(_mm_res_ln_kernel, nk=nk, alpha=alpha, part_starts=tuple(part_starts)),
        grid=(m // tm, nk),
        in_specs=part_specs + [
                  pl.BlockSpec((tk, d), lambda i, kk: (kk, 0), **w_mode),
                  pl.BlockSpec((tm, d), lambda i, kk: (i, 0)),
                  pl.BlockSpec((1, d), lambda i, kk: (0, 0)),
                  pl.BlockSpec((1, d), lambda i, kk: (0, 0))],
        out_specs=[pl.BlockSpec((tm, d), lambda i, kk: (i, 0)),
                   pl.BlockSpec((tm, d), lambda i, kk: (i, 0))],
        out_shape=[jax.ShapeDtypeStruct((m, d), _F32), jax.ShapeDtypeStruct((m, d), _BF16)],
        scratch_shapes=scratch,
        compiler_params=_params(("parallel", "arbitrary"), blk, w_bytes + 2 * _nbytes((tm, d), _F32)),
        name=name,
    )(*parts, w16, res, g.reshape(1, d), b.reshape(1, d))


def _swiglu(g, u):
    return g * jax.nn.sigmoid(g) * u


def _gate_up_kernel(x_ref, wg_ref, wu_ref, h_ref):
    x = x_ref[...]
    g = jnp.dot(x, wg_ref[...], preferred_element_type=_F32)
    u = jnp.dot(x, wu_ref[...], preferred_element_type=_F32)
    h_ref[...] = _swiglu(g, u).astype(h_ref.dtype)


def _gate_up(x16, w_gu16):
    m, k = x16.shape
    f = w_gu16.shape[1] // 2
    tm = _tile(m, 1024)
    tf = _tile(f, 512)
    nf = f // tf
    blk = _nbytes((tm, k), _BF16) + 2 * _nbytes((k, tf), _BF16) + _nbytes((tm, tf), _BF16)
    return pl.pallas_call(
        _gate_up_kernel,
        grid=(m // tm, nf),
        in_specs=[pl.BlockSpec((tm, k), lambda i, j: (i, 0)),
                  pl.BlockSpec((k, tf), lambda i, j: (0, j)),
                  pl.BlockSpec((k, tf), lambda i, j: (0, nf + j))],
        out_specs=pl.BlockSpec((tm, tf), lambda i, j: (i, j)),
        out_shape=jax.ShapeDtypeStruct((m, f), _BF16),
        compiler_params=_params(("parallel", "parallel"), blk, 3 * _nbytes((tm, tf), _F32)),
        name="ffn_gate_up",
    )(x16, w_gu16, w_gu16)


def _pool_kernel(halo_ref, u_ref, w_ref, scale_ref, z_ref, ext_ref, *, tm, pos0, zero_first):
    i = pl.program_id(1)
    halo = halo_ref[...]
    if zero_first:
        halo = jnp.where(i == 0, jnp.zeros_like(halo), halo)
    ext_ref[0:HALO, :] = halo
    ext_ref[HALO:HALO + tm, :] = u_ref[...]
    pos = pos0 + i * tm + lax.broadcasted_iota(jnp.int32, (tm, 1), 0)
    gd = u_ref.shape[1] // len(POOL_WINDOWS)
    for g, w in enumerate(POOL_WINDOWS):
        cols = slice(g * gd, (g + 1) * gd)
        win = ext_ref[HALO:HALO + tm, cols]
        for back in range(1, w):
            win = win + ext_ref[HALO - back:HALO - back + tm, cols]
        count = jnp.minimum(w, pos + 1).astype(_F32)
        d = win / count - u_ref[:, cols]
        zg = jnp.dot(d.astype(_BF16), w_ref[g], preferred_element_type=_F32)
        z_ref[:, cols] = (zg * scale_ref[:, cols]).astype(z_ref.dtype)


def _pool_call(halo_arr, halo_spec, u, u_spec, w_grp16, scale, grid, tm, pos0, zero_first,
               out_rows, out_spec, name):
    d = u.shape[1]
    blk = (_nbytes((HALO, d), _F32) + _nbytes((tm, d), _F32) + _nbytes(w_grp16.shape, _BF16)
           + _nbytes((tm, d), _BF16))
    return pl.pallas_call(
        functools.partial(_pool_kernel, tm=tm, pos0=pos0, zero_first=zero_first),
        grid=grid,
        in_specs=[halo_spec, u_spec,
                  pl.BlockSpec(w_grp16.shape, lambda b, i: (0, 0, 0)),
                  pl.BlockSpec((1, d), lambda b, i: (0, 0))],
        out_specs=out_spec,
        out_shape=jax.ShapeDtypeStruct((out_rows, d), _BF16),
        scratch_shapes=[pltpu.VMEM((HALO + tm, d), _F32)],
        compiler_params=_params(("parallel", "arbitrary"), blk, 3 * _nbytes((HALO + tm, d), _F32)),
        name=name,
    )(halo_arr, u, w_grp16, scale.reshape(1, d))


def _pool_prompt(u, w_grp16, scale, batch, seq):
    d = u.shape[1]
    tm = _tile(seq, 256)
    nt = seq // tm
    per = tm // HALO
    halo_spec = pl.BlockSpec((HALO, d), lambda b, i: (jnp.maximum((b * nt + i) * per - 1, 0), 0))
    u_spec = pl.BlockSpec((tm, d), lambda b, i: (b * nt + i, 0))
    out_spec = pl.BlockSpec((tm, d), lambda b, i: (b * nt + i, 0))
    return _pool_call(u, halo_spec, u, u_spec, w_grp16, scale, (batch, nt), tm, 0, True,
                      batch * seq, out_spec, "pool_prompt")


def _pool_sample(u, ctx, layer, w_grp16, scale, row0, n_seq, t, pos0):
    d = u.shape[1]
    r0 = row0 // t
    s0 = layer * n_seq
    halo_spec = pl.BlockSpec((None, HALO, d), lambda s, i: (s0 + s, 0, 0))
    u_spec = pl.BlockSpec((t, d), lambda s, i: (r0 + s, 0))
    out_spec = pl.BlockSpec((t, d), lambda s, i: (s, 0))
    return _pool_call(ctx, halo_spec, u, u_spec, w_grp16, scale, (n_seq, 1), t, pos0, False,
                      n_seq * t, out_spec, "pool_sample")


def _split_bf16(x):
    hi = x.astype(_BF16)
    lo = (x - hi.astype(_F32)).astype(_BF16)
    return hi, lo


def _router_kernel(x_ref, w_ref, meta_ref, count_ref, carry_ref, *, tm, n_experts):
    i = pl.program_id(0)

    @pl.when(i == 0)
    def _():
        carry_ref[...] = jnp.zeros_like(carry_ref)

    xh, xl = _split_bf16(x_ref[...])
    wh, wl = _split_bf16(w_ref[...])
    logits = (jnp.dot(xh, wh, preferred_element_type=_F32)
              + (jnp.dot(xh, wl, preferred_element_type=_F32)
                 + jnp.dot(xl, wh, preferred_element_type=_F32)))
    lane = lax.broadcasted_iota(jnp.int32, (tm, LANES), 1)
    lg = jnp.where(lane < n_experts, logits, -jnp.inf)
    m1 = jnp.max(lg, axis=1, keepdims=True)
    i1 = jnp.min(jnp.where(lg == m1, lane, LANES), axis=1, keepdims=True)
    lg2 = jnp.where(lane == i1, -jnp.inf, lg)
    m2 = jnp.max(lg2, axis=1, keepdims=True)
    i2 = jnp.min(jnp.where(lg2 == m2, lane, LANES), axis=1, keepdims=True)
    e = jnp.exp(m2 - m1)
    g1 = 1.0 / (1.0 + e)
    g2 = e / (1.0 + e)

    sel1 = lane == i1
    sel2 = lane == i2
    cnt = jnp.where(sel1, 1.0, 0.0) + jnp.where(sel2, 1.0, 0.0)
    row = lax.broadcasted_iota(jnp.int32, (tm, tm), 0)
    col = lax.broadcasted_iota(jnp.int32, (tm, tm), 1)
    lower = jnp.where(col < row, 1.0, 0.0).astype(_BF16)
    before = jnp.dot(lower, cnt.astype(_BF16), preferred_element_type=_F32) + carry_ref[0:1, :]
    r1 = jnp.sum(jnp.where(sel1, before, 0.0), axis=1, keepdims=True)
    r2 = jnp.sum(jnp.where(sel2, before, 0.0), axis=1, keepdims=True)
    carry_ref[0:1, :] = carry_ref[0:1, :] + jnp.sum(cnt, axis=0, keepdims=True)
    count_ref[...] = carry_ref[...]

    meta = jnp.zeros((tm, LANES), _F32)
    for k, v in enumerate((i1.astype(_F32), i2.astype(_F32), r1, r2, g1, g2)):
        meta = jnp.where(lane == k, v, meta)
    meta_ref[...] = meta


def _router(x32, w_router):
    n, d = x32.shape
    n_experts = w_router.shape[1]
    tm = _tile(n, 256)
    w_pad = jnp.pad(w_router, ((0, 0), (0, LANES - n_experts)))
    blk = _nbytes((tm, d), _F32) + _nbytes((d, LANES), _F32) + 2 * _nbytes((tm, LANES), _F32)
    return pl.pallas_call(
        functools.partial(_router_kernel, tm=tm, n_experts=n_experts),
        grid=(n // tm,),
        in_specs=[pl.BlockSpec((tm, d), lambda i: (i, 0)),
                  pl.BlockSpec((d, LANES), lambda i: (0, 0))],
        out_specs=[pl.BlockSpec((tm, LANES), lambda i: (i, 0)),
                   pl.BlockSpec((8, LANES), lambda i: (0, 0))],
        out_shape=[jax.ShapeDtypeStruct((n, LANES), _F32), jax.ShapeDtypeStruct((8, LANES), _F32)],
        scratch_shapes=[pltpu.VMEM((8, LANES), _F32)],
        compiler_params=_params(("arbitrary",), blk, 4 * _nbytes((tm, d), _F32)),
        name="moe_router",
    )(x32, w_pad)


def _row_copy(src_hbm, row, dst_vmem, r, sem):
    return pltpu.make_async_copy(src_hbm.at[pl.ds(row, 1), :], dst_vmem.at[pl.ds(r, 1), :], sem)


def _gather_rows(src_hbm, idx_ref, base, dst_vmem, sem, n):
    def issue(i, c):
        for u in range(2):
            r = 2 * i + u
            _row_copy(src_hbm, idx_ref[base + r], dst_vmem, r, sem).start(priority=u)
        return c
    lax.fori_loop(0, n // 2, issue, 0, unroll=2)


def _wait_rows(src_hbm, dst_vmem, sem, n):
    def drain(r, c):
        _row_copy(src_hbm, 0, dst_vmem, r, sem).wait()
        return c
    lax.fori_loop(0, n, drain, 0, unroll=8)


def _moe_gather_kernel(tok_ref, nu_ref, x_hbm, xs_hbm, in_ref, out_ref, in_sem, out_sem, *, tg):
    n_used = nu_ref[0]

    def out_copy(b, slot):
        return pltpu.make_async_copy(out_ref.at[slot], xs_hbm.at[pl.ds(pl.multiple_of(b * tg, tg), tg), :],
                                     out_sem.at[slot])

    _gather_rows(x_hbm, tok_ref, 0, in_ref.at[0], in_sem.at[0], tg)

    def body(b, c):
        slot = b % 2

        @pl.when(b + 1 < n_used)
        def _():
            _gather_rows(x_hbm, tok_ref, (b + 1) * tg, in_ref.at[1 - slot], in_sem.at[1 - slot], tg)

        _wait_rows(x_hbm, in_ref.at[slot], in_sem.at[slot], tg)

        @pl.when(b >= 2)
        def _():
            out_copy(b - 2, slot).wait()

        out_ref[slot] = in_ref[slot].astype(out_ref.dtype)
        out_copy(b, slot).start()
        return c

    lax.fori_loop(0, n_used, body, 0)

    @pl.when(n_used >= 2)
    def _():
        out_copy(n_used - 2, n_used % 2).wait()

    out_copy(n_used - 1, (n_used - 1) % 2).wait()

    n_blocks = xs_hbm.shape[0] // tg
    out_ref[0] = jnp.zeros(out_ref.shape[1:], out_ref.dtype)

    def start_zero(b, c):
        out_copy(b, 0).start()
        return c

    def wait_zero(b, c):
        out_copy(b, 0).wait()
        return c

    lax.fori_loop(n_used, n_blocks, start_zero, 0)
    lax.fori_loop(n_used, n_blocks, wait_zero, 0)


def _moe_gather(x32, row_tok, n_used):
    rows = row_tok.shape[0]
    d = x32.shape[1]
    tg = MOE_BLOCK
    return pl.pallas_call(
        functools.partial(_moe_gather_kernel, tg=tg),
        grid_spec=pltpu.PrefetchScalarGridSpec(
            num_scalar_prefetch=2,
            grid=(1,),
            in_specs=[pl.BlockSpec(memory_space=pl.ANY)],
            out_specs=pl.BlockSpec(memory_space=pl.ANY),
            scratch_shapes=[pltpu.VMEM((2, tg, d), _F32), pltpu.VMEM((2, tg, d), _BF16),
                            pltpu.SemaphoreType.DMA((2,)), pltpu.SemaphoreType.DMA((2,))]),
        out_shape=jax.ShapeDtypeStruct((rows, d), _BF16),
        compiler_params=_params(("arbitrary",), 0, 2 * _nbytes((tg, d), _F32) + 2 * _nbytes((tg, d), _BF16)),
        name="moe_gather",
    )(row_tok, n_used, x32)


def _grouped_rows(gs_ref, gb_ref, e, src_hbm, dst_hbm, in_ref, out_ref, in_sem, out_sem, col, compute, tm, big):
    assert big in (1, 2, 4)
    r0 = gs_ref[e]
    nblk = gb_ref[e]
    n_main = nblk // big
    tails = [big >> k for k in range(1, big.bit_length())]

    def rows(off, nb):
        return pl.ds(pl.multiple_of(r0 + off * tm, tm), nb * tm)

    def in_copy(off, nb, slot):
        return pltpu.make_async_copy(src_hbm.at[rows(off, nb), :], in_ref.at[slot, pl.ds(0, nb * tm), :],
                                     in_sem.at[slot])

    def out_copy(off, nb, slot):
        return pltpu.make_async_copy(out_ref.at[slot, pl.ds(0, nb * tm), :], dst_hbm.at[rows(off, nb), col],
                                     out_sem.at[slot])

    def tail_off(t):
        return (nblk // (2 * t)) * (2 * t)

    def start_first_tail(first, slot, enable):
        pending = enable
        for t in tails[first:]:
            present = (nblk & t) != 0

            @pl.when(jnp.logical_and(pending, present))
            def _():
                in_copy(tail_off(t), t, slot).start(priority=ROW_DMA_PRIORITY)

            pending = jnp.logical_and(pending, jnp.logical_not(present))

    @pl.when(n_main > 0)
    def _():
        in_copy(0, big, 0).start(priority=ROW_DMA_PRIORITY)

    start_first_tail(0, 0, n_main == 0)

    def body(b, c):
        slot = b % 2
        in_copy(b * big, big, slot).wait()

        @pl.when(b + 1 < n_main)
        def _():
            in_copy((b + 1) * big, big, 1 - slot).start(priority=ROW_DMA_PRIORITY)

        start_first_tail(0, 1 - slot, b + 1 == n_main)

        @pl.when(b >= 2)
        def _():
            out_copy((b - 2) * big, big, slot).wait()

        out_ref[slot] = compute(in_ref[slot])
        out_copy(b * big, big, slot).start()
        return c

    lax.fori_loop(0, n_main, body, 0)

    @pl.when(n_main >= 2)
    def _():
        out_copy((n_main - 2) * big, big, n_main % 2).wait()

    @pl.when(n_main >= 1)
    def _():
        out_copy((n_main - 1) * big, big, (n_main - 1) % 2).wait()

    slot = n_main % 2
    tail_slots = []
    for k, t in enumerate(tails):
        present = (nblk & t) != 0
        tail_slots.append(slot)
        cur = slot

        @pl.when(present)
        def _():
            in_copy(tail_off(t), t, cur).wait()
            start_first_tail(k + 1, 1 - cur, True)
            out_ref[cur, 0:t * tm, :] = compute(in_ref[cur, 0:t * tm, :])
            out_copy(tail_off(t), t, cur).start()

        slot = jnp.where(present, 1 - slot, slot)

    for k, t in enumerate(tails):
        @pl.when((nblk & t) != 0)
        def _():
            out_copy(tail_off(t), t, tail_slots[k]).wait()


def _zero_rows(gs_ref, gb_ref, e, dst_hbm, out_ref, out_sem, col, tm):
    r0 = gs_ref[e]
    out_ref[0, 0:tm, :] = jnp.zeros((tm, out_ref.shape[2]), out_ref.dtype)

    def copy(b):
        return pltpu.make_async_copy(out_ref.at[0, pl.ds(0, tm), :],
                                     dst_hbm.at[pl.ds(pl.multiple_of(r0 + b * tm, tm), tm), col], out_sem.at[0])

    def start(b, c):
        copy(b).start()
        return c

    def wait(b, c):
        copy(b).wait()
        return c

    lax.fori_loop(0, gb_ref[e], start, 0)
    lax.fori_loop(0, gb_ref[e], wait, 0)


def _moe_gate_up_kernel(gs_ref, gb_ref, xs_hbm, wg_ref, wu_ref, h_hbm, wg16_ref, wu16_ref,
                        in_ref, out_ref, in_sem, out_sem, *, tm, tf, n_experts):
    j = pl.program_id(0)
    e = pl.program_id(1)
    col = pl.ds(pl.multiple_of(j * tf, tf), tf)

    def compute(x):
        g = jnp.dot(x, wg16_ref[...], preferred_element_type=_F32)
        u = jnp.dot(x, wu16_ref[...], preferred_element_type=_F32)
        return _swiglu(g, u).astype(_BF16)

    @pl.when(e < n_experts)
    def _():
        wg16_ref[...] = wg_ref[...].astype(_BF16)
        wu16_ref[...] = wu_ref[...].astype(_BF16)
        _grouped_rows(gs_ref, gb_ref, e, xs_hbm, h_hbm, in_ref, out_ref, in_sem, out_sem, col, compute, tm,
                      in_ref.shape[1] // tm)

    @pl.when(e == n_experts)
    def _():
        _zero_rows(gs_ref, gb_ref, e, h_hbm, out_ref, out_sem, col, tm)


def _moe_gate_up(xs16, w_gu, gstart, gblocks):
    rows, d = xs16.shape
    n_experts = w_gu.shape[0]
    f = w_gu.shape[2] // 2
    tm = MOE_BLOCK
    tf = _tile(f, 1024)
    nf = f // tf
    blk = 2 * _nbytes((d, tf), _F32)
    tc = MOE_CHUNK_BLOCKS * tm
    scratch_bytes = (2 * _nbytes((d, tf), _BF16) + 2 * _nbytes((tc, d), _BF16) + 2 * _nbytes((tc, tf), _BF16)
                     + 3 * _nbytes((tc, tf), _F32))
    last = n_experts - 1
    return pl.pallas_call(
        functools.partial(_moe_gate_up_kernel, tm=tm, tf=tf, n_experts=n_experts),
        grid_spec=pltpu.PrefetchScalarGridSpec(
            num_scalar_prefetch=2,
            grid=(nf, n_experts + 1),
            in_specs=[pl.BlockSpec(memory_space=pl.ANY),
                      pl.BlockSpec((None, d, tf), lambda j, e, gs, gb: (jnp.minimum(e, last), 0, j)),
                      pl.BlockSpec((None, d, tf), lambda j, e, gs, gb: (jnp.minimum(e, last), 0, nf + j))],
            out_specs=pl.BlockSpec(memory_space=pl.ANY),
            scratch_shapes=[pltpu.VMEM((d, tf), _BF16), pltpu.VMEM((d, tf), _BF16),
                            pltpu.VMEM((2, tc, d), _BF16), pltpu.VMEM((2, tc, tf), _BF16),
                            pltpu.SemaphoreType.DMA((2,)), pltpu.SemaphoreType.DMA((2,))]),
        out_shape=jax.ShapeDtypeStruct((rows, f), _BF16),
        compiler_params=_params(("arbitrary", "arbitrary"), blk, scratch_bytes),
        name="moe_gate_up",
    )(gstart, gblocks, xs16, w_gu, w_gu)


def _moe_down_kernel(gs_ref, gb_ref, h_hbm, w_ref, y_hbm, w16_ref, in_ref, out_ref, in_sem, out_sem,
                     *, tm, tn, n_experts):
    j = pl.program_id(0)
    e = pl.program_id(1)
    col = pl.ds(pl.multiple_of(j * tn, tn), tn)

    def compute(h):
        return jnp.dot(h, w16_ref[...], preferred_element_type=_F32)

    @pl.when(e < n_experts)
    def _():
        w16_ref[...] = w_ref[...].astype(_BF16)
        _grouped_rows(gs_ref, gb_ref, e, h_hbm, y_hbm, in_ref, out_ref, in_sem, out_sem, col, compute, tm,
                      in_ref.shape[1] // tm)

    @pl.when(e == n_experts)
    def _():
        _zero_rows(gs_ref, gb_ref, e, y_hbm, out_ref, out_sem, col, tm)


def _moe_down(h16, w_down, gstart, gblocks):
    rows, f = h16.shape
    n_experts = w_down.shape[0]
    d = w_down.shape[2]
    tm = MOE_BLOCK
    tn = _tile(d, 512)
    tc = tm
    blk = _nbytes((f, tn), _F32)
    scratch_bytes = (_nbytes((f, tn), _BF16) + 2 * _nbytes((tc, f), _BF16) + 3 * _nbytes((tc, tn), _F32))
    last = n_experts - 1
    return pl.pallas_call(
        functools.partial(_moe_down_kernel, tm=tm, tn=tn, n_experts=n_experts),
        grid_spec=pltpu.PrefetchScalarGridSpec(
            num_scalar_prefetch=2,
            grid=(d // tn, n_experts + 1),
            in_specs=[pl.BlockSpec(memory_space=pl.ANY),
                      pl.BlockSpec((None, f, tn), lambda j, e, gs, gb: (jnp.minimum(e, last), 0, j))],
            out_specs=pl.BlockSpec(memory_space=pl.ANY),
            scratch_shapes=[pltpu.VMEM((f, tn), _BF16),
                            pltpu.VMEM((2, tc, f), _BF16), pltpu.VMEM((2, tc, tn), _F32),
                            pltpu.SemaphoreType.DMA((2,)), pltpu.SemaphoreType.DMA((2,))]),
        out_shape=jax.ShapeDtypeStruct((rows, d), _F32),
        compiler_params=_params(("arbitrary", "arbitrary"), blk, scratch_bytes),
        name="moe_down",
    )(gstart, gblocks, h16, w_down)


def _moe_combine_kernel(d1_ref, d2_ref, y_hbm, meta_ref, res_ref, g_ref, b_ref, *rest, tc, alpha, n_split):
    outs, (buf_ref, sem) = rest[:-2], rest[-2:]
    i = pl.program_id(0)
    slot = i % 2

    def fetch(step, s):
        _gather_rows(y_hbm, d1_ref, step * tc, buf_ref.at[s, 0], sem.at[s], tc)
        _gather_rows(y_hbm, d2_ref, step * tc, buf_ref.at[s, 1], sem.at[s], tc)

    @pl.when(i == 0)
    def _():
        fetch(0, 0)

    @pl.when(i + 1 < pl.num_programs(0))
    def _():
        fetch(i + 1, 1 - slot)

    _wait_rows(y_hbm, buf_ref.at[slot, 0], sem.at[slot], tc)
    _wait_rows(y_hbm, buf_ref.at[slot, 1], sem.at[slot], tc)
    meta = meta_ref[...]
    f = buf_ref[slot, 0] * meta[:, 4:5] + buf_ref[slot, 1] * meta[:, 5:6]
    y = _res_ln(f, res_ref[...], g_ref[...], b_ref[...], alpha)
    if n_split is None:
        outs[0][...] = y
        outs[1][...] = y.astype(_BF16)
    else:
        @pl.when(i < n_split)
        def _():
            outs[0][...] = y

        @pl.when(i >= n_split)
        def _():
            outs[1][...] = y


def _moe_combine_ln(y_rows, dest1, dest2, meta, res, g, b, alpha, split_rows):
    n, d = res.shape
    tc = _tile(n if split_rows is None else math.gcd(n, split_rows), 128)
    blk = _nbytes((tc, LANES), _F32) + 2 * _nbytes((tc, d), _F32) + _nbytes((tc, d), _BF16)
    tile_spec = pl.BlockSpec((tc, d), lambda i, a, c: (i, 0))
    if split_rows is None:
        n_split = None
        out_specs = [tile_spec, tile_spec]
        out_shape = [jax.ShapeDtypeStruct((n, d), _F32), jax.ShapeDtypeStruct((n, d), _BF16)]
    else:
        n_split = split_rows // tc
        out_specs = [pl.BlockSpec((tc, d), lambda i, a, c: (jnp.minimum(i, n_split - 1), 0)),
                     pl.BlockSpec((tc, d), lambda i, a, c: (jnp.maximum(i - n_split, 0), 0))]
        out_shape = [jax.ShapeDtypeStruct((split_rows, d), _F32),
                     jax.ShapeDtypeStruct((n - split_rows, d), _F32)]
    return pl.pallas_call(
        functools.partial(_moe_combine_kernel, tc=tc, alpha=alpha, n_split=n_split),
        grid_spec=pltpu.PrefetchScalarGridSpec(
            num_scalar_prefetch=2,
            grid=(n // tc,),
            in_specs=[pl.BlockSpec(memory_space=pl.ANY),
                      pl.BlockSpec((tc, LANES), lambda i, a, c: (i, 0)),
                      tile_spec,
                      pl.BlockSpec((1, d), lambda i, a, c: (0, 0)),
                      pl.BlockSpec((1, d), lambda i, a, c: (0, 0))],
            out_specs=out_specs,
            scratch_shapes=[pltpu.VMEM((2, 2, tc, d), _F32), pltpu.SemaphoreType.DMA((2,))]),
        out_shape=out_shape,
        compiler_params=_params(("arbitrary",), blk, 8 * _nbytes((tc, d), _F32)),
        name="moe_combine_ln",
    )(dest1, dest2, y_rows, meta, res, g.reshape(1, d), b.reshape(1, d))


def _moe_layer(x32, w_router, w_gu, w_down, g, b, alpha, split_rows):
    n, d = x32.shape
    n_experts = w_router.shape[1]
    meta, counts = _router(x32, w_router)
    e1 = meta[:, 0].astype(jnp.int32)
    e2 = meta[:, 1].astype(jnp.int32)
    counts = counts[0, :n_experts].astype(jnp.int32)
    gblocks = (counts + MOE_BLOCK - 1) // MOE_BLOCK
    gend = jnp.cumsum(gblocks) * MOE_BLOCK
    gstart = gend - gblocks * MOE_BLOCK
    dest1 = gstart[e1] + meta[:, 2].astype(jnp.int32)
    dest2 = gstart[e2] + meta[:, 3].astype(jnp.int32)
    nb = (n * TOP_K) // MOE_BLOCK + n_experts
    rows = nb * MOE_BLOCK
    tok = jnp.arange(n, dtype=jnp.int32)
    row_tok = jnp.zeros((rows,), jnp.int32).at[dest1].set(tok).at[dest2].set(tok)
    n_used = (gend[-1] // MOE_BLOCK).astype(jnp.int32).reshape(1)
    gstart = jnp.concatenate([gstart, gend[-1:]]).astype(jnp.int32)
    gblocks = jnp.concatenate([gblocks, nb - n_used]).astype(jnp.int32)

    xs16 = _moe_gather(x32, row_tok, n_used)
    h16 = _moe_gate_up(xs16, w_gu, gstart, gblocks)
    y_rows = _moe_down(h16, w_down, gstart, gblocks)
    return _moe_combine_ln(y_rows, dest1, dest2, meta, x32, g, b, alpha, split_rows)


def kernel(x_prompt, x_sample, cache_k, cache_v, state_pool, attn_w_in, attn_w_o, attn_lambda, attn_subln_g, pool_w_in, pool_w_grp, pool_scale, pool_w_o, ln_mix_g, ln_mix_b, ln_ffn_g, ln_ffn_b, ffn_w_gu, ffn_w_down, moe_w_router, moe_w_gu, moe_w_down):
    batch, seq, d = x_prompt.shape
    n_seq, t, _ = x_sample.shape
    past = cache_k.shape[2]
    depth = ln_mix_g.shape[0]
    alpha = (2 * depth) ** 0.25
    n_p = batch * seq
    n_s = n_seq * t
    n = n_p + n_s
    n_heads = cache_k.shape[3]
    qk_width = n_heads * 2 * HEAD_DIM
    v_width = n_heads * V_DIM

    x32 = jnp.concatenate([x_prompt.reshape(n_p, d), x_sample.reshape(n_s, d)], axis=0)
    x16 = x32.astype(_BF16)
    tab_p = _rope_tables(np.arange(seq))
    tab_s = _rope_tables(np.tile(past + np.arange(t), n_seq))
    past_k = cache_k.reshape(-1, past * n_heads, 2 * HEAD_DIM)
    past_v = cache_v.reshape(-1, past * n_heads, V_DIM)
    ctx = jnp.pad(state_pool, ((0, 0), (0, 0), (HALO - POOL_CTX, 0), (0, 0))).reshape(-1, HALO, d)

    kp_l, vp_l, ks_l, vs_l, pp_l, ps_l = [], [], [], [], [], []
    y_p = y_s = None
    for i in range(depth):
        j = i // 2
        last = i == depth - 1
        if i % 2 == 0:
            lam_init = 0.8 - 0.6 * math.exp(-0.3 * i)
            w_in16 = attn_w_in[j].astype(_BF16)
            g = attn_subln_g[j].reshape(1, V_DIM)
            rows_p, rows_s = (0, n_p, tab_p), (n_p, n_s, tab_s)
            (q_p,), (q_s,) = (_proj(x16, w_in16, 0, qk_width, r0, nr, tab, False, True, "proj_q", Q_SCALE)
                              for r0, nr, tab in (rows_p, rows_s))
            (k32_p, k_p), (k32_s, k_s) = (_proj(x16, w_in16, qk_width, qk_width, r0, nr, tab, True, True, "proj_k")
                                          for r0, nr, tab in (rows_p, rows_s))
            (v32_p, v_p), (v32_s, v_s) = (_proj(x16, w_in16, 2 * qk_width, v_width, r0, nr, None, True, True, "proj_v")
                                          for r0, nr, tab in (rows_p, rows_s))
            mix = [_attn_prompt(q_p, k_p, v_p, attn_lambda[j], g, batch, seq, lam_init),
                   _attn_sample(q_s, k_s, v_s, past_k, past_v, j, attn_lambda[j], g, n_seq, t, lam_init)]
            w_o16 = attn_w_o[j].astype(_BF16)
            kp_l.append(k32_p.reshape(batch, seq, n_heads, 2 * HEAD_DIM))
            vp_l.append(v32_p.reshape(batch, seq, n_heads, V_DIM))
            ks_l.append(k32_s.reshape(n_seq, t, n_heads, 2 * HEAD_DIM))
            vs_l.append(v32_s.reshape(n_seq, t, n_heads, V_DIM))
        else:
            u32, = _proj(x16, pool_w_in[j].astype(_BF16), 0, d, 0, n, None, True, False, "pool_in")
            w_grp16 = pool_w_grp[j].astype(_BF16)
            mix = [_pool_prompt(u32, w_grp16, pool_scale[j], batch, seq),
                   _pool_sample(u32, ctx, j, w_grp16, pool_scale[j], n_p, n_seq, t, past)]
            w_o16 = pool_w_o[j].astype(_BF16)
            u_p = u32[:n_p].reshape(batch, seq, d)
            u_s = u32[n_p:].reshape(n_seq, t, d)
            pp_l.append(jnp.concatenate([jnp.zeros((batch, POOL_CTX, d), _F32), u_p[:, -POOL_CTX:]], 1)[:, -POOL_CTX:])
            ps_l.append(jnp.concatenate([state_pool[j], u_s[:, -POOL_CTX:]], 1)[:, -POOL_CTX:])
        x32, x16 = _mm_res_ln(mix, w_o16, x32, ln_mix_g[i], ln_mix_b[i], alpha, "mix_out_ln")
        if i % 2 == 0:
            h16 = _gate_up(x16, ffn_w_gu[j].astype(_BF16))
            x32, x16 = _mm_res_ln([h16], ffn_w_down[j].astype(_BF16), x32, ln_ffn_g[i], ln_ffn_b[i],
                                  alpha, "ffn_down_ln")
        elif last:
            y_p, y_s = _moe_layer(x32, moe_w_router[j], moe_w_gu[j], moe_w_down[j],
                                  ln_ffn_g[i], ln_ffn_b[i], alpha, n_p)
        else:
            x32, x16 = _moe_layer(x32, moe_w_router[j], moe_w_gu[j], moe_w_down[j],
                                  ln_ffn_g[i], ln_ffn_b[i], alpha, None)
    if y_p is None:
        y_p, y_s = x32[:n_p], x32[n_p:]
    return (y_p.reshape(batch, seq, d), y_s.reshape(n_seq, t, d), jnp.stack(kp_l), jnp.stack(vp_l),
            jnp.stack(pp_l), jnp.stack(ks_l), jnp.stack(vs_l), jnp.stack(ps_l))
```

```python
import functools
import math

import numpy as np
import jax
import jax.numpy as jnp
from jax import lax
from jax.experimental import pallas as pl
from jax.experimental.pallas import tpu as pltpu

CHUNK = 64
HEAD_DIM = 64
V_DIM = 2 * HEAD_DIM
ROT_DIM = HEAD_DIM // 4
ROPE_THETA = 500000.0
POOL_WINDOWS = (2, 4, 8, 16)
POOL_CTX = max(POOL_WINDOWS) - 1
TOP_K = 2
LN_EPS = 1e-5

LANES = 128
V7X_VMEM_REQUEST_CAP = 56 * 1024 * 1024
COMPILER_SCRATCH_ALLOWANCE = 6 * 1024 * 1024

HALO = 16
MOE_BLOCK = 256
MOE_CHUNK_BLOCKS = 2
ROW_DMA_PRIORITY = 1
Q_SCALE = HEAD_DIM ** -0.5 * math.log2(math.e)

_F32 = jnp.float32
_BF16 = jnp.bfloat16
_NT = (((1,), (1,)), ((), ()))


def _nbytes(shape, dtype):
    return math.prod(shape) * jnp.dtype(dtype).itemsize


def _params(semantics, pipelined_bytes, scratch_bytes=0):
    need = 2 * pipelined_bytes + scratch_bytes + COMPILER_SCRATCH_ALLOWANCE
    return pltpu.CompilerParams(dimension_semantics=semantics,
                                vmem_limit_bytes=min(need, V7X_VMEM_REQUEST_CAP))


def _tile(n, pref):
    t = min(n, pref)
    while n % t:
        t //= 2
    return t


def _rope(acc, cos, sin_lo, sin_hi):
    half = ROT_DIM // 2
    pieces = []
    for g in range(acc.shape[1] // LANES):
        xg = acc[:, g * LANES:(g + 1) * LANES]
        pieces.append(xg * cos
                      + pltpu.roll(xg, half, 1) * sin_hi
                      + pltpu.roll(xg, LANES - half, 1) * sin_lo)
    return pieces[0] if len(pieces) == 1 else jnp.concatenate(pieces, axis=1)


def _proj_kernel(*refs, rope, want32, want16, scale):
    x_ref, w_ref = refs[0], refs[1]
    acc = jnp.dot(x_ref[...], w_ref[...], preferred_element_type=_F32)
    pos = 2
    if rope:
        acc = _rope(acc, refs[2][...], refs[3][...], refs[4][...])
        pos = 5
    if scale is not None:
        acc = acc * scale
    if want32:
        refs[pos][...] = acc
        pos += 1
    if want16:
        refs[pos][...] = acc.astype(_BF16)


def _proj(xb, w, col0, ncols, row0, nrows, tables, want32, want16, name, scale=None):
    k = xb.shape[1]
    period = nrows if tables is None else tables[0].shape[0]
    tm = _tile(math.gcd(math.gcd(row0, nrows), period) if row0 else math.gcd(nrows, period), 1024)
    tn = _tile(ncols, 512)
    off = col0 // tn
    rb0 = row0 // tm
    tper = period // tm
    in_specs = [pl.BlockSpec((tm, k), lambda i, j: (rb0 + i, 0)),
                pl.BlockSpec((k, tn), lambda i, j: (0, j + off))]
    args = [xb, w]
    blk = _nbytes((tm, k), _BF16) + _nbytes((k, tn), _BF16)
    if tables is not None:
        for t in tables:
            in_specs.append(pl.BlockSpec((tm, LANES), lambda i, j: (i % tper, 0)))
            args.append(t)
            blk += _nbytes((tm, LANES), _F32)
    out_shape, out_specs = [], []
    for want, dt in ((want32, _F32), (want16, _BF16)):
        if want:
            out_shape.append(jax.ShapeDtypeStruct((nrows, ncols), dt))
            out_specs.append(pl.BlockSpec((tm, tn), lambda i, j: (i, j)))
            blk += _nbytes((tm, tn), dt)
    return pl.pallas_call(
        functools.partial(_proj_kernel, rope=tables is not None, want32=want32, want16=want16,
                          scale=scale),
        grid=(nrows // tm, ncols // tn),
        in_specs=in_specs, out_specs=out_specs, out_shape=out_shape,
        compiler_params=_params(("parallel", "parallel"), blk, _nbytes((tm, tn), _F32)),
        name=name,
    )(*args)


def _rope_tables(pos):
    half = ROT_DIM // 2
    inv_freq = ROPE_THETA ** (-np.arange(0, ROT_DIM, 2, dtype=np.float64) / ROT_DIM)
    ang = np.asarray(pos, np.float64)[:, None] * inv_freq[None, :]
    cos, sin = np.cos(ang), np.sin(ang)
    t = ang.shape[0]
    ones = np.ones((t, HEAD_DIM - ROT_DIM))
    zeros = np.zeros((t, HEAD_DIM - ROT_DIM))
    zh = np.zeros((t, half))
    cos_map = np.concatenate([cos, cos, ones], 1)
    lo_map = np.concatenate([-sin, zh, zeros], 1)
    hi_map = np.concatenate([zh, sin, zeros], 1)
    return tuple(jnp.asarray(np.concatenate([a, a], 1), _F32) for a in (cos_map, lo_map, hi_map))


def _stack_maps(q):
    lane = lax.broadcasted_iota(jnp.int32, q.shape, 1)
    zero = jnp.zeros_like(q)
    return jnp.concatenate([jnp.where(lane < HEAD_DIM, q, zero),
                            jnp.where(lane >= HEAD_DIM, q, zero)], axis=0)


def _diff_lambda(lam_ref, lam_init):
    lf = lam_ref[...]
    a = jnp.sum(lf[0:1] * lf[1:2], axis=1, keepdims=True)
    b = jnp.sum(lf[2:3] * lf[3:4], axis=1, keepdims=True)
    return jnp.exp(a) - jnp.exp(b) + lam_init


def _diff_finish(acc, l, tq, lam, g, lam_init):
    o = acc / l
    o = o[:tq] - lam * o[tq:]
    ms = jnp.mean(o * o, axis=1, keepdims=True)
    return (o * lax.rsqrt(ms + LN_EPS) * g) * (1.0 - lam_init)


def _attn_prompt_kernel(q_ref, k_ref, v_ref, lam_ref, g_ref, o_ref, sa_ref, sb_ref, *, tq, tk, lam_init):
    q0 = pl.program_id(2) * tq
    qs = _stack_maps(q_ref[...])
    n_full = q0 // tk

    def scores(kb, dst_ref):
        k0 = pl.multiple_of(kb * tk, tk)
        dst_ref[...] = lax.dot_general(qs, k_ref[pl.ds(k0, tk), :], _NT, preferred_element_type=_F32)

    def consume(kb, src_ref, carry, masked):
        m, l, acc = carry
        k0 = pl.multiple_of(kb * tk, tk)
        s = src_ref[...]
        if masked:
            qc = (q0 + lax.broadcasted_iota(jnp.int32, (tq, tk), 0)) // CHUNK
            kc = (k0 + lax.broadcasted_iota(jnp.int32, (tq, tk), 1)) // CHUNK
            vis = kc <= qc
            s = jnp.where(jnp.concatenate([vis, vis], axis=0), s, -jnp.inf)
        m_new = jnp.maximum(m, jnp.max(s, axis=1, keepdims=True))
        p = jnp.exp2(s - m_new)
        alpha = jnp.exp2(m - m_new)
        l = alpha * l + jnp.sum(p, axis=1, keepdims=True)
        acc = alpha * acc + jnp.dot(p.astype(_BF16), v_ref[pl.ds(k0, tk), :], preferred_element_type=_F32)
        return m_new, l, acc

    def pair(p, carry):
        kb = 2 * p
        scores(kb + 1, sb_ref)
        carry = consume(kb, sa_ref, carry, False)
        scores(kb + 2, sa_ref)
        return consume(kb + 1, sb_ref, carry, False)

    def odd_tail(carry):
        scores(n_full, sb_ref)
        carry = consume(n_full - 1, sa_ref, carry, False)
        return consume(n_full, sb_ref, carry, True)

    def even_tail(carry):
        return consume(n_full, sa_ref, carry, True)

    carry = (jnp.full((2 * tq, 1), -jnp.inf, _F32), jnp.zeros((2 * tq, 1), _F32),
             jnp.zeros((2 * tq, V_DIM), _F32))
    scores(0, sa_ref)
    carry = lax.fori_loop(0, n_full // 2, pair, carry)
    _, l, acc = lax.cond(n_full % 2 == 1, odd_tail, even_tail, carry)
    lam = _diff_lambda(lam_ref, lam_init)
    o_ref[...] = _diff_finish(acc, l, tq, lam, g_ref[...], lam_init).astype(o_ref.dtype)


def _attn_prompt(q16, k16, v16, lam, g, batch, seq, lam_init):
    n_heads = q16.shape[1] // LANES
    tq = _tile(seq, 256)
    tk = _tile(seq, 512)
    assert tk % tq == 0
    nq = seq // tq
    blk = (_nbytes((tq, LANES), _BF16) * 2 + 2 * _nbytes((seq, LANES), _BF16))
    return pl.pallas_call(
        functools.partial(_attn_prompt_kernel, tq=tq, tk=tk, lam_init=lam_init),
        grid=(batch, n_heads, nq),
        in_specs=[pl.BlockSpec((tq, LANES), lambda b, h, i: (b * nq + i, h)),
                  pl.BlockSpec((seq, LANES), lambda b, h, i: (b, h)),
                  pl.BlockSpec((seq, LANES), lambda b, h, i: (b, h)),
                  pl.BlockSpec(lam.shape, lambda b, h, i: (0, 0)),
                  pl.BlockSpec((1, V_DIM), lambda b, h, i: (0, 0))],
        out_specs=pl.BlockSpec((tq, LANES), lambda b, h, i: (b * nq + i, h)),
        out_shape=jax.ShapeDtypeStruct((batch * seq, n_heads * V_DIM), _BF16),
        scratch_shapes=[pltpu.VMEM((2 * tq, tk), _F32), pltpu.VMEM((2 * tq, tk), _F32)],
        compiler_params=_params(("parallel", "parallel", "parallel"), blk,
                                6 * _nbytes((2 * tq, tk), _F32)),
        name="attn_prompt",
    )(q16, k16, v16, lam, g)


def _attn_sample_kernel(q_ref, kn_ref, vn_ref, kp_ref, vp_ref, lam_ref, g_ref, o_ref, *,
                        t, past, heads, lam_init):
    lam = _diff_lambda(lam_ref, lam_init)
    g = g_ref[...]
    qc = (past + lax.broadcasted_iota(jnp.int32, (t, t), 0)) // CHUNK
    kc = (past + lax.broadcasted_iota(jnp.int32, (t, t), 1)) // CHUNK
    vis = jnp.concatenate([kc <= qc, kc <= qc], axis=0)
    outs = []
    for h in range(heads):
        sl = slice(h * LANES, (h + 1) * LANES)
        qs = _stack_maps(q_ref[:, sl])
        kp = kp_ref[pl.ds(h, past, stride=heads), :].astype(_BF16)
        vp = vp_ref[pl.ds(h, past, stride=heads), :].astype(_BF16)
        s_p = lax.dot_general(qs, kp, _NT, preferred_element_type=_F32)
        s_n = lax.dot_general(qs, kn_ref[:, sl], _NT, preferred_element_type=_F32)
        s_n = jnp.where(vis, s_n, -jnp.inf)
        m = jnp.maximum(jnp.max(s_p, axis=1, keepdims=True), jnp.max(s_n, axis=1, keepdims=True))
        p_p = jnp.exp2(s_p - m)
        p_n = jnp.exp2(s_n - m)
        l = jnp.sum(p_p, axis=1, keepdims=True) + jnp.sum(p_n, axis=1, keepdims=True)
        acc = (jnp.dot(p_p.astype(_BF16), vp, preferred_element_type=_F32)
               + jnp.dot(p_n.astype(_BF16), vn_ref[:, sl], preferred_element_type=_F32))
        outs.append(_diff_finish(acc, l, t, lam, g, lam_init))
    o_ref[...] = jnp.concatenate(outs, axis=1).astype(o_ref.dtype)


def _attn_sample(q16, k16, v16, past_k, past_v, layer, lam, g, n_seq, t, lam_init):
    width = q16.shape[1]
    heads = width // LANES
    past = past_k.shape[1] // heads
    s0 = layer * n_seq
    new_spec = pl.BlockSpec((t, width), lambda s: (s, 0))
    past_spec = pl.BlockSpec((None, past * heads, LANES), lambda s: (s0 + s, 0, 0))
    blk = 4 * _nbytes((t, width), _BF16) + 2 * _nbytes((past * heads, LANES), _F32)
    return pl.pallas_call(
        functools.partial(_attn_sample_kernel, t=t, past=past, heads=heads, lam_init=lam_init),
        grid=(n_seq,),
        in_specs=[new_spec, new_spec, new_spec, past_spec, past_spec,
                  pl.BlockSpec(lam.shape, lambda s: (0, 0)),
                  pl.BlockSpec((1, V_DIM), lambda s: (0, 0))],
        out_specs=new_spec,
        out_shape=jax.ShapeDtypeStruct((n_seq * t, width), _BF16),
        compiler_params=_params(("parallel",), blk, 6 * _nbytes((2 * t, past), _F32)),
        name="attn_sample",
    )(q16, k16, v16, past_k, past_v, lam, g)


def _res_ln(acc, res, g, b, alpha):
    y = alpha * res + acc
    mu = jnp.mean(y, axis=1, keepdims=True)
    yc = y - mu
    var = jnp.mean(yc * yc, axis=1, keepdims=True)
    return yc * lax.rsqrt(var + LN_EPS) * g + b


def _mm_res_ln_kernel(*refs, nk, alpha, part_starts):
    n_parts = len(part_starts)
    a_refs = refs[:n_parts]
    w_ref, res_ref, g_ref, b_ref, o32_ref, o16_ref = refs[n_parts:n_parts + 6]
    scratch = refs[n_parts + 6:]

    def finish(acc):
        y = _res_ln(acc, res_ref[...], g_ref[...], b_ref[...], alpha)
        o32_ref[...] = y
        o16_ref[...] = y.astype(_BF16)

    a = a_refs[0][...]
    for p in range(1, n_parts):
        a = jnp.where(pl.program_id(0) >= part_starts[p], a_refs[p][...], a)
    part = jnp.dot(a, w_ref[...], preferred_element_type=_F32)
    if nk == 1:
        finish(part)
        return
    acc_ref, = scratch
    kk = pl.program_id(1)

    @pl.when(kk == 0)
    def _():
        acc_ref[...] = part

    @pl.when(kk > 0)
    def _():
        acc_ref[...] += part

    @pl.when(kk == nk - 1)
    def _():
        finish(acc_ref[...])


def _mm_res_ln(parts, w16, res, g, b, alpha, name):
    m = sum(a.shape[0] for a in parts)
    k = parts[0].shape[1]
    d = w16.shape[1]
    resident = _nbytes((k, d), _BF16) <= V7X_VMEM_REQUEST_CAP // 2
    tk = k if resident else _tile(k, 512)
    nk = k // tk
    tm = _tile(functools.reduce(math.gcd, [a.shape[0] for a in parts]), 512 if k <= 2048 else 256)
    blk = (len(parts) * _nbytes((tm, tk), _BF16) + 2 * _nbytes((tm, d), _F32) + _nbytes((tm, d), _BF16))
    w_bytes = _nbytes((tk, d), _BF16) * (1 if resident else 2)
    w_mode = dict(pipeline_mode=pl.Buffered(1)) if resident else {}
    scratch = [pltpu.VMEM((tm, d), _F32)] if nk > 1 else []
    part_starts, part_specs, start = [], [], 0
    for a in parts:
        nblk = a.shape[0] // tm
        part_starts.append(start)
        part_specs.append(pl.BlockSpec(
            (tm, tk), lambda i, kk, start=start, nblk=nblk: (jnp.clip(i - start, 0, nblk - 1), kk)))
        start += nblk
    return pl.pallas_call(
        functools.partial(_mm_res_ln_kernel, nk=nk, alpha=alpha, part_starts=tuple(part_starts)),
        grid=(m // tm, nk),
        in_specs=part_specs + [
                  pl.BlockSpec((tk, d), lambda i, kk: (kk, 0), **w_mode),
                  pl.BlockSpec((tm, d), lambda i, kk: (i, 0)),
                  pl.BlockSpec((1, d), lambda i, kk: (0, 0)),
                  pl.BlockSpec((1, d), lambda i, kk: (0, 0))],
        out_specs=[pl.BlockSpec((tm, d), lambda i, kk: (i, 0)),
                   pl.BlockSpec((tm, d), lambda i, kk: (i, 0))],
        out_shape=[jax.ShapeDtypeStruct((m, d), _F32), jax.ShapeDtypeStruct((m, d), _BF16)],
        scratch_shapes=scratch,
        compiler_params=_params(("parallel", "arbitrary"), blk, w_bytes + 2 * _nbytes((tm, d), _F32)),
        name=name,
    )(*parts, w16, res, g.reshape(1, d), b.reshape(1, d))


def _swiglu(g, u):
    return g * jax.nn.sigmoid(g) * u


def _gate_up_kernel(x_ref, wg_ref, wu_ref, h_ref):
    x = x_ref[...]
    g = jnp.dot(x, wg_ref[...], preferred_element_type=_F32)
    u = jnp.dot(x, wu_ref[...], preferred_element_type=_F32)
    h_ref[...] = _swiglu(g, u).astype(h_ref.dtype)


def _gate_up(x16, w_gu16):
    m, k = x16.shape
    f = w_gu16.shape[1] // 2
    tm = _tile(m, 1024)
    tf = _tile(f, 512)
    nf = f // tf
    blk = _nbytes((tm, k), _BF16) + 2 * _nbytes((k, tf), _BF16) + _nbytes((tm, tf), _BF16)
    return pl.pallas_call(
        _gate_up_kernel,
        grid=(m // tm, nf),
        in_specs=[pl.BlockSpec((tm, k), lambda i, j: (i, 0)),
                  pl.BlockSpec((k, tf), lambda i, j: (0, j)),
                  pl.BlockSpec((k, tf), lambda i, j: (0, nf + j))],
        out_specs=pl.BlockSpec((tm, tf), lambda i, j: (i, j)),
        out_shape=jax.ShapeDtypeStruct((m, f), _BF16),
        compiler_params=_params(("parallel", "parallel"), blk, 3 * _nbytes((tm, tf), _F32)),
        name="ffn_gate_up",
    )(x16, w_gu16, w_gu16)


def _pool_kernel(halo_ref, u_ref, w_ref, scale_ref, z_ref, ext_ref, *, tm, pos0, zero_first):
    i = pl.program_id(1)
    halo = halo_ref[...]
    if zero_first:
        halo = jnp.where(i == 0, jnp.zeros_like(halo), halo)
    ext_ref[0:HALO, :] = halo
    ext_ref[HALO:HALO + tm, :] = u_ref[...]
    pos = pos0 + i * tm + lax.broadcasted_iota(jnp.int32, (tm, 1), 0)
    gd = u_ref.shape[1] // len(POOL_WINDOWS)
    for g, w in enumerate(POOL_WINDOWS):
        cols = slice(g * gd, (g + 1) * gd)
        win = ext_ref[HALO:HALO + tm, cols]
        for back in range(1, w):
            win = win + ext_ref[HALO - back:HALO - back + tm, cols]
        count = jnp.minimum(w, pos + 1).astype(_F32)
        d = win / count - u_ref[:, cols]
        zg = jnp.dot(d.astype(_BF16), w_ref[g], preferred_element_type=_F32)
        z_ref[:, cols] = (zg * scale_ref[:, cols]).astype(z_ref.dtype)


def _pool_call(halo_arr, halo_spec, u, u_spec, w_grp16, scale, grid, tm, pos0, zero_first,
               out_rows, out_spec, name):
    d = u.shape[1]
    blk = (_nbytes((HALO, d), _F32) + _nbytes((tm, d), _F32) + _nbytes(w_grp16.shape, _BF16)
           + _nbytes((tm, d), _BF16))
    return pl.pallas_call(
        functools.partial(_pool_kernel, tm=tm, pos0=pos0, zero_first=zero_first),
        grid=grid,
        in_specs=[halo_spec, u_spec,
                  pl.BlockSpec(w_grp16.shape, lambda b, i: (0, 0, 0)),
                  pl.BlockSpec((1, d), lambda b, i: (0, 0))],
        out_specs=out_spec,
        out_shape=jax.ShapeDtypeStruct((out_rows, d), _BF16),
        scratch_shapes=[pltpu.VMEM((HALO + tm, d), _F32)],
        compiler_params=_params(("parallel", "arbitrary"), blk, 3 * _nbytes((HALO + tm, d), _F32)),
        name=name,
    )(halo_arr, u, w_grp16, scale.reshape(1, d))


def _pool_prompt(u, w_grp16, scale, batch, seq):
    d = u.shape[1]
    tm = _tile(seq, 256)
    nt = seq // tm
    per = tm // HALO
    halo_spec = pl.BlockSpec((HALO, d), lambda b, i: (jnp.maximum((b * nt + i) * per - 1, 0), 0))
    u_spec = pl.BlockSpec((tm, d), lambda b, i: (b * nt + i, 0))
    out_spec = pl.BlockSpec((tm, d), lambda b, i: (b * nt + i, 0))
    return _pool_call(u, halo_spec, u, u_spec, w_grp16, scale, (batch, nt), tm, 0, True,
                      batch * seq, out_spec, "pool_prompt")


def _pool_sample(u, ctx, layer, w_grp16, scale, row0, n_seq, t, pos0):
    d = u.shape[1]
    r0 = row0 // t
    s0 = layer * n_seq
    halo_spec = pl.BlockSpec((None, HALO, d), lambda s, i: (s0 + s, 0, 0))
    u_spec = pl.BlockSpec((t, d), lambda s, i: (r0 + s, 0))
    out_spec = pl.BlockSpec((t, d), lambda s, i: (s, 0))
    return _pool_call(ctx, halo_spec, u, u_spec, w_grp16, scale, (n_seq, 1), t, pos0, False,
                      n_seq * t, out_spec, "pool_sample")


def _split_bf16(x):
    hi = x.astype(_BF16)
    lo = (x - hi.astype(_F32)).astype(_BF16)
    return hi, lo


def _router_kernel(x_ref, w_ref, meta_ref, count_ref, carry_ref, *, tm, n_experts):
    i = pl.program_id(0)

    @pl.when(i == 0)
    def _():
        carry_ref[...] = jnp.zeros_like(carry_ref)

    xh, xl = _split_bf16(x_ref[...])
    wh, wl = _split_bf16(w_ref[...])
    logits = (jnp.dot(xh, wh, preferred_element_type=_F32)
              + (jnp.dot(xh, wl, preferred_element_type=_F32)
                 + jnp.dot(xl, wh, preferred_element_type=_F32)))
    lane = lax.broadcasted_iota(jnp.int32, (tm, LANES), 1)
    lg = jnp.where(lane < n_experts, logits, -jnp.inf)
    m1 = jnp.max(lg, axis=1, keepdims=True)
    i1 = jnp.min(jnp.where(lg == m1, lane, LANES), axis=1, keepdims=True)
    lg2 = jnp.where(lane == i1, -jnp.inf, lg)
    m2 = jnp.max(lg2, axis=1, keepdims=True)
    i2 = jnp.min(jnp.where(lg2 == m2, lane, LANES), axis=1, keepdims=True)
    e = jnp.exp(m2 - m1)
    g1 = 1.0 / (1.0 + e)
    g2 = e / (1.0 + e)

    sel1 = lane == i1
    sel2 = lane == i2
    cnt = jnp.where(sel1, 1.0, 0.0) + jnp.where(sel2, 1.0, 0.0)
    row = lax.broadcasted_iota(jnp.int32, (tm, tm), 0)
    col = lax.broadcasted_iota(jnp.int32, (tm, tm), 1)
    lower = jnp.where(col < row, 1.0, 0.0).astype(_BF16)
    before = jnp.dot(lower, cnt.astype(_BF16), preferred_element_type=_F32) + carry_ref[0:1, :]
    r1 = jnp.sum(jnp.where(sel1, before, 0.0), axis=1, keepdims=True)
    r2 = jnp.sum(jnp.where(sel2, before, 0.0), axis=1, keepdims=True)
    carry_ref[0:1, :] = carry_ref[0:1, :] + jnp.sum(cnt, axis=0, keepdims=True)
    count_ref[...] = carry_ref[...]

    meta = jnp.zeros((tm, LANES), _F32)
    for k, v in enumerate((i1.astype(_F32), i2.astype(_F32), r1, r2, g1, g2)):
        meta = jnp.where(lane == k, v, meta)
    meta_ref[...] = meta


def _router(x32, w_router):
    n, d = x32.shape
    n_experts = w_router.shape[1]
    tm = _tile(n, 256)
    w_pad = jnp.pad(w_router, ((0, 0), (0, LANES - n_experts)))
    blk = _nbytes((tm, d), _F32) + _nbytes((d, LANES), _F32) + 2 * _nbytes((tm, LANES), _F32)
    return pl.pallas_call(
        functools.partial(_router_kernel, tm=tm, n_experts=n_experts),
        grid=(n // tm,),
        in_specs=[pl.BlockSpec((tm, d), lambda i: (i, 0)),
                  pl.BlockSpec((d, LANES), lambda i: (0, 0))],
        out_specs=[pl.BlockSpec((tm, LANES), lambda i: (i, 0)),
                   pl.BlockSpec((8, LANES), lambda i: (0, 0))],
        out_shape=[jax.ShapeDtypeStruct((n, LANES), _F32), jax.ShapeDtypeStruct((8, LANES), _F32)],
        scratch_shapes=[pltpu.VMEM((8, LANES), _F32)],
        compiler_params=_params(("arbitrary",), blk, 4 * _nbytes((tm, d), _F32)),
        name="moe_router",
    )(x32, w_pad)


def _row_copy(src_hbm, row, dst_vmem, r, sem):
    return pltpu.make_async_copy(src_hbm.at[pl.ds(row, 1), :], dst_vmem.at[pl.ds(r, 1), :], sem)


def _gather_rows(src_hbm, idx_ref, base, dst_vmem, sem, n):
    def issue(i, c):
        for u in range(2):
            r = 2 * i + u
            _row_copy(src_hbm, idx_ref[base + r], dst_vmem, r, sem).start(priority=u)
        return c
    lax.fori_loop(0, n // 2, issue, 0, unroll=2)


def _wait_rows(src_hbm, dst_vmem, sem, n):
    def drain(r, c):
        _row_copy(src_hbm, 0, dst_vmem, r, sem).wait()
        return c
    lax.fori_loop(0, n, drain, 0, unroll=8)


def _moe_gather_kernel(tok_ref, nu_ref, x_hbm, xs_hbm, in_ref, out_ref, in_sem, out_sem, *, tg):
    n_used = nu_ref[0]

    def out_copy(b, slot):
        return pltpu.make_async_copy(out_ref.at[slot], xs_hbm.at[pl.ds(pl.multiple_of(b * tg, tg), tg), :],
                                     out_sem.at[slot])

    _gather_rows(x_hbm, tok_ref, 0, in_ref.at[0], in_sem.at[0], tg)

    def body(b, c):
        slot = b % 2

        @pl.when(b + 1 < n_used)
        def _():
            _gather_rows(x_hbm, tok_ref, (b + 1) * tg, in_ref.at[1 - slot], in_sem.at[1 - slot], tg)

        _wait_rows(x_hbm, in_ref.at[slot], in_sem.at[slot], tg)

        @pl.when(b >= 2)
        def _():
            out_copy(b - 2, slot).wait()

        out_ref[slot] = in_ref[slot].astype(out_ref.dtype)
        out_copy(b, slot).start()
        return c

    lax.fori_loop(0, n_used, body, 0)

    @pl.when(n_used >= 2)
    def _():
        out_copy(n_used - 2, n_used % 2).wait()

    out_copy(n_used - 1, (n_used - 1) % 2).wait()

    n_blocks = xs_hbm.shape[0] // tg
    out_ref[0] = jnp.zeros(out_ref.shape[1:], out_ref.dtype)

    def start_zero(b, c):
        out_copy(b, 0).start()
        return c

    def wait_zero(b, c):
        out_copy(b, 0).wait()
        return c

    lax.fori_loop(n_used, n_blocks, start_zero, 0)
    lax.fori_loop(n_used, n_blocks, wait_zero, 0)


def _moe_gather(x32, row_tok, n_used):
    rows = row_tok.shape[0]
    d = x32.shape[1]
    tg = MOE_BLOCK
    return pl.pallas_call(
        functools.partial(_moe_gather_kernel, tg=tg),
        grid_spec=pltpu.PrefetchScalarGridSpec(
            num_scalar_prefetch=2,
            grid=(1,),
            in_specs=[pl.BlockSpec(memory_space=pl.ANY)],
            out_specs=pl.BlockSpec(memory_space=pl.ANY),
            scratch_shapes=[pltpu.VMEM((2, tg, d), _F32), pltpu.VMEM((2, tg, d), _BF16),
                            pltpu.SemaphoreType.DMA((2,)), pltpu.SemaphoreType.DMA((2,))]),
        out_shape=jax.ShapeDtypeStruct((rows, d), _BF16),
        compiler_params=_params(("arbitrary",), 0, 2 * _nbytes((tg, d), _F32) + 2 * _nbytes((tg, d), _BF16)),
        name="moe_gather",
    )(row_tok, n_used, x32)


def _grouped_rows(gs_ref, gb_ref, e, src_hbm, dst_hbm, in_ref, out_ref, in_sem, out_sem, col, compute, tm, big):
    assert big in (1, 2, 4)
    r0 = gs_ref[e]
    nblk = gb_ref[e]
    n_main = nblk // big
    tails = [big >> k for k in range(1, big.bit_length())]

    def rows(off, nb):
        return pl.ds(pl.multiple_of(r0 + off * tm, tm), nb * tm)

    def in_copy(off, nb, slot):
        return pltpu.make_async_copy(src_hbm.at[rows(off, nb), :], in_ref.at[slot, pl.ds(0, nb * tm), :],
                                     in_sem.at[slot])

    def out_copy(off, nb, slot):
        return pltpu.make_async_copy(out_ref.at[slot, pl.ds(0, nb * tm), :], dst_hbm.at[rows(off, nb), col],
                                     out_sem.at[slot])

    def tail_off(t):
        return (nblk // (2 * t)) * (2 * t)

    def start_first_tail(first, slot, enable):
        pending = enable
        for t in tails[first:]:
            present = (nblk & t) != 0

            @pl.when(jnp.logical_and(pending, present))
            def _():
                in_copy(tail_off(t), t, slot).start(priority=ROW_DMA_PRIORITY)

            pending = jnp.logical_and(pending, jnp.logical_not(present))

    @pl.when(n_main > 0)
    def _():
        in_copy(0, big, 0).start(priority=ROW_DMA_PRIORITY)

    start_first_tail(0, 0, n_main == 0)

    def body(b, c):
        slot = b % 2
        in_copy(b * big, big, slot).wait()

        @pl.when(b + 1 < n_main)
        def _():
            in_copy((b + 1) * big, big, 1 - slot).start(priority=ROW_DMA_PRIORITY)

        start_first_tail(0, 1 - slot, b + 1 == n_main)

        @pl.when(b >= 2)
        def _():
            out_copy((b - 2) * big, big, slot).wait()

        out_ref[slot] = compute(in_ref[slot])
        out_copy(b * big, big, slot).start()
        return c

    lax.fori_loop(0, n_main, body, 0)

    @pl.when(n_main >= 2)
    def _():
        out_copy((n_main - 2) * big, big, n_main % 2).wait()

    @pl.when(n_main >= 1)
    def _():
        out_copy((n_main - 1) * big, big, (n_main - 1) % 2).wait()

    slot = n_main % 2
    tail_slots = []
    for k, t in enumerate(tails):
        present = (nblk & t) != 0
        tail_slots.append(slot)
        cur = slot

        @pl.when(present)
        def _():
            in_copy(tail_off(t), t, cur).wait()
            start_first_tail(k + 1, 1 - cur, True)
            out_ref[cur, 0:t * tm, :] = compute(in_ref[cur, 0:t * tm, :])
            out_copy(tail_off(t), t, cur).start()

        slot = jnp.where(present, 1 - slot, slot)

    for k, t in enumerate(tails):
        @pl.when((nblk & t) != 0)
        def _():
            out_copy(tail_off(t), t, tail_slots[k]).wait()


def _grouped_rows_ring(gs_ref, gb_ref, e, src_hbm, dst_hbm, in_ref, out_ref, in_sem, out_sem, col, compute, tm):
    r0 = gs_ref[e]
    nblk = gb_ref[e]

    def rows(b):
        return pl.ds(pl.multiple_of(r0 + b * tm, tm), tm)

    def in_copy(b, slot):
        return pltpu.make_async_copy(src_hbm.at[rows(b), :], in_ref.at[slot], in_sem.at[slot])

    def out_copy(b, slot):
        return pltpu.make_async_copy(out_ref.at[slot], dst_hbm.at[rows(b), col], out_sem.at[slot])

    for first in range(2):
        @pl.when(nblk > first)
        def _():
            in_copy(first, first).start()

    def body(b, c):
        slot = b % 3
        oslot = b % 2
        in_copy(b, slot).wait()

        @pl.when(b + 2 < nblk)
        def _():
            in_copy(b + 2, (b + 2) % 3).start()

        @pl.when(b >= 2)
        def _():
            out_copy(b - 2, oslot).wait()

        out_ref[oslot] = compute(in_ref[slot])
        out_copy(b, oslot).start()
        return c

    lax.fori_loop(0, nblk, body, 0)

    @pl.when(nblk >= 2)
    def _():
        out_copy(nblk - 2, nblk % 2).wait()

    @pl.when(nblk >= 1)
    def _():
        out_copy(nblk - 1, (nblk - 1) % 2).wait()


def _zero_rows(gs_ref, gb_ref, e, dst_hbm, out_ref, out_sem, col, tm):
    r0 = gs_ref[e]
    out_ref[0, 0:tm, :] = jnp.zeros((tm, out_ref.shape[2]), out_ref.dtype)

    def copy(b):
        return pltpu.make_async_copy(out_ref.at[0, pl.ds(0, tm), :],
                                     dst_hbm.at[pl.ds(pl.multiple_of(r0 + b * tm, tm), tm), col], out_sem.at[0])

    def start(b, c):
        copy(b).start()
        return c

    def wait(b, c):
        copy(b).wait()
        return c

    lax.fori_loop(0, gb_ref[e], start, 0)
    lax.fori_loop(0, gb_ref[e], wait, 0)


def _moe_gate_up_kernel(gs_ref, gb_ref, xs_hbm, wg_ref, wu_ref, h_hbm, wg16_ref, wu16_ref,
                        in_ref, out_ref, in_sem, out_sem, *, tm, tf, n_experts):
    j = pl.program_id(0)
    e = pl.program_id(1)
    col = pl.ds(pl.multiple_of(j * tf, tf), tf)

    def compute(x):
        g = jnp.dot(x, wg16_ref[...], preferred_element_type=_F32)
        u = jnp.dot(x, wu16_ref[...], preferred_element_type=_F32)
        return _swiglu(g, u).astype(_BF16)

    @pl.when(e < n_experts)
    def _():
        wg16_ref[...] = wg_ref[...].astype(_BF16)
        wu16_ref[...] = wu_ref[...].astype(_BF16)
        _grouped_rows(gs_ref, gb_ref, e, xs_hbm, h_hbm, in_ref, out_ref, in_sem, out_sem, col, compute, tm,
                      in_ref.shape[1] // tm)

    @pl.when(e == n_experts)
    def _():
        _zero_rows(gs_ref, gb_ref, e, h_hbm, out_ref, out_sem, col, tm)


def _moe_gate_up(xs16, w_gu, gstart, gblocks):
    rows, d = xs16.shape
    n_experts = w_gu.shape[0]
    f = w_gu.shape[2] // 2
    tm = MOE_BLOCK
    tf = _tile(f, 1024)
    nf = f // tf
    blk = 2 * _nbytes((d, tf), _F32)
    tc = MOE_CHUNK_BLOCKS * tm
    scratch_bytes = (2 * _nbytes((d, tf), _BF16) + 2 * _nbytes((tc, d), _BF16) + 2 * _nbytes((tc, tf), _BF16)
                     + 3 * _nbytes((tc, tf), _F32))
    last = n_experts - 1
    return pl.pallas_call(
        functools.partial(_moe_gate_up_kernel, tm=tm, tf=tf, n_experts=n_experts),
        grid_spec=pltpu.PrefetchScalarGridSpec(
            num_scalar_prefetch=2,
            grid=(nf, n_experts + 1),
            in_specs=[pl.BlockSpec(memory_space=pl.ANY),
                      pl.BlockSpec((None, d, tf), lambda j, e, gs, gb: (jnp.minimum(e, last), 0, j)),
                      pl.BlockSpec((None, d, tf), lambda j, e, gs, gb: (jnp.minimum(e, last), 0, nf + j))],
            out_specs=pl.BlockSpec(memory_space=pl.ANY),
            scratch_shapes=[pltpu.VMEM((d, tf), _BF16), pltpu.VMEM((d, tf), _BF16),
                            pltpu.VMEM((2, tc, d), _BF16), pltpu.VMEM((2, tc, tf), _BF16),
                            pltpu.SemaphoreType.DMA((2,)), pltpu.SemaphoreType.DMA((2,))]),
        out_shape=jax.ShapeDtypeStruct((rows, f), _BF16),
        compiler_params=_params(("arbitrary", "arbitrary"), blk, scratch_bytes),
        name="moe_gate_up",
    )(gstart, gblocks, xs16, w_gu, w_gu)


def _moe_down_kernel(gs_ref, gb_ref, h_hbm, w_ref, y_hbm, w16_ref, in_ref, out_ref, in_sem, out_sem,
                     *, tm, tn, n_experts):
    j = pl.program_id(0)
    e = pl.program_id(1)
    col = pl.ds(pl.multiple_of(j * tn, tn), tn)

    def compute(h):
        return jnp.dot(h, w16_ref[...], preferred_element_type=_F32)

    @pl.when(e < n_experts)
    def _():
        w16_ref[...] = w_ref[...].astype(_BF16)
        _grouped_rows_ring(gs_ref, gb_ref, e, h_hbm, y_hbm, in_ref, out_ref, in_sem, out_sem, col, compute, tm)

    @pl.when(e == n_experts)
    def _():
        _zero_rows(gs_ref, gb_ref, e, y_hbm, out_ref, out_sem, col, tm)


def _moe_down(h16, w_down, gstart, gblocks):
    rows, f = h16.shape
    n_experts = w_down.shape[0]
    d = w_down.shape[2]
    tm = MOE_BLOCK
    tn = _tile(d, 512)
    tc = tm
    blk = _nbytes((f, tn), _F32)
    scratch_bytes = (_nbytes((f, tn), _BF16) + 3 * _nbytes((tc, f), _BF16) + 3 * _nbytes((tc, tn), _F32))
    last = n_experts - 1
    return pl.pallas_call(
        functools.partial(_moe_down_kernel, tm=tm, tn=tn, n_experts=n_experts),
        grid_spec=pltpu.PrefetchScalarGridSpec(
            num_scalar_prefetch=2,
            grid=(d // tn, n_experts + 1),
            in_specs=[pl.BlockSpec(memory_space=pl.ANY),
                      pl.BlockSpec((None, f, tn), lambda j, e, gs, gb: (jnp.minimum(e, last), 0, j))],
            out_specs=pl.BlockSpec(memory_space=pl.ANY),
            scratch_shapes=[pltpu.VMEM((f, tn), _BF16),
                            pltpu.VMEM((3, tc, f), _BF16), pltpu.VMEM((2, tc, tn), _F32),
                            pltpu.SemaphoreType.DMA((3,)), pltpu.SemaphoreType.DMA((2,))]),
        out_shape=jax.ShapeDtypeStruct((rows, d), _F32),
        compiler_params=_params(("arbitrary", "arbitrary"), blk, scratch_bytes),
        name="moe_down",
    )(gstart, gblocks, h16, w_down)


def _moe_combine_kernel(d1_ref, d2_ref, y_hbm, meta_ref, res_ref, g_ref, b_ref, *rest, tc, alpha, n_split):
    outs, (buf_ref, sem) = rest[:-2], rest[-2:]
    i = pl.program_id(0)
    slot = i % 2

    def fetch(step, s):
        _gather_rows(y_hbm, d1_ref, step * tc, buf_ref.at[s, 0], sem.at[s], tc)
        _gather_rows(y_hbm, d2_ref, step * tc, buf_ref.at[s, 1], sem.at[s], tc)

    @pl.when(i == 0)
    def _():
        fetch(0, 0)

    @pl.when(i + 1 < pl.num_programs(0))
    def _():
        fetch(i + 1, 1 - slot)

    _wait_rows(y_hbm, buf_ref.at[slot, 0], sem.at[slot], tc)
    _wait_rows(y_hbm, buf_ref.at[slot, 1], sem.at[slot], tc)
    meta = meta_ref[...]
    f = buf_ref[slot, 0] * meta[:, 4:5] + buf_ref[slot, 1] * meta[:, 5:6]
    y = _res_ln(f, res_ref[...], g_ref[...], b_ref[...], alpha)
    if n_split is None:
        outs[0][...] = y
        outs[1][...] = y.astype(_BF16)
    else:
        @pl.when(i < n_split)
        def _():
            outs[0][...] = y

        @pl.when(i >= n_split)
        def _():
            outs[1][...] = y


def _moe_combine_ln(y_rows, dest1, dest2, meta, res, g, b, alpha, split_rows):
    n, d = res.shape
    tc = _tile(n if split_rows is None else math.gcd(n, split_rows), 128)
    blk = _nbytes((tc, LANES), _F32) + 2 * _nbytes((tc, d), _F32) + _nbytes((tc, d), _BF16)
    tile_spec = pl.BlockSpec((tc, d), lambda i, a, c: (i, 0))
    if split_rows is None:
        n_split = None
        out_specs = [tile_spec, tile_spec]
        out_shape = [jax.ShapeDtypeStruct((n, d), _F32), jax.ShapeDtypeStruct((n, d), _BF16)]
    else:
        n_split = split_rows // tc
        out_specs = [pl.BlockSpec((tc, d), lambda i, a, c: (jnp.minimum(i, n_split - 1), 0)),
                     pl.BlockSpec((tc, d), lambda i, a, c: (jnp.maximum(i - n_split, 0), 0))]
        out_shape = [jax.ShapeDtypeStruct((split_rows, d), _F32),
                     jax.ShapeDtypeStruct((n - split_rows, d), _F32)]
    return pl.pallas_call(
        functools.partial(_moe_combine_kernel, tc=tc, alpha=alpha, n_split=n_split),
        grid_spec=pltpu.PrefetchScalarGridSpec(
            num_scalar_prefetch=2,
            grid=(n // tc,),
            in_specs=[pl.BlockSpec(memory_space=pl.ANY),
                      pl.BlockSpec((tc, LANES), lambda i, a, c: (i, 0)),
                      tile_spec,
                      pl.BlockSpec((1, d), lambda i, a, c: (0, 0)),
                      pl.BlockSpec((1, d), lambda i, a, c: (0, 0))],
            out_specs=out_specs,
            scratch_shapes=[pltpu.VMEM((2, 2, tc, d), _F32), pltpu.SemaphoreType.DMA((2,))]),
        out_shape=out_shape,
        compiler_params=_params(("arbitrary",), blk, 8 * _nbytes((tc, d), _F32)),
        name="moe_combine_ln",
    )(dest1, dest2, y_rows, meta, res, g.reshape(1, d), b.reshape(1, d))


def _moe_layer(x32, w_router, w_gu, w_down, g, b, alpha, split_rows):
    n, d = x32.shape
    n_experts = w_router.shape[1]
    meta, counts = _router(x32, w_router)
    e1 = meta[:, 0].astype(jnp.int32)
    e2 = meta[:, 1].astype(jnp.int32)
    counts = counts[0, :n_experts].astype(jnp.int32)
    gblocks = (counts + MOE_BLOCK - 1) // MOE_BLOCK
    gend = jnp.cumsum(gblocks) * MOE_BLOCK
    gstart = gend - gblocks * MOE_BLOCK
    dest1 = gstart[e1] + meta[:, 2].astype(jnp.int32)
    dest2 = gstart[e2] + meta[:, 3].astype(jnp.int32)
    nb = (n * TOP_K) // MOE_BLOCK + n_experts
    rows = nb * MOE_BLOCK
    tok = jnp.arange(n, dtype=jnp.int32)
    row_tok = jnp.zeros((rows,), jnp.int32).at[dest1].set(tok).at[dest2].set(tok)
    n_used = (gend[-1] // MOE_BLOCK).astype(jnp.int32).reshape(1)
    gstart = jnp.concatenate([gstart, gend[-1:]]).astype(jnp.int32)
    gblocks = jnp.concatenate([gblocks, nb - n_used]).astype(jnp.int32)

    xs16 = _moe_gather(x32, row_tok, n_used)
    h16 = _moe_gate_up(xs16, w_gu, gstart, gblocks)
    y_rows = _moe_down(h16, w_down, gstart, gblocks)
    return _moe_combine_ln(y_rows, dest1, dest2, meta, x32, g, b, alpha, split_rows)


def kernel(x_prompt, x_sample, cache_k, cache_v, state_pool, attn_w_in, attn_w_o, attn_lambda, attn_subln_g, pool_w_in, pool_w_grp, pool_scale, pool_w_o, ln_mix_g, ln_mix_b, ln_ffn_g, ln_ffn_b, ffn_w_gu, ffn_w_down, moe_w_router, moe_w_gu, moe_w_down):
    batch, seq, d = x_prompt.shape
    n_seq, t, _ = x_sample.shape
    past = cache_k.shape[2]
    depth = ln_mix_g.shape[0]
    alpha = (2 * depth) ** 0.25
    n_p = batch * seq
    n_s = n_seq * t
    n = n_p + n_s
    n_heads = cache_k.shape[3]
    qk_width = n_heads * 2 * HEAD_DIM
    v_width = n_heads * V_DIM

    x32 = jnp.concatenate([x_prompt.reshape(n_p, d), x_sample.reshape(n_s, d)], axis=0)
    x16 = x32.astype(_BF16)
    tab_p = _rope_tables(np.arange(seq))
    tab_s = _rope_tables(np.tile(past + np.arange(t), n_seq))
    past_k = cache_k.reshape(-1, past * n_heads, 2 * HEAD_DIM)
    past_v = cache_v.reshape(-1, past * n_heads, V_DIM)
    ctx = jnp.pad(state_pool, ((0, 0), (0, 0), (HALO - POOL_CTX, 0), (0, 0))).reshape(-1, HALO, d)

    kp_l, vp_l, ks_l, vs_l, pp_l, ps_l = [], [], [], [], [], []
    y_p = y_s = None
    for i in range(depth):
        j = i // 2
        last = i == depth - 1
        if i % 2 == 0:
            lam_init = 0.8 - 0.6 * math.exp(-0.3 * i)
            w_in16 = attn_w_in[j].astype(_BF16)
            g = attn_subln_g[j].reshape(1, V_DIM)
            rows_p, rows_s = (0, n_p, tab_p), (n_p, n_s, tab_s)
            (q_p,), (q_s,) = (_proj(x16, w_in16, 0, qk_width, r0, nr, tab, False, True, "proj_q", Q_SCALE)
                              for r0, nr, tab in (rows_p, rows_s))
            (k32_p, k_p), (k32_s, k_s) = (_proj(x16, w_in16, qk_width, qk_width, r0, nr, tab, True, True, "proj_k")
                                          for r0, nr, tab in (rows_p, rows_s))
            (v32_p, v_p), (v32_s, v_s) = (_proj(x16, w_in16, 2 * qk_width, v_width, r0, nr, None, True, True, "proj_v")
                                          for r0, nr, tab in (rows_p, rows_s))
            mix = [_attn_prompt(q_p, k_p, v_p, attn_lambda[j], g, batch, seq, lam_init),
                   _attn_sample(q_s, k_s, v_s, past_k, past_v, j, attn_lambda[j], g, n_seq, t, lam_init)]
            w_o16 = attn_w_o[j].astype(_BF16)
            kp_l.append(k32_p.reshape(batch, seq, n_heads, 2 * HEAD_DIM))
            vp_l.append(v32_p.reshape(batch, seq, n_heads, V_DIM))
            ks_l.append(k32_s.reshape(n_seq, t, n_heads, 2 * HEAD_DIM))
            vs_l.append(v32_s.reshape(n_seq, t, n_heads, V_DIM))
        else:
            u32, = _proj(x16, pool_w_in[j].astype(_BF16), 0, d, 0, n, None, True, False, "pool_in")
            w_grp16 = pool_w_grp[j].astype(_BF16)
            mix = [_pool_prompt(u32, w_grp16, pool_scale[j], batch, seq),
                   _pool_sample(u32, ctx, j, w_grp16, pool_scale[j], n_p, n_seq, t, past)]
            w_o16 = pool_w_o[j].astype(_BF16)
            u_p = u32[:n_p].reshape(batch, seq, d)
            u_s = u32[n_p:].reshape(n_seq, t, d)
            pp_l.append(jnp.concatenate([jnp.zeros((batch, POOL_CTX, d), _F32), u_p[:, -POOL_CTX:]], 1)[:, -POOL_CTX:])
            ps_l.append(jnp.concatenate([state_pool[j], u_s[:, -POOL_CTX:]], 1)[:, -POOL_CTX:])
        x32, x16 = _mm_res_ln(mix, w_o16, x32, ln_mix_g[i], ln_mix_b[i], alpha, "mix_out_ln")
        if i % 2 == 0:
            h16 = _gate_up(x16, ffn_w_gu[j].astype(_BF16))
            x32, x16 = _mm_res_ln([h16], ffn_w_down[j].astype(_BF16), x32, ln_ffn_g[i], ln_ffn_b[i],
                                  alpha, "ffn_down_ln")
        elif last:
            y_p, y_s = _moe_layer(x32, moe_w_router[j], moe_w_gu[j], moe_w_down[j],
                                  ln_ffn_g[i], ln_ffn_b[i], alpha, n_p)
        else:
            x32, x16 = _moe_layer(x32, moe_w_router[j], moe_w_gu[j], moe_w_down[j],
                                  ln_ffn_g[i], ln_ffn_b[i], alpha, None)
    if y_p is None:
        y_p, y_s = x32[:n_p], x32[n_p:]
    return (y_p.reshape(batch, seq, d), y_s.reshape(n_seq, t, d), jnp.stack(kp_l), jnp.stack(vp_l),
            jnp.stack(pp_l), jnp.stack(ks_l), jnp.stack(vs_l), jnp.stack(ps_l))
```

```python
import functools
import math

import numpy as np
import jax
import jax.numpy as jnp
from jax import lax
from jax.experimental import pallas as pl
from jax.experimental.pallas import tpu as pltpu

CHUNK = 64
HEAD_DIM = 64
V_DIM = 2 * HEAD_DIM
ROT_DIM = HEAD_DIM // 4
ROPE_THETA = 500000.0
POOL_WINDOWS = (2, 4, 8, 16)
POOL_CTX = max(POOL_WINDOWS) - 1
TOP_K = 2
LN_EPS = 1e-5

LANES = 128
V7X_VMEM_REQUEST_CAP = 56 * 1024 * 1024
COMPILER_SCRATCH_ALLOWANCE = 6 * 1024 * 1024

HALO = 16
MOE_BLOCK = 256
MOE_CHUNK_BLOCKS = 2
ROW_DMA_PRIORITY = 1
Q_SCALE = HEAD_DIM ** -0.5 * math.log2(math.e)

_F32 = jnp.float32
_BF16 = jnp.bfloat16
_NT = (((1,), (1,)), ((), ()))


def _nbytes(shape, dtype):
    return math.prod(shape) * jnp.dtype(dtype).itemsize


def _params(semantics, pipelined_bytes, scratch_bytes=0):
    need = 2 * pipelined_bytes + scratch_bytes + COMPILER_SCRATCH_ALLOWANCE
    return pltpu.CompilerParams(dimension_semantics=semantics,
                                vmem_limit_bytes=min(need, V7X_VMEM_REQUEST_CAP))


def _tile(n, pref):
    t = min(n, pref)
    while n % t:
        t //= 2
    return t


def _rope(acc, cos, sin_lo, sin_hi):
    half = ROT_DIM // 2
    pieces = []
    for g in range(acc.shape[1] // LANES):
        xg = acc[:, g * LANES:(g + 1) * LANES]
        pieces.append(xg * cos
                      + pltpu.roll(xg, half, 1) * sin_hi
                      + pltpu.roll(xg, LANES - half, 1) * sin_lo)
    return pieces[0] if len(pieces) == 1 else jnp.concatenate(pieces, axis=1)


def _proj_kernel(*refs, rope, want32, want16, scale):
    x_ref, w_ref = refs[0], refs[1]
    acc = jnp.dot(x_ref[...], w_ref[...], preferred_element_type=_F32)
    pos = 2
    if rope:
        acc = _rope(acc, refs[2][...], refs[3][...], refs[4][...])
        pos = 5
    if scale is not None:
        acc = acc * scale
    if want32:
        refs[pos][...] = acc
        pos += 1
    if want16:
        refs[pos][...] = acc.astype(_BF16)


def _proj(xb, w, col0, ncols, row0, nrows, tables, want32, want16, name, scale=None):
    k = xb.shape[1]
    period = nrows if tables is None else tables[0].shape[0]
    tm = _tile(math.gcd(math.gcd(row0, nrows), period) if row0 else math.gcd(nrows, period), 1024)
    tn = _tile(ncols, 512)
    off = col0 // tn
    rb0 = row0 // tm
    tper = period // tm
    in_specs = [pl.BlockSpec((tm, k), lambda i, j: (rb0 + i, 0)),
                pl.BlockSpec((k, tn), lambda i, j: (0, j + off))]
    args = [xb, w]
    blk = _nbytes((tm, k), _BF16) + _nbytes((k, tn), _BF16)
    if tables is not None:
        for t in tables:
            in_specs.append(pl.BlockSpec((tm, LANES), lambda i, j: (i % tper, 0)))
            args.append(t)
            blk += _nbytes((tm, LANES), _F32)
    out_shape, out_specs = [], []
    for want, dt in ((want32, _F32), (want16, _BF16)):
        if want:
            out_shape.append(jax.ShapeDtypeStruct((nrows, ncols), dt))
            out_specs.append(pl.BlockSpec((tm, tn), lambda i, j: (i, j)))
            blk += _nbytes((tm, tn), dt)
    return pl.pallas_call(
        functools.partial(_proj_kernel, rope=tables is not None, want32=want32, want16=want16,
                          scale=scale),
        grid=(nrows // tm, ncols // tn),
        in_specs=in_specs, out_specs=out_specs, out_shape=out_shape,
        compiler_params=_params(("parallel", "parallel"), blk, _nbytes((tm, tn), _F32)),
        name=name,
    )(*args)


def _rope_tables(pos):
    half = ROT_DIM // 2
    inv_freq = ROPE_THETA ** (-np.arange(0, ROT_DIM, 2, dtype=np.float64) / ROT_DIM)
    ang = np.asarray(pos, np.float64)[:, None] * inv_freq[None, :]
    cos, sin = np.cos(ang), np.sin(ang)
    t = ang.shape[0]
    ones = np.ones((t, HEAD_DIM - ROT_DIM))
    zeros = np.zeros((t, HEAD_DIM - ROT_DIM))
    zh = np.zeros((t, half))
    cos_map = np.concatenate([cos, cos, ones], 1)
    lo_map = np.concatenate([-sin, zh, zeros], 1)
    hi_map = np.concatenate([zh, sin, zeros], 1)
    return tuple(jnp.asarray(np.concatenate([a, a], 1), _F32) for a in (cos_map, lo_map, hi_map))


def _stack_maps(q):
    lane = lax.broadcasted_iota(jnp.int32, q.shape, 1)
    zero = jnp.zeros_like(q)
    return jnp.concatenate([jnp.where(lane < HEAD_DIM, q, zero),
                            jnp.where(lane >= HEAD_DIM, q, zero)], axis=0)


def _diff_lambda(lam_ref, lam_init):
    lf = lam_ref[...]
    a = jnp.sum(lf[0:1] * lf[1:2], axis=1, keepdims=True)
    b = jnp.sum(lf[2:3] * lf[3:4], axis=1, keepdims=True)
    return jnp.exp(a) - jnp.exp(b) + lam_init


def _diff_finish(acc, l, tq, lam, g, lam_init):
    o = acc / l
    o = o[:tq] - lam * o[tq:]
    ms = jnp.mean(o * o, axis=1, keepdims=True)
    return (o * lax.rsqrt(ms + LN_EPS) * g) * (1.0 - lam_init)


def _attn_prompt_kernel(q_ref, k_ref, v_ref, lam_ref, g_ref, o_ref, sa_ref, sb_ref, *, tq, tk, lam_init):
    q0 = pl.program_id(2) * tq
    qs = _stack_maps(q_ref[...])
    n_full = q0 // tk

    def scores(kb, dst_ref):
        k0 = pl.multiple_of(kb * tk, tk)
        dst_ref[...] = lax.dot_general(qs, k_ref[pl.ds(k0, tk), :], _NT, preferred_element_type=_F32)

    def consume(kb, src_ref, carry, masked):
        m, l, acc = carry
        k0 = pl.multiple_of(kb * tk, tk)
        s = src_ref[...]
        if masked:
            qc = (q0 + lax.broadcasted_iota(jnp.int32, (tq, tk), 0)) // CHUNK
            kc = (k0 + lax.broadcasted_iota(jnp.int32, (tq, tk), 1)) // CHUNK
            vis = kc <= qc
            s = jnp.where(jnp.concatenate([vis, vis], axis=0), s, -jnp.inf)
        m_new = jnp.maximum(m, jnp.max(s, axis=1, keepdims=True))
        p = jnp.exp2(s - m_new)
        alpha = jnp.exp2(m - m_new)
        l = alpha * l + jnp.sum(p, axis=1, keepdims=True)
        acc = alpha * acc + jnp.dot(p.astype(_BF16), v_ref[pl.ds(k0, tk), :], preferred_element_type=_F32)
        return m_new, l, acc

    def pair(p, carry):
        kb = 2 * p
        scores(kb + 1, sb_ref)
        carry = consume(kb, sa_ref, carry, False)
        scores(kb + 2, sa_ref)
        return consume(kb + 1, sb_ref, carry, False)

    def odd_tail(carry):
        scores(n_full, sb_ref)
        carry = consume(n_full - 1, sa_ref, carry, False)
        return consume(n_full, sb_ref, carry, True)

    def even_tail(carry):
        return consume(n_full, sa_ref, carry, True)

    carry = (jnp.full((2 * tq, 1), -jnp.inf, _F32), jnp.zeros((2 * tq, 1), _F32),
             jnp.zeros((2 * tq, V_DIM), _F32))
    scores(0, sa_ref)
    carry = lax.fori_loop(0, n_full // 2, pair, carry)
    _, l, acc = lax.cond(n_full % 2 == 1, odd_tail, even_tail, carry)
    lam = _diff_lambda(lam_ref, lam_init)
    o_ref[...] = _diff_finish(acc, l, tq, lam, g_ref[...], lam_init).astype(o_ref.dtype)


def _attn_prompt(q16, k16, v16, lam, g, batch, seq, lam_init):
    n_heads = q16.shape[1] // LANES
    tq = _tile(seq, 256)
    tk = _tile(seq, 512)
    assert tk % tq == 0
    nq = seq // tq
    blk = (_nbytes((tq, LANES), _BF16) * 2 + 2 * _nbytes((seq, LANES), _BF16))
    return pl.pallas_call(
        functools.partial(_attn_prompt_kernel, tq=tq, tk=tk, lam_init=lam_init),
        grid=(batch, n_heads, nq),
        in_specs=[pl.BlockSpec((tq, LANES), lambda b, h, i: (b * nq + i, h)),
                  pl.BlockSpec((seq, LANES), lambda b, h, i: (b, h)),
                  pl.BlockSpec((seq, LANES), lambda b, h, i: (b, h)),
                  pl.BlockSpec(lam.shape, lambda b, h, i: (0, 0)),
                  pl.BlockSpec((1, V_DIM), lambda b, h, i: (0, 0))],
        out_specs=pl.BlockSpec((tq, LANES), lambda b, h, i: (b * nq + i, h)),
        out_shape=jax.ShapeDtypeStruct((batch * seq, n_heads * V_DIM), _BF16),
        scratch_shapes=[pltpu.VMEM((2 * tq, tk), _F32), pltpu.VMEM((2 * tq, tk), _F32)],
        compiler_params=_params(("parallel", "parallel", "parallel"), blk,
                                6 * _nbytes((2 * tq, tk), _F32)),
        name="attn_prompt",
    )(q16, k16, v16, lam, g)


def _attn_sample_kernel(q_ref, kn_ref, vn_ref, kp_hbm, vp_hbm, lam_ref, g_ref, o_ref, kbuf, vbuf, sem, *,
                        t, past, heads, lam_init, s0):
    seq = s0 + pl.program_id(0)

    def head_copies(h):
        slot = h % 2
        return (pltpu.make_async_copy(kp_hbm.at[seq, :, h, :], kbuf.at[slot], sem.at[0, slot]),
                pltpu.make_async_copy(vp_hbm.at[seq, :, h, :], vbuf.at[slot], sem.at[1, slot]))

    for c in head_copies(0):
        c.start()
    lam = _diff_lambda(lam_ref, lam_init)
    g = g_ref[...]
    qc = (past + lax.broadcasted_iota(jnp.int32, (t, t), 0)) // CHUNK
    kc = (past + lax.broadcasted_iota(jnp.int32, (t, t), 1)) // CHUNK
    vis = jnp.concatenate([kc <= qc, kc <= qc], axis=0)
    outs = []
    for h in range(heads):
        sl = slice(h * LANES, (h + 1) * LANES)
        qs = _stack_maps(q_ref[:, sl])
        if h + 1 < heads:
            for c in head_copies(h + 1):
                c.start()
        for c in head_copies(h):
            c.wait()
        kp = kbuf[h % 2].astype(_BF16)
        vp = vbuf[h % 2].astype(_BF16)
        s_p = lax.dot_general(qs, kp, _NT, preferred_element_type=_F32)
        s_n = lax.dot_general(qs, kn_ref[:, sl], _NT, preferred_element_type=_F32)
        s_n = jnp.where(vis, s_n, -jnp.inf)
        m = jnp.maximum(jnp.max(s_p, axis=1, keepdims=True), jnp.max(s_n, axis=1, keepdims=True))
        p_p = jnp.exp2(s_p - m)
        p_n = jnp.exp2(s_n - m)
        l = jnp.sum(p_p, axis=1, keepdims=True) + jnp.sum(p_n, axis=1, keepdims=True)
        acc = (jnp.dot(p_p.astype(_BF16), vp, preferred_element_type=_F32)
               + jnp.dot(p_n.astype(_BF16), vn_ref[:, sl], preferred_element_type=_F32))
        outs.append(_diff_finish(acc, l, t, lam, g, lam_init))
    o_ref[...] = jnp.concatenate(outs, axis=1).astype(o_ref.dtype)


def _attn_sample(q16, k16, v16, past_k, past_v, layer, lam, g, n_seq, t, lam_init):
    width = q16.shape[1]
    _, past, heads, _ = past_k.shape
    new_spec = pl.BlockSpec((t, width), lambda s: (s, 0))
    blk = 4 * _nbytes((t, width), _BF16)
    return pl.pallas_call(
        functools.partial(_attn_sample_kernel, t=t, past=past, heads=heads, lam_init=lam_init,
                          s0=layer * n_seq),
        grid=(n_seq,),
        in_specs=[new_spec, new_spec, new_spec,
                  pl.BlockSpec(memory_space=pl.ANY), pl.BlockSpec(memory_space=pl.ANY),
                  pl.BlockSpec(lam.shape, lambda s: (0, 0)),
                  pl.BlockSpec((1, V_DIM), lambda s: (0, 0))],
        out_specs=new_spec,
        out_shape=jax.ShapeDtypeStruct((n_seq * t, width), _BF16),
        scratch_shapes=[pltpu.VMEM((2, past, LANES), _F32), pltpu.VMEM((2, past, LANES), _F32),
                        pltpu.SemaphoreType.DMA((2, 2))],
        compiler_params=_params(("arbitrary",), blk,
                                4 * _nbytes((past, LANES), _F32) + 6 * _nbytes((2 * t, past), _F32)),
        name="attn_sample",
    )(q16, k16, v16, past_k, past_v, lam, g)


def _res_ln(acc, res, g, b, alpha):
    y = alpha * res + acc
    mu = jnp.mean(y, axis=1, keepdims=True)
    yc = y - mu
    var = jnp.mean(yc * yc, axis=1, keepdims=True)
    return yc * lax.rsqrt(var + LN_EPS) * g + b


def _mm_res_ln_kernel(*refs, nk, alpha, part_starts):
    n_parts = len(part_starts)
    a_refs = refs[:n_parts]
    w_ref, res_ref, g_ref, b_ref, o32_ref, o16_ref = refs[n_parts:n_parts + 6]
    scratch = refs[n_parts + 6:]

    def finish(acc):
        y = _res_ln(acc, res_ref[...], g_ref[...], b_ref[...], alpha)
        o32_ref[...] = y
        o16_ref[...] = y.astype(_BF16)

    a = a_refs[0][...]
    for p in range(1, n_parts):
        a = jnp.where(pl.program_id(0) >= part_starts[p], a_refs[p][...], a)
    part = jnp.dot(a, w_ref[...], preferred_element_type=_F32)
    if nk == 1:
        finish(part)
        return
    acc_ref, = scratch
    kk = pl.program_id(1)

    @pl.when(kk == 0)
    def _():
        acc_ref[...] = part

    @pl.when(kk > 0)
    def _():
        acc_ref[...] += part

    @pl.when(kk == nk - 1)
    def _():
        finish(acc_ref[...])


def _mm_res_ln(parts, w16, res, g, b, alpha, name):
    m = sum(a.shape[0] for a in parts)
    k = parts[0].shape[1]
    d = w16.shape[1]
    resident = _nbytes((k, d), _BF16) <= V7X_VMEM_REQUEST_CAP // 2
    tk = k if resident else _tile(k, 512)
    nk = k // tk
    tm = _tile(functools.reduce(math.gcd, [a.shape[0] for a in parts]), 512 if k <= 2048 else 256)
    blk = (len(parts) * _nbytes((tm, tk), _BF16) + 2 * _nbytes((tm, d), _F32) + _nbytes((tm, d), _BF16))
    w_bytes = _nbytes((tk, d), _BF16) * (1 if resident else 2)
    w_mode = dict(pipeline_mode=pl.Buffered(1)) if resident else {}
    scratch = [pltpu.VMEM((tm, d), _F32)] if nk > 1 else []
    part_starts, part_specs, start = [], [], 0
    for a in parts:
        nblk = a.shape[0] // tm
        part_starts.append(start)
        part_specs.append(pl.BlockSpec(
            (tm, tk), lambda i, kk, start=start, nblk=nblk: (jnp.clip(i - start, 0, nblk - 1), kk)))
        start += nblk
    return pl.pallas_call(
        functools.partial(_mm_res_ln_kernel, nk=nk, alpha=alpha, part_starts=tuple(part_starts)),
        grid=(m // tm, nk),
        in_specs=part_specs + [
                  pl.BlockSpec((tk, d), lambda i, kk: (kk, 0), **w_mode),
                  pl.BlockSpec((tm, d), lambda i, kk: (i, 0)),
                  pl.BlockSpec((1, d), lambda i, kk: (0, 0)),
                  pl.BlockSpec((1, d), lambda i, kk: (0, 0))],
        out_specs=[pl.BlockSpec((tm, d), lambda i, kk: (i, 0)),
                   pl.BlockSpec((tm, d), lambda i, kk: (i, 0))],
        out_shape=[jax.ShapeDtypeStruct((m, d), _F32), jax.ShapeDtypeStruct((m, d), _BF16)],
        scratch_shapes=scratch,
        compiler_params=_params(("parallel", "arbitrary"), blk, w_bytes + 2 * _nbytes((tm, d), _F32)),
        name=name,
    )(*parts, w16, res, g.reshape(1, d), b.reshape(1, d))


def _swiglu(g, u):
    return g * jax.nn.sigmoid(g) * u


def _gate_up_kernel(x_ref, wg_ref, wu_ref, h_ref):
    x = x_ref[...]
    g = jnp.dot(x, wg_ref[...], preferred_element_type=_F32)
    u = jnp.dot(x, wu_ref[...], preferred_element_type=_F32)
    h_ref[...] = _swiglu(g, u).astype(h_ref.dtype)


def _gate_up(x16, w_gu16):
    m, k = x16.shape
    f = w_gu16.shape[1] // 2
    tm = _tile(m, 1024)
    tf = _tile(f, 512)
    nf = f // tf
    blk = _nbytes((tm, k), _BF16) + 2 * _nbytes((k, tf), _BF16) + _nbytes((tm, tf), _BF16)
    return pl.pallas_call(
        _gate_up_kernel,
        grid=(m // tm, nf),
        in_specs=[pl.BlockSpec((tm, k), lambda i, j: (i, 0)),
                  pl.BlockSpec((k, tf), lambda i, j: (0, j)),
                  pl.BlockSpec((k, tf), lambda i, j: (0, nf + j))],
        out_specs=pl.BlockSpec((tm, tf), lambda i, j: (i, j)),
        out_shape=jax.ShapeDtypeStruct((m, f), _BF16),
        compiler_params=_params(("parallel", "parallel"), blk, 3 * _nbytes((tm, tf), _F32)),
        name="ffn_gate_up",
    )(x16, w_gu16, w_gu16)


def _pool_kernel(halo_ref, u_ref, w_ref, scale_ref, z_ref, ext_ref, *, tm, pos0, zero_first):
    i = pl.program_id(1)
    halo = halo_ref[...]
    if zero_first:
        halo = jnp.where(i == 0, jnp.zeros_like(halo), halo)
    ext_ref[0:HALO, :] = halo
    ext_ref[HALO:HALO + tm, :] = u_ref[...]
    pos = pos0 + i * tm + lax.broadcasted_iota(jnp.int32, (tm, 1), 0)
    gd = u_ref.shape[1] // len(POOL_WINDOWS)
    for g, w in enumerate(POOL_WINDOWS):
        cols = slice(g * gd, (g + 1) * gd)
        win = ext_ref[HALO:HALO + tm, cols]
        for back in range(1, w):
            win = win + ext_ref[HALO - back:HALO - back + tm, cols]
        count = jnp.minimum(w, pos + 1).astype(_F32)
        d = win / count - u_ref[:, cols]
        zg = jnp.dot(d.astype(_BF16), w_ref[g], preferred_element_type=_F32)
        z_ref[:, cols] = (zg * scale_ref[:, cols]).astype(z_ref.dtype)


def _pool_call(halo_arr, halo_spec, u, u_spec, w_grp16, scale, grid, tm, pos0, zero_first,
               out_rows, out_spec, name):
    d = u.shape[1]
    blk = (_nbytes((HALO, d), _F32) + _nbytes((tm, d), _F32) + _nbytes(w_grp16.shape, _BF16)
           + _nbytes((tm, d), _BF16))
    return pl.pallas_call(
        functools.partial(_pool_kernel, tm=tm, pos0=pos0, zero_first=zero_first),
        grid=grid,
        in_specs=[halo_spec, u_spec,
                  pl.BlockSpec(w_grp16.shape, lambda b, i: (0, 0, 0)),
                  pl.BlockSpec((1, d), lambda b, i: (0, 0))],
        out_specs=out_spec,
        out_shape=jax.ShapeDtypeStruct((out_rows, d), _BF16),
        scratch_shapes=[pltpu.VMEM((HALO + tm, d), _F32)],
        compiler_params=_params(("parallel", "arbitrary"), blk, 3 * _nbytes((HALO + tm, d), _F32)),
        name=name,
    )(halo_arr, u, w_grp16, scale.reshape(1, d))


def _pool_prompt(u, w_grp16, scale, batch, seq):
    d = u.shape[1]
    tm = _tile(seq, 256)
    nt = seq // tm
    per = tm // HALO
    halo_spec = pl.BlockSpec((HALO, d), lambda b, i: (jnp.maximum((b * nt + i) * per - 1, 0), 0))
    u_spec = pl.BlockSpec((tm, d), lambda b, i: (b * nt + i, 0))
    out_spec = pl.BlockSpec((tm, d), lambda b, i: (b * nt + i, 0))
    return _pool_call(u, halo_spec, u, u_spec, w_grp16, scale, (batch, nt), tm, 0, True,
                      batch * seq, out_spec, "pool_prompt")


def _pool_sample(u, ctx, layer, w_grp16, scale, row0, n_seq, t, pos0):
    d = u.shape[1]
    r0 = row0 // t
    s0 = layer * n_seq
    halo_spec = pl.BlockSpec((None, HALO, d), lambda s, i: (s0 + s, 0, 0))
    u_spec = pl.BlockSpec((t, d), lambda s, i: (r0 + s, 0))
    out_spec = pl.BlockSpec((t, d), lambda s, i: (s, 0))
    return _pool_call(ctx, halo_spec, u, u_spec, w_grp16, scale, (n_seq, 1), t, pos0, False,
                      n_seq * t, out_spec, "pool_sample")


def _split_bf16(x):
    hi = x.astype(_BF16)
    lo = (x - hi.astype(_F32)).astype(_BF16)
    return hi, lo


def _router_kernel(x_ref, w_ref, meta_ref, count_ref, carry_ref, *, tm, n_experts):
    i = pl.program_id(0)

    @pl.when(i == 0)
    def _():
        carry_ref[...] = jnp.zeros_like(carry_ref)

    xh, xl = _split_bf16(x_ref[...])
    wh, wl = _split_bf16(w_ref[...])
    logits = (jnp.dot(xh, wh, preferred_element_type=_F32)
              + (jnp.dot(xh, wl, preferred_element_type=_F32)
                 + jnp.dot(xl, wh, preferred_element_type=_F32)))
    lane = lax.broadcasted_iota(jnp.int32, (tm, LANES), 1)
    lg = jnp.where(lane < n_experts, logits, -jnp.inf)
    m1 = jnp.max(lg, axis=1, keepdims=True)
    i1 = jnp.min(jnp.where(lg == m1, lane, LANES), axis=1, keepdims=True)
    lg2 = jnp.where(lane == i1, -jnp.inf, lg)
    m2 = jnp.max(lg2, axis=1, keepdims=True)
    i2 = jnp.min(jnp.where(lg2 == m2, lane, LANES), axis=1, keepdims=True)
    e = jnp.exp(m2 - m1)
    g1 = 1.0 / (1.0 + e)
    g2 = e / (1.0 + e)

    sel1 = lane == i1
    sel2 = lane == i2
    cnt = jnp.where(sel1, 1.0, 0.0) + jnp.where(sel2, 1.0, 0.0)
    row = lax.broadcasted_iota(jnp.int32, (tm, tm), 0)
    col = lax.broadcasted_iota(jnp.int32, (tm, tm), 1)
    lower = jnp.where(col < row, 1.0, 0.0).astype(_BF16)
    before = jnp.dot(lower, cnt.astype(_BF16), preferred_element_type=_F32) + carry_ref[0:1, :]
    r1 = jnp.sum(jnp.where(sel1, before, 0.0), axis=1, keepdims=True)
    r2 = jnp.sum(jnp.where(sel2, before, 0.0), axis=1, keepdims=True)
    carry_ref[0:1, :] = carry_ref[0:1, :] + jnp.sum(cnt, axis=0, keepdims=True)
    count_ref[...] = carry_ref[...]

    meta = jnp.zeros((tm, LANES), _F32)
    for k, v in enumerate((i1.astype(_F32), i2.astype(_F32), r1, r2, g1, g2)):
        meta = jnp.where(lane == k, v, meta)
    meta_ref[...] = meta


def _router(x32, w_router):
    n, d = x32.shape
    n_experts = w_router.shape[1]
    tm = _tile(n, 256)
    w_pad = jnp.pad(w_router, ((0, 0), (0, LANES - n_experts)))
    blk = _nbytes((tm, d), _F32) + _nbytes((d, LANES), _F32) + 2 * _nbytes((tm, LANES), _F32)
    return pl.pallas_call(
        functools.partial(_router_kernel, tm=tm, n_experts=n_experts),
        grid=(n // tm,),
        in_specs=[pl.BlockSpec((tm, d), lambda i: (i, 0)),
                  pl.BlockSpec((d, LANES), lambda i: (0, 0))],
        out_specs=[pl.BlockSpec((tm, LANES), lambda i: (i, 0)),
                   pl.BlockSpec((8, LANES), lambda i: (0, 0))],
        out_shape=[jax.ShapeDtypeStruct((n, LANES), _F32), jax.ShapeDtypeStruct((8, LANES), _F32)],
        scratch_shapes=[pltpu.VMEM((8, LANES), _F32)],
        compiler_params=_params(("arbitrary",), blk, 4 * _nbytes((tm, d), _F32)),
        name="moe_router",
    )(x32, w_pad)


def _row_copy(src_hbm, row, dst_vmem, r, sem):
    return pltpu.make_async_copy(src_hbm.at[pl.ds(row, 1), :], dst_vmem.at[pl.ds(r, 1), :], sem)


def _gather_rows(src_hbm, idx_ref, base, dst_vmem, sem, n):
    def issue(i, c):
        for u in range(2):
            r = 2 * i + u
            _row_copy(src_hbm, idx_ref[base + r], dst_vmem, r, sem).start(priority=u)
        return c
    lax.fori_loop(0, n // 2, issue, 0, unroll=2)


def _wait_rows(src_hbm, dst_vmem, sem, n):
    def drain(r, c):
        _row_copy(src_hbm, 0, dst_vmem, r, sem).wait()
        return c
    lax.fori_loop(0, n, drain, 0, unroll=8)


def _moe_gather_kernel(tok_ref, nu_ref, x_hbm, xs_hbm, in_ref, out_ref, in_sem, out_sem, *, tg):
    n_used = nu_ref[0]

    def out_copy(b, slot):
        return pltpu.make_async_copy(out_ref.at[slot], xs_hbm.at[pl.ds(pl.multiple_of(b * tg, tg), tg), :],
                                     out_sem.at[slot])

    _gather_rows(x_hbm, tok_ref, 0, in_ref.at[0], in_sem.at[0], tg)

    def body(b, c):
        slot = b % 2

        @pl.when(b + 1 < n_used)
        def _():
            _gather_rows(x_hbm, tok_ref, (b + 1) * tg, in_ref.at[1 - slot], in_sem.at[1 - slot], tg)

        _wait_rows(x_hbm, in_ref.at[slot], in_sem.at[slot], tg)

        @pl.when(b >= 2)
        def _():
            out_copy(b - 2, slot).wait()

        out_ref[slot] = in_ref[slot].astype(out_ref.dtype)
        out_copy(b, slot).start()
        return c

    lax.fori_loop(0, n_used, body, 0)

    @pl.when(n_used >= 2)
    def _():
        out_copy(n_used - 2, n_used % 2).wait()

    out_copy(n_used - 1, (n_used - 1) % 2).wait()

    n_blocks = xs_hbm.shape[0] // tg
    out_ref[0] = jnp.zeros(out_ref.shape[1:], out_ref.dtype)

    def start_zero(b, c):
        out_copy(b, 0).start()
        return c

    def wait_zero(b, c):
        out_copy(b, 0).wait()
        return c

    lax.fori_loop(n_used, n_blocks, start_zero, 0)
    lax.fori_loop(n_used, n_blocks, wait_zero, 0)


def _moe_gather(x32, row_tok, n_used):
    rows = row_tok.shape[0]
    d = x32.shape[1]
    tg = MOE_BLOCK
    return pl.pallas_call(
        functools.partial(_moe_gather_kernel, tg=tg),
        grid_spec=pltpu.PrefetchScalarGridSpec(
            num_scalar_prefetch=2,
            grid=(1,),
            in_specs=[pl.BlockSpec(memory_space=pl.ANY)],
            out_specs=pl.BlockSpec(memory_space=pl.ANY),
            scratch_shapes=[pltpu.VMEM((2, tg, d), _F32), pltpu.VMEM((2, tg, d), _BF16),
                            pltpu.SemaphoreType.DMA((2,)), pltpu.SemaphoreType.DMA((2,))]),
        out_shape=jax.ShapeDtypeStruct((rows, d), _BF16),
        compiler_params=_params(("arbitrary",), 0, 2 * _nbytes((tg, d), _F32) + 2 * _nbytes((tg, d), _BF16)),
        name="moe_gather",
    )(row_tok, n_used, x32)


def _grouped_rows(gs_ref, gb_ref, e, src_hbm, dst_hbm, in_ref, out_ref, in_sem, out_sem, col, compute, tm, big):
    assert big in (1, 2, 4)
    r0 = gs_ref[e]
    nblk = gb_ref[e]
    n_main = nblk // big
    tails = [big >> k for k in range(1, big.bit_length())]

    def rows(off, nb):
        return pl.ds(pl.multiple_of(r0 + off * tm, tm), nb * tm)

    def in_copy(off, nb, slot):
        return pltpu.make_async_copy(src_hbm.at[rows(off, nb), :], in_ref.at[slot, pl.ds(0, nb * tm), :],
                                     in_sem.at[slot])

    def out_copy(off, nb, slot):
        return pltpu.make_async_copy(out_ref.at[slot, pl.ds(0, nb * tm), :], dst_hbm.at[rows(off, nb), col],
                                     out_sem.at[slot])

    def tail_off(t):
        return (nblk // (2 * t)) * (2 * t)

    def start_first_tail(first, slot, enable):
        pending = enable
        for t in tails[first:]:
            present = (nblk & t) != 0

            @pl.when(jnp.logical_and(pending, present))
            def _():
                in_copy(tail_off(t), t, slot).start(priority=ROW_DMA_PRIORITY)

            pending = jnp.logical_and(pending, jnp.logical_not(present))

    @pl.when(n_main > 0)
    def _():
        in_copy(0, big, 0).start(priority=ROW_DMA_PRIORITY)

    start_first_tail(0, 0, n_main == 0)

    def body(b, c):
        slot = b % 2
        in_copy(b * big, big, slot).wait()

        @pl.when(b + 1 < n_main)
        def _():
            in_copy((b + 1) * big, big, 1 - slot).start(priority=ROW_DMA_PRIORITY)

        start_first_tail(0, 1 - slot, b + 1 == n_main)

        @pl.when(b >= 2)
        def _():
            out_copy((b - 2) * big, big, slot).wait()

        out_ref[slot] = compute(in_ref[slot])
        out_copy(b * big, big, slot).start()
        return c

    lax.fori_loop(0, n_main, body, 0)

    @pl.when(n_main >= 2)
    def _():
        out_copy((n_main - 2) * big, big, n_main % 2).wait()

    @pl.when(n_main >= 1)
    def _():
        out_copy((n_main - 1) * big, big, (n_main - 1) % 2).wait()

    slot = n_main % 2
    tail_slots = []
    for k, t in enumerate(tails):
        present = (nblk & t) != 0
        tail_slots.append(slot)
        cur = slot

        @pl.when(present)
        def _():
            in_copy(tail_off(t), t, cur).wait()
            start_first_tail(k + 1, 1 - cur, True)
            out_ref[cur, 0:t * tm, :] = compute(in_ref[cur, 0:t * tm, :])
            out_copy(tail_off(t), t, cur).start()

        slot = jnp.where(present, 1 - slot, slot)

    for k, t in enumerate(tails):
        @pl.when((nblk & t) != 0)
        def _():
            out_copy(tail_off(t), t, tail_slots[k]).wait()


def _grouped_rows_ring(gs_ref, gb_ref, e, src_hbm, dst_hbm, in_ref, out_ref, in_sem, out_sem, col, compute, tm):
    r0 = gs_ref[e]
    nblk = gb_ref[e]

    def rows(b):
        return pl.ds(pl.multiple_of(r0 + b * tm, tm), tm)

    def in_copy(b, slot):
        return pltpu.make_async_copy(src_hbm.at[rows(b), :], in_ref.at[slot], in_sem.at[slot])

    def out_copy(b, slot):
        return pltpu.make_async_copy(out_ref.at[slot], dst_hbm.at[rows(b), col], out_sem.at[slot])

    for first in range(2):
        @pl.when(nblk > first)
        def _():
            in_copy(first, first).start()

    def body(b, c):
        slot = b % 3
        oslot = b % 2
        in_copy(b, slot).wait()

        @pl.when(b + 2 < nblk)
        def _():
            in_copy(b + 2, (b + 2) % 3).start()

        @pl.when(b >= 2)
        def _():
            out_copy(b - 2, oslot).wait()

        out_ref[oslot] = compute(in_ref[slot])
        out_copy(b, oslot).start()
        return c

    lax.fori_loop(0, nblk, body, 0)

    @pl.when(nblk >= 2)
    def _():
        out_copy(nblk - 2, nblk % 2).wait()

    @pl.when(nblk >= 1)
    def _():
        out_copy(nblk - 1, (nblk - 1) % 2).wait()


def _zero_rows(gs_ref, gb_ref, e, dst_hbm, out_ref, out_sem, col, tm):
    r0 = gs_ref[e]
    out_ref[0, 0:tm, :] = jnp.zeros((tm, out_ref.shape[2]), out_ref.dtype)

    def copy(b):
        return pltpu.make_async_copy(out_ref.at[0, pl.ds(0, tm), :],
                                     dst_hbm.at[pl.ds(pl.multiple_of(r0 + b * tm, tm), tm), col], out_sem.at[0])

    def start(b, c):
        copy(b).start()
        return c

    def wait(b, c):
        copy(b).wait()
        return c

    lax.fori_loop(0, gb_ref[e], start, 0)
    lax.fori_loop(0, gb_ref[e], wait, 0)


def _moe_gate_up_kernel(gs_ref, gb_ref, xs_hbm, wg_ref, wu_ref, h_hbm, wg16_ref, wu16_ref,
                        in_ref, out_ref, in_sem, out_sem, *, tm, tf, n_experts):
    j = pl.program_id(0)
    e = pl.program_id(1)
    col = pl.ds(pl.multiple_of(j * tf, tf), tf)

    def compute(x):
        g = jnp.dot(x, wg16_ref[...], preferred_element_type=_F32)
        u = jnp.dot(x, wu16_ref[...], preferred_element_type=_F32)
        return _swiglu(g, u).astype(_BF16)

    @pl.when(e < n_experts)
    def _():
        wg16_ref[...] = wg_ref[...].astype(_BF16)
        wu16_ref[...] = wu_ref[...].astype(_BF16)
        _grouped_rows(gs_ref, gb_ref, e, xs_hbm, h_hbm, in_ref, out_ref, in_sem, out_sem, col, compute, tm,
                      in_ref.shape[1] // tm)

    @pl.when(e == n_experts)
    def _():
        _zero_rows(gs_ref, gb_ref, e, h_hbm, out_ref, out_sem, col, tm)


def _moe_gate_up(xs16, w_gu, gstart, gblocks):
    rows, d = xs16.shape
    n_experts = w_gu.shape[0]
    f = w_gu.shape[2] // 2
    tm = MOE_BLOCK
    tf = _tile(f, 1024)
    nf = f // tf
    blk = 2 * _nbytes((d, tf), _F32)
    tc = MOE_CHUNK_BLOCKS * tm
    scratch_bytes = (2 * _nbytes((d, tf), _BF16) + 2 * _nbytes((tc, d), _BF16) + 2 * _nbytes((tc, tf), _BF16)
                     + 3 * _nbytes((tc, tf), _F32))
    last = n_experts - 1
    return pl.pallas_call(
        functools.partial(_moe_gate_up_kernel, tm=tm, tf=tf, n_experts=n_experts),
        grid_spec=pltpu.PrefetchScalarGridSpec(
            num_scalar_prefetch=2,
            grid=(nf, n_experts + 1),
            in_specs=[pl.BlockSpec(memory_space=pl.ANY),
                      pl.BlockSpec((None, d, tf), lambda j, e, gs, gb: (jnp.minimum(e, last), 0, j)),
                      pl.BlockSpec((None, d, tf), lambda j, e, gs, gb: (jnp.minimum(e, last), 0, nf + j))],
            out_specs=pl.BlockSpec(memory_space=pl.ANY),
            scratch_shapes=[pltpu.VMEM((d, tf), _BF16), pltpu.VMEM((d, tf), _BF16),
                            pltpu.VMEM((2, tc, d), _BF16), pltpu.VMEM((2, tc, tf), _BF16),
                            pltpu.SemaphoreType.DMA((2,)), pltpu.SemaphoreType.DMA((2,))]),
        out_shape=jax.ShapeDtypeStruct((rows, f), _BF16),
        compiler_params=_params(("arbitrary", "arbitrary"), blk, scratch_bytes),
        name="moe_gate_up",
    )(gstart, gblocks, xs16, w_gu, w_gu)


def _moe_down_kernel(gs_ref, gb_ref, h_hbm, w_ref, y_hbm, w16_ref, in_ref, out_ref, in_sem, out_sem,
                     *, tm, tn, n_experts):
    j = pl.program_id(0)
    e = pl.program_id(1)
    col = pl.ds(pl.multiple_of(j * tn, tn), tn)

    def compute(h):
        return jnp.dot(h, w16_ref[...], preferred_element_type=_F32)

    @pl.when(e < n_experts)
    def _():
        w16_ref[...] = w_ref[...].astype(_BF16)
        _grouped_rows_ring(gs_ref, gb_ref, e, h_hbm, y_hbm, in_ref, out_ref, in_sem, out_sem, col, compute, tm)

    @pl.when(e == n_experts)
    def _():
        _zero_rows(gs_ref, gb_ref, e, y_hbm, out_ref, out_sem, col, tm)


def _moe_down(h16, w_down, gstart, gblocks):
    rows, f = h16.shape
    n_experts = w_down.shape[0]
    d = w_down.shape[2]
    tm = MOE_BLOCK
    tn = _tile(d, 512)
    tc = tm
    blk = _nbytes((f, tn), _F32)
    scratch_bytes = (_nbytes((f, tn), _BF16) + 3 * _nbytes((tc, f), _BF16) + 3 * _nbytes((tc, tn), _F32))
    last = n_experts - 1
    return pl.pallas_call(
        functools.partial(_moe_down_kernel, tm=tm, tn=tn, n_experts=n_experts),
        grid_spec=pltpu.PrefetchScalarGridSpec(
            num_scalar_prefetch=2,
            grid=(d // tn, n_experts + 1),
            in_specs=[pl.BlockSpec(memory_space=pl.ANY),
                      pl.BlockSpec((None, f, tn), lambda j, e, gs, gb: (jnp.minimum(e, last), 0, j))],
            out_specs=pl.BlockSpec(memory_space=pl.ANY),
            scratch_shapes=[pltpu.VMEM((f, tn), _BF16),
                            pltpu.VMEM((3, tc, f), _BF16), pltpu.VMEM((2, tc, tn), _F32),
                            pltpu.SemaphoreType.DMA((3,)), pltpu.SemaphoreType.DMA((2,))]),
        out_shape=jax.ShapeDtypeStruct((rows, d), _F32),
        compiler_params=_params(("arbitrary", "arbitrary"), blk, scratch_bytes),
        name="moe_down",
    )(gstart, gblocks, h16, w_down)


def _moe_combine_kernel(d1_ref, d2_ref, y_hbm, meta_ref, res_ref, g_ref, b_ref, *rest, tc, alpha, n_split):
    outs, (buf_ref, sem) = rest[:-2], rest[-2:]
    i = pl.program_id(0)
    slot = i % 2

    def fetch(step, s):
        _gather_rows(y_hbm, d1_ref, step * tc, buf_ref.at[s, 0], sem.at[s], tc)
        _gather_rows(y_hbm, d2_ref, step * tc, buf_ref.at[s, 1], sem.at[s], tc)

    @pl.when(i == 0)
    def _():
        fetch(0, 0)

    @pl.when(i + 1 < pl.num_programs(0))
    def _():
        fetch(i + 1, 1 - slot)

    _wait_rows(y_hbm, buf_ref.at[slot, 0], sem.at[slot], tc)
    _wait_rows(y_hbm, buf_ref.at[slot, 1], sem.at[slot], tc)
    meta = meta_ref[...]
    f = buf_ref[slot, 0] * meta[:, 4:5] + buf_ref[slot, 1] * meta[:, 5:6]
    y = _res_ln(f, res_ref[...], g_ref[...], b_ref[...], alpha)
    if n_split is None:
        outs[0][...] = y
        outs[1][...] = y.astype(_BF16)
    else:
        @pl.when(i < n_split)
        def _():
            outs[0][...] = y

        @pl.when(i >= n_split)
        def _():
            outs[1][...] = y


def _moe_combine_ln(y_rows, dest1, dest2, meta, res, g, b, alpha, split_rows):
    n, d = res.shape
    tc = _tile(n if split_rows is None else math.gcd(n, split_rows), 128)
    blk = _nbytes((tc, LANES), _F32) + 2 * _nbytes((tc, d), _F32) + _nbytes((tc, d), _BF16)
    tile_spec = pl.BlockSpec((tc, d), lambda i, a, c: (i, 0))
    if split_rows is None:
        n_split = None
        out_specs = [tile_spec, tile_spec]
        out_shape = [jax.ShapeDtypeStruct((n, d), _F32), jax.ShapeDtypeStruct((n, d), _BF16)]
    else:
        n_split = split_rows // tc
        out_specs = [pl.BlockSpec((tc, d), lambda i, a, c: (jnp.minimum(i, n_split - 1), 0)),
                     pl.BlockSpec((tc, d), lambda i, a, c: (jnp.maximum(i - n_split, 0), 0))]
        out_shape = [jax.ShapeDtypeStruct((split_rows, d), _F32),
                     jax.ShapeDtypeStruct((n - split_rows, d), _F32)]
    return pl.pallas_call(
        functools.partial(_moe_combine_kernel, tc=tc, alpha=alpha, n_split=n_split),
        grid_spec=pltpu.PrefetchScalarGridSpec(
            num_scalar_prefetch=2,
            grid=(n // tc,),
            in_specs=[pl.BlockSpec(memory_space=pl.ANY),
                      pl.BlockSpec((tc, LANES), lambda i, a, c: (i, 0)),
                      tile_spec,
                      pl.BlockSpec((1, d), lambda i, a, c: (0, 0)),
                      pl.BlockSpec((1, d), lambda i, a, c: (0, 0))],
            out_specs=out_specs,
            scratch_shapes=[pltpu.VMEM((2, 2, tc, d), _F32), pltpu.SemaphoreType.DMA((2,))]),
        out_shape=out_shape,
        compiler_params=_params(("arbitrary",), blk, 8 * _nbytes((tc, d), _F32)),
        name="moe_combine_ln",
    )(dest1, dest2, y_rows, meta, res, g.reshape(1, d), b.reshape(1, d))


def _moe_layer(x32, w_router, w_gu, w_down, g, b, alpha, split_rows):
    n, d = x32.shape
    n_experts = w_router.shape[1]
    meta, counts = _router(x32, w_router)
    e1 = meta[:, 0].astype(jnp.int32)
    e2 = meta[:, 1].astype(jnp.int32)
    counts = counts[0, :n_experts].astype(jnp.int32)
    gblocks = (counts + MOE_BLOCK - 1) // MOE_BLOCK
    gend = jnp.cumsum(gblocks) * MOE_BLOCK
    gstart = gend - gblocks * MOE_BLOCK
    dest1 = gstart[e1] + meta[:, 2].astype(jnp.int32)
    dest2 = gstart[e2] + meta[:, 3].astype(jnp.int32)
    nb = (n * TOP_K) // MOE_BLOCK + n_experts
    rows = nb * MOE_BLOCK
    tok = jnp.arange(n, dtype=jnp.int32)
    row_tok = jnp.zeros((rows,), jnp.int32).at[dest1].set(tok).at[dest2].set(tok)
    n_used = (gend[-1] // MOE_BLOCK).astype(jnp.int32).reshape(1)
    gstart = jnp.concatenate([gstart, gend[-1:]]).astype(jnp.int32)
    gblocks = jnp.concatenate([gblocks, nb - n_used]).astype(jnp.int32)

    xs16 = _moe_gather(x32, row_tok, n_used)
    h16 = _moe_gate_up(xs16, w_gu, gstart, gblocks)
    y_rows = _moe_down(h16, w_down, gstart, gblocks)
    return _moe_combine_ln(y_rows, dest1, dest2, meta, x32, g, b, alpha, split_rows)


def kernel(x_prompt, x_sample, cache_k, cache_v, state_pool, attn_w_in, attn_w_o, attn_lambda, attn_subln_g, pool_w_in, pool_w_grp, pool_scale, pool_w_o, ln_mix_g, ln_mix_b, ln_ffn_g, ln_ffn_b, ffn_w_gu, ffn_w_down, moe_w_router, moe_w_gu, moe_w_down):
    batch, seq, d = x_prompt.shape
    n_seq, t, _ = x_sample.shape
    past = cache_k.shape[2]
    depth = ln_mix_g.shape[0]
    alpha = (2 * depth) ** 0.25
    n_p = batch * seq
    n_s = n_seq * t
    n = n_p + n_s
    n_heads = cache_k.shape[3]
    qk_width = n_heads * 2 * HEAD_DIM
    v_width = n_heads * V_DIM

    x32 = jnp.concatenate([x_prompt.reshape(n_p, d), x_sample.reshape(n_s, d)], axis=0)
    x16 = x32.astype(_BF16)
    tab_p = _rope_tables(np.arange(seq))
    tab_s = _rope_tables(np.tile(past + np.arange(t), n_seq))
    past_k = cache_k.reshape(-1, past, n_heads, 2 * HEAD_DIM)
    past_v = cache_v.reshape(-1, past, n_heads, V_DIM)
    ctx = jnp.pad(state_pool, ((0, 0), (0, 0), (HALO - POOL_CTX, 0), (0, 0))).reshape(-1, HALO, d)

    kp_l, vp_l, ks_l, vs_l, pp_l, ps_l = [], [], [], [], [], []
    y_p = y_s = None
    for i in range(depth):
        j = i // 2
        last = i == depth - 1
        if i % 2 == 0:
            lam_init = 0.8 - 0.6 * math.exp(-0.3 * i)
            w_in16 = attn_w_in[j].astype(_BF16)
            g = attn_subln_g[j].reshape(1, V_DIM)
            rows_p, rows_s = (0, n_p, tab_p), (n_p, n_s, tab_s)
            (q_p,), (q_s,) = (_proj(x16, w_in16, 0, qk_width, r0, nr, tab, False, True, "proj_q", Q_SCALE)
                              for r0, nr, tab in (rows_p, rows_s))
            (k32_p, k_p), (k32_s, k_s) = (_proj(x16, w_in16, qk_width, qk_width, r0, nr, tab, True, True, "proj_k")
                                          for r0, nr, tab in (rows_p, rows_s))
            (v32_p, v_p), (v32_s, v_s) = (_proj(x16, w_in16, 2 * qk_width, v_width, r0, nr, None, True, True, "proj_v")
                                          for r0, nr, tab in (rows_p, rows_s))
            mix = [_attn_prompt(q_p, k_p, v_p, attn_lambda[j], g, batch, seq, lam_init),
                   _attn_sample(q_s, k_s, v_s, past_k, past_v, j, attn_lambda[j], g, n_seq, t, lam_init)]
            w_o16 = attn_w_o[j].astype(_BF16)
            kp_l.append(k32_p.reshape(batch, seq, n_heads, 2 * HEAD_DIM))
            vp_l.append(v32_p.reshape(batch, seq, n_heads, V_DIM))
            ks_l.append(k32_s.reshape(n_seq, t, n_heads, 2 * HEAD_DIM))
            vs_l.append(v32_s.reshape(n_seq, t, n_heads, V_DIM))
        else:
            u32, = _proj(x16, pool_w_in[j].astype(_BF16), 0, d, 0, n, None, True, False, "pool_in")
            w_grp16 = pool_w_grp[j].astype(_BF16)
            mix = [_pool_prompt(u32, w_grp16, pool_scale[j], batch, seq),
                   _pool_sample(u32, ctx, j, w_grp16, pool_scale[j], n_p, n_seq, t, past)]
            w_o16 = pool_w_o[j].astype(_BF16)
            u_p = u32[:n_p].reshape(batch, seq, d)
            u_s = u32[n_p:].reshape(n_seq, t, d)
            pp_l.append(jnp.concatenate([jnp.zeros((batch, POOL_CTX, d), _F32), u_p[:, -POOL_CTX:]], 1)[:, -POOL_CTX:])
            ps_l.append(jnp.concatenate([state_pool[j], u_s[:, -POOL_CTX:]], 1)[:, -POOL_CTX:])
        x32, x16 = _mm_res_ln(mix, w_o16, x32, ln_mix_g[i], ln_mix_b[i], alpha, "mix_out_ln")
        if i % 2 == 0:
            h16 = _gate_up(x16, ffn_w_gu[j].astype(_BF16))
            x32, x16 = _mm_res_ln([h16], ffn_w_down[j].astype(_BF16), x32, ln_ffn_g[i], ln_ffn_b[i],
                                  alpha, "ffn_down_ln")
        elif last:
            y_p, y_s = _moe_layer(x32, moe_w_router[j], moe_w_gu[j], moe_w_down[j],
                                  ln_ffn_g[i], ln_ffn_b[i], alpha, n_p)
        else:
            x32, x16 = _moe_layer(x32, moe_w_router[j], moe_w_gu[j], moe_w_down[j],
                                  ln_ffn_g[i], ln_ffn_b[i], alpha, None)
    if y_p is None:
        y_p, y_s = x32[:n_p], x32[n_p:]
    return (y_p.reshape(batch, seq, d), y_s.reshape(n_seq, t, d), jnp.stack(kp_l), jnp.stack(vp_l),
            jnp.stack(pp_l), jnp.stack(ks_l), jnp.stack(vs_l), jnp.stack(ps_l))
```
